```python
import jax, jax.numpy as jnp
from jax import lax
import numpy as np

D_MODEL = 1024
BATCH = 16
SEQ = 2048
DEPTH = 1

RNN_WIDTH = 1024
RNN_BLOCKS = 8
RNN_BLOCK_W = RNN_WIDTH // RNN_BLOCKS
CONV_WIDTH = 4
LRU_C = 8.0
GLA_HEADS = 4
GLA_DK = D_MODEL // 2
GLA_DV = D_MODEL
GLA_HEAD_K = GLA_DK // GLA_HEADS
GLA_HEAD_V = GLA_DV // GLA_HEADS
GLA_RANK = 16
GLA_TAU = 16.0
GLA_CHUNK = 64
N_GROUPS = 4
EXPERTS_PER_GROUP = 8
N_EXPERTS = N_GROUPS * EXPERTS_PER_GROUP
TOP_K = 2
EXPERT_FF = 512
MOE_BLOCK = 128
DN_ALPHA = (2.0 * DEPTH) ** 0.25
DN_BETA = (8.0 * DEPTH) ** -0.25
LN_EPS = 1e-5
RMS_EPS = 1e-6
IN_SPLITS = (RNN_WIDTH, RNN_WIDTH, GLA_DK, GLA_DK, GLA_DV, GLA_DV, GLA_RANK, D_MODEL, D_MODEL)
IN_COL_BETA = (DN_BETA, 1.0, 1.0, 1.0, DN_BETA, 1.0, 1.0, 1.0, 1.0)
IN_COLS = sum(IN_SPLITS)
SPLIT_POINTS = tuple(np.cumsum(IN_SPLITS)[:-1].tolist())

kernel_name = "hawk_gla_hier_moe_deepnorm_block"


def layer_norm(x, g, b):
    xf = x.astype(jnp.float32)
    mu = jnp.mean(xf, axis=-1, keepdims=True)
    var = jnp.mean(jnp.square(xf - mu), axis=-1, keepdims=True)
    y = (xf - mu) * lax.rsqrt(var + LN_EPS) * g.astype(jnp.float32) + b.astype(jnp.float32)
    return y.astype(x.dtype)


def causal_depthwise_conv(u, w, b):
    S = u.shape[1]
    up = jnp.pad(u, ((0, 0), (CONV_WIDTH - 1, 0), (0, 0)))
    out = b
    for k in range(CONV_WIDTH):
        out = out + w[k] * up[:, k:k + S]
    return out


def _linear_combine(c1, c2):
    a1, b1 = c1
    a2, b2 = c2
    return a1 * a2, a2 * b1 + b2


def rg_lru(u, w_a, b_a, w_x, b_x, lam):
    B, S, _ = u.shape
    ub = u.reshape(B, S, RNN_BLOCKS, RNN_BLOCK_W)
    r = jax.nn.sigmoid(jnp.einsum('bshi,hij->bshj', ub, w_a).reshape(B, S, RNN_WIDTH) + b_a)
    i = jax.nn.sigmoid(jnp.einsum('bshi,hij->bshj', ub, w_x).reshape(B, S, RNN_WIDTH) + b_x)
    log_a = -LRU_C * r.astype(jnp.float32) * jax.nn.softplus(-lam.astype(jnp.float32))
    a = jnp.exp(log_a)
    mult = jnp.sqrt(-jnp.expm1(2.0 * log_a))
    bvals = mult * (i * u).astype(jnp.float32)
    _, h = lax.associative_scan(_linear_combine, (a, bvals), axis=1)
    return h.astype(u.dtype)


def gla_chunked(q, k, v, log_alpha):
    B, S, H, K = q.shape
    V = v.shape[-1]
    C = GLA_CHUNK
    n = S // C
    q = q.reshape(B, n, C, H, K)
    k = k.reshape(B, n, C, H, K)
    v = v.reshape(B, n, C, H, V)
    bcum = jnp.cumsum(log_alpha.reshape(B, n, C, H, K), axis=2)
    b_last = bcum[:, :, -1:]
    q_dec = q * jnp.exp(bcum)
    k_inv = k * jnp.exp(-bcum)
    k_end = k * jnp.exp(b_last - bcum)
    mask = jnp.tril(jnp.ones((C, C), dtype=bool))
    scores = jnp.einsum('bnthk,bnshk->bnhts', q_dec, k_inv)
    scores = jnp.where(mask, scores, 0.0)
    o_intra = jnp.einsum('bnhts,bnshv->bnthv', scores, v)
    chunk_decay = jnp.exp(b_last[:, :, 0])

    def step(state, inp):
        qd, ke, vc, dec = inp
        o = jnp.einsum('bthk,bhkv->bthv', qd, state)
        state = dec[..., None] * state + jnp.einsum('bshk,bshv->bhkv', ke, vc)
        return state, o

    xs = (jnp.moveaxis(q_dec, 1, 0), jnp.moveaxis(k_end, 1, 0),
          jnp.moveaxis(v, 1, 0), jnp.moveaxis(chunk_decay, 1, 0))
    state0 = jnp.zeros((B, H, K, V), jnp.float32)
    _, o_inter = lax.scan(step, state0, xs)
    o = o_intra + jnp.moveaxis(o_inter, 0, 1)
    return o.reshape(B, S, H, V)


def hierarchical_moe(xf, w_rg, b_rg, w_re, b_re, w1, w3, w2):
    N, D = xf.shape
    g_logits = (xf @ w_rg + b_rg).astype(jnp.float32)
    p_group = jax.nn.softmax(g_logits, axis=-1)
    grp = jnp.argmax(g_logits, axis=-1).astype(jnp.int32)
    p_grp_sel = jnp.take_along_axis(p_group, grp[:, None], axis=1)[:, 0]
    e_logits = (xf @ w_re + b_re).astype(jnp.float32).reshape(N, N_GROUPS, EXPERTS_PER_GROUP)
    e_sel = jnp.take_along_axis(e_logits, grp[:, None, None], axis=1)[:, 0]
    p_exp = jax.nn.softmax(e_sel, axis=-1)
    top_p, top_i = lax.top_k(p_exp, TOP_K)
    top_p = top_p / jnp.sum(top_p, axis=-1, keepdims=True)
    weights = p_grp_sel[:, None] * top_p
    expert_id = grp[:, None] * EXPERTS_PER_GROUP + top_i.astype(jnp.int32)

    e_flat = expert_id.reshape(-1)
    w_flat = weights.reshape(-1)
    tok_flat = jnp.repeat(jnp.arange(N, dtype=jnp.int32), TOP_K)
    order = jnp.argsort(e_flat)
    e_s, w_s, tok_s = e_flat[order], w_flat[order], tok_flat[order]
    counts = jnp.zeros((N_EXPERTS,), jnp.int32).at[e_flat].add(1)
    offsets = jnp.cumsum(counts) - counts
    pcounts = (counts + MOE_BLOCK - 1) // MOE_BLOCK * MOE_BLOCK
    pend = jnp.cumsum(pcounts)
    poffsets = pend - pcounts
    rank = jnp.arange(N * TOP_K, dtype=jnp.int32) - offsets[e_s]
    dest = poffsets[e_s] + rank
    n_blocks = -(-(N * TOP_K) // MOE_BLOCK) + N_EXPERTS
    P = n_blocks * MOE_BLOCK
    buf_tok = jnp.zeros((P,), jnp.int32).at[dest].set(tok_s)
    buf_w = jnp.zeros((P,), jnp.float32).at[dest].set(w_s)
    block_start = jnp.arange(n_blocks, dtype=jnp.int32) * MOE_BLOCK
    blk_e = jnp.minimum(jnp.searchsorted(pend, block_start, side='right'), N_EXPERTS - 1)

    def run_block(args):
        tok_b, w_b, e_b = args
        xb = xf[tok_b]
        h = jax.nn.silu(xb @ w1[e_b]) * (xb @ w3[e_b])
        return (h @ w2[e_b]) * w_b[:, None]

    y = lax.map(run_block, (buf_tok.reshape(n_blocks, MOE_BLOCK),
                            buf_w.reshape(n_blocks, MOE_BLOCK).astype(xf.dtype), blk_e))
    return jax.ops.segment_sum(y.reshape(P, D), buf_tok, num_segments=N)


def hybrid_layer(x, w_in, b_in, conv_w, conv_b, rg_w_a, rg_b_a, rg_w_x, rg_b_x, rg_lambda,
                 gla_w_a2, gla_b_a, gla_norm_g, w_proj_rnn, w_proj_gla, w_o, b_o,
                 ln1_g, ln1_b, router_w_group, router_b_group, router_w_expert,
                 router_b_expert, exp_w1, exp_w3, exp_w2, ln2_g, ln2_b):
    B, S, D = x.shape
    proj = x @ w_in + b_in
    rx, ry, q, k, v, g, alr, ga, gb = jnp.split(proj, SPLIT_POINTS, axis=-1)

    u = causal_depthwise_conv(rx, conv_w, conv_b)
    h = rg_lru(u, rg_w_a, rg_b_a, rg_w_x, rg_b_x, rg_lambda)
    out_a = (h * jax.nn.gelu(ry)) @ w_proj_rnn

    log_alpha = jax.nn.log_sigmoid((alr @ gla_w_a2 + gla_b_a).astype(jnp.float32)) / GLA_TAU
    qh = q.reshape(B, S, GLA_HEADS, GLA_HEAD_K).astype(jnp.float32) * (GLA_HEAD_K ** -0.5)
    kh = k.reshape(B, S, GLA_HEADS, GLA_HEAD_K).astype(jnp.float32)
    vh = v.reshape(B, S, GLA_HEADS, GLA_HEAD_V).astype(jnp.float32)
    la = log_alpha.reshape(B, S, GLA_HEADS, GLA_HEAD_K)
    o = gla_chunked(qh, kh, vh, la)
    o = o * lax.rsqrt(jnp.mean(jnp.square(o), axis=-1, keepdims=True) + RMS_EPS)
    o = o * gla_norm_g.astype(jnp.float32).reshape(GLA_HEADS, GLA_HEAD_V)
    o = o.reshape(B, S, GLA_DV).astype(x.dtype) * jax.nn.silu(g)
    out_b = o @ w_proj_gla

    merged = jax.nn.sigmoid(ga) * out_a + jax.nn.sigmoid(gb) * out_b
    y_mix = merged @ w_o + b_o
    x1 = layer_norm(DN_ALPHA * x + y_mix, ln1_g, ln1_b)

    y_moe = hierarchical_moe(x1.reshape(B * S, D), router_w_group, router_b_group,
                             router_w_expert, router_b_expert, exp_w1, exp_w3, exp_w2)
    x2 = layer_norm(DN_ALPHA * x1 + y_moe.reshape(B, S, D), ln2_g, ln2_b)
    return x2


def setup_inputs(seed: int = 0) -> dict:
    key = jax.random.key(seed)
    ks = jax.random.split(key, 28)
    L = DEPTH
    D = D_MODEL

    def nrm(k, shape, scale):
        return jax.random.normal(k, shape, jnp.float32) * scale

    col_scale = jnp.concatenate([jnp.full((n,), s, jnp.float32) for n, s in zip(IN_SPLITS, IN_COL_BETA)])
    a_c = jax.random.uniform(ks[9], (L, RNN_WIDTH), jnp.float32, minval=0.9, maxval=0.999)
    a0 = a_c ** (1.0 / LRU_C)
    return {
        "x": nrm(ks[0], (BATCH, SEQ, D), 1.0),
        "w_in": nrm(ks[1], (L, D, IN_COLS), D ** -0.5) * col_scale,
        "b_in": nrm(ks[2], (L, IN_COLS), 0.02),
        "conv_w": nrm(ks[3], (L, CONV_WIDTH, RNN_WIDTH), CONV_WIDTH ** -0.5),
        "conv_b": nrm(ks[4], (L, RNN_WIDTH), 0.02),
        "rg_w_a": nrm(ks[5], (L, RNN_BLOCKS, RNN_BLOCK_W, RNN_BLOCK_W), RNN_BLOCK_W ** -0.5),
        "rg_b_a": nrm(ks[6], (L, RNN_WIDTH), 0.02),
        "rg_w_x": nrm(ks[7], (L, RNN_BLOCKS, RNN_BLOCK_W, RNN_BLOCK_W), RNN_BLOCK_W ** -0.5),
        "rg_b_x": nrm(ks[8], (L, RNN_WIDTH), 0.02),
        "rg_lambda": jnp.log(a0) - jnp.log1p(-a0),
        "gla_w_a2": nrm(ks[10], (L, GLA_RANK, GLA_DK), GLA_RANK ** -0.5),
        "gla_b_a": nrm(ks[11], (L, GLA_DK), 0.1),
        "gla_norm_g": 1.0 + nrm(ks[12], (L, GLA_DV), 0.02),
        "w_proj_rnn": nrm(ks[13], (L, RNN_WIDTH, D), RNN_WIDTH ** -0.5 * DN_BETA),
        "w_proj_gla": nrm(ks[14], (L, GLA_DV, D), GLA_DV ** -0.5 * DN_BETA),
        "w_o": nrm(ks[15], (L, D, D), D ** -0.5 * DN_BETA),
        "b_o": nrm(ks[16], (L, D), 0.02),
        "ln1_g": 1.0 + nrm(ks[17], (L, D), 0.02),
        "ln1_b": nrm(ks[18], (L, D), 0.02),
        "router_w_group": nrm(ks[19], (L, D, N_GROUPS), D ** -0.5),
        "router_b_group": nrm(ks[20], (L, N_GROUPS), 0.01),
        "router_w_expert": nrm(ks[21], (L, D, N_EXPERTS), D ** -0.5),
        "router_b_expert": nrm(ks[22], (L, N_EXPERTS), 0.01),
        "exp_w1": nrm(ks[23], (L, N_EXPERTS, D, EXPERT_FF), D ** -0.5 * DN_BETA),
        "exp_w3": nrm(ks[24], (L, N_EXPERTS, D, EXPERT_FF), D ** -0.5 * DN_BETA),
        "exp_w2": nrm(ks[25], (L, N_EXPERTS, EXPERT_FF, D), EXPERT_FF ** -0.5 * DN_BETA),
        "ln2_g": 1.0 + nrm(ks[26], (L, D), 0.02),
        "ln2_b": nrm(ks[27], (L, D), 0.02),
    }


def reference(x, w_in, b_in, conv_w, conv_b, rg_w_a, rg_b_a, rg_w_x, rg_b_x, rg_lambda,
              gla_w_a2, gla_b_a, gla_norm_g, w_proj_rnn, w_proj_gla, w_o, b_o,
              ln1_g, ln1_b, router_w_group, router_b_group, router_w_expert,
              router_b_expert, exp_w1, exp_w3, exp_w2, ln2_g, ln2_b):
    h = x
    for l in range(DEPTH):
        h = hybrid_layer(h, w_in[l], b_in[l], conv_w[l], conv_b[l], rg_w_a[l], rg_b_a[l],
                         rg_w_x[l], rg_b_x[l], rg_lambda[l], gla_w_a2[l], gla_b_a[l],
                         gla_norm_g[l], w_proj_rnn[l], w_proj_gla[l], w_o[l], b_o[l],
                         ln1_g[l], ln1_b[l], router_w_group[l], router_b_group[l],
                         router_w_expert[l], router_b_expert[l], exp_w1[l], exp_w3[l],
                         exp_w2[l], ln2_g[l], ln2_b[l])
    return h
```

```python
import functools

import jax
import jax.numpy as jnp
from jax import lax
from jax.experimental import pallas as pl
from jax.experimental.pallas import tpu as pltpu

F32 = jnp.float32
BF16 = jnp.bfloat16

D_MODEL = 1024
RNN_WIDTH = 1024
RNN_BLOCKS = 8
RNN_BLOCK_W = RNN_WIDTH // RNN_BLOCKS
CONV_WIDTH = 4
LRU_C = 8.0
GLA_HEADS = 4
GLA_DK = D_MODEL // 2
GLA_DV = D_MODEL
GLA_HEAD_K = GLA_DK // GLA_HEADS
GLA_HEAD_V = GLA_DV // GLA_HEADS
GLA_RANK = 16
GLA_TAU = 16.0
GLA_CHUNK = 64
N_GROUPS = 4
EXPERTS_PER_GROUP = 8
N_EXPERTS = N_GROUPS * EXPERTS_PER_GROUP
EXPERT_FF = 512
DN_ALPHA = 2.0 ** 0.25
LN_EPS = 1e-5
RMS_EPS = 1e-6

LANES = 128
SUBLANES = 8
VMEM_LIMIT = 56 * 1024 * 1024

C_RX, C_RY, C_Q, C_K, C_V, C_G, C_GA, C_GB, C_ALR = 0, 1024, 2048, 2560, 3072, 4096, 5120, 6144, 7168
IN_COLS_PACKED = C_ALR + LANES

MIX_TILE = 256
ROUTE_TILE = 256
MOE_ROWS = 256
DISPATCH_TILE = 1024
COMBINE_TILE = 256
DMA_LAG = 64
ROUTE_ROWS = 8 + N_EXPERTS


def _sigmoid(v):
    return 1.0 / (1.0 + jnp.exp(-v))


def _softplus(v):
    return jnp.maximum(v, 0.0) + jnp.log1p(jnp.exp(-jnp.abs(v)))


def _layer_norm(v, g, b):
    mu = jnp.mean(v, axis=-1, keepdims=True)
    c = v - mu
    var = jnp.mean(c * c, axis=-1, keepdims=True)
    return c * lax.rsqrt(var + LN_EPS) * g + b


def _dot(a, b):
    return jnp.dot(a, b, preferred_element_type=F32)


def _dot_nt(a, b):
    return lax.dot_general(a, b, (((1,), (1,)), ((), ())), preferred_element_type=F32)


def _dot_tn(a, b):
    return lax.dot_general(a, b, (((0,), (0,)), ((), ())), preferred_element_type=F32)


def _const_spec(shape):
    nd = len(shape)
    return pl.BlockSpec(shape, lambda *_: (0,) * nd, pipeline_mode=pl.Buffered(1))


def _mixer_kernel(x_ref, win_ref, bin_ref, convw_ref, convb_ref, wgate_ref, bgate_ref, lam_ref,
                  wa2_ref, ba2_ref, gnorm_ref, wprnn_ref, wpgla_ref, wo_ref, bo_ref, ln1g_ref, ln1b_ref,
                  x1_ref, xpk_ref, rxbuf, hcar, st_ref, obuf, *, tile):
    T = tile

    @pl.when(pl.program_id(1) == 0)
    def _():
        rxbuf[0:SUBLANES, :] = jnp.zeros((SUBLANES, RNN_WIDTH), F32)
        hcar[...] = jnp.zeros_like(hcar)
        st_ref[...] = jnp.zeros_like(st_ref)

    x = x_ref[0]
    xb = x.astype(BF16)

    def proj(c0, c1):
        return _dot(xb, win_ref[:, c0:c1]) + bin_ref[:, c0:c1]

    rx = proj(C_RX, C_RX + RNN_WIDTH)
    rxbuf[SUBLANES:SUBLANES + T, :] = rx
    u = convb_ref[...] + convw_ref[CONV_WIDTH - 1:CONV_WIDTH, :] * rx
    for j in range(1, CONV_WIDTH):
        u = u + convw_ref[CONV_WIDTH - 1 - j:CONV_WIDTH - j, :] * rxbuf[SUBLANES - j:SUBLANES - j + T, :]
    rxbuf[0:SUBLANES, :] = rxbuf[T:T + SUBLANES, :]

    r_parts, i_parts = [], []
    for p in range(RNN_BLOCKS // 2):
        up = u[:, 256 * p:256 * (p + 1)].astype(BF16)
        gp = _dot(up, wgate_ref[p])
        r_parts.append(gp[:, :256])
        i_parts.append(gp[:, 256:])
    r = _sigmoid(jnp.concatenate(r_parts, axis=1) + bgate_ref[:, :RNN_WIDTH])
    ig = _sigmoid(jnp.concatenate(i_parts, axis=1) + bgate_ref[:, RNN_WIDTH:])
    log_a = (-LRU_C) * r * _softplus(-lam_ref[...])
    a = jnp.exp(log_a)
    y2 = 2.0 * log_a
    series = y2 * (1.0 + y2 * (0.5 + y2 * (1.0 / 6.0 + y2 * (1.0 / 24.0 + y2 * (1.0 / 120.0)))))
    em1 = jnp.where(y2 > -0.03125, series, a * a - 1.0)
    bv = jnp.sqrt(-em1) * (ig * u)

    rows = lax.broadcasted_iota(jnp.int32, (T, RNN_WIDTH), 0)
    sa, sb = a, bv
    s = 1
    while s < T:
        keep = rows >= s
        sb = sa * jnp.where(keep, pltpu.roll(sb, s, 0), 0.0) + sb
        sa = sa * jnp.where(keep, pltpu.roll(sa, s, 0), 1.0)
        s *= 2
    h = sb + sa * hcar[0:1, :]
    hcar[0:1, :] = h[T - 1:T, :]

    ry = proj(C_RY, C_RY + RNN_WIDTH)
    out_a = _dot((h * jax.nn.gelu(ry)).astype(BF16), wprnn_ref[...])

    qk = proj(C_Q, C_Q + 2 * GLA_DK)
    q = qk[:, :GLA_DK] * (GLA_HEAD_K ** -0.5)
    k = qk[:, GLA_DK:]
    v = proj(C_V, C_V + GLA_DV)
    alr = proj(C_ALR, C_ALR + LANES)
    z = _dot(alr.astype(BF16), wa2_ref[...]) + ba2_ref[...]
    la = -_softplus(-z) * (1.0 / GLA_TAU)

    ri = lax.broadcasted_iota(jnp.int32, (T, T), 0)
    ci = lax.broadcasted_iota(jnp.int32, (T, T), 1)
    chunk_start = (ri >> 6) << 6
    tri = jnp.where((ci <= ri) & (ci >= chunk_start), 1.0, 0.0).astype(BF16)
    la_hi = la.astype(BF16)
    la_lo = (la - la_hi.astype(F32)).astype(BF16)
    bcum = _dot(tri, la_hi) + _dot(tri, la_lo)

    cr = lax.broadcasted_iota(jnp.int32, (GLA_CHUNK, GLA_CHUNK), 0)
    cc = lax.broadcasted_iota(jnp.int32, (GLA_CHUNK, GLA_CHUNK), 1)
    causal = cr >= cc
    for c in range(T // GLA_CHUNK):
        r0 = c * GLA_CHUNK
        bc = bcum[r0:r0 + GLA_CHUNK, :]
        bl = bcum[r0 + GLA_CHUNK - 1:r0 + GLA_CHUNK, :]
        qd_all = q[r0:r0 + GLA_CHUNK, :] * jnp.exp(bc)
        kc = k[r0:r0 + GLA_CHUNK, :]
        ki_all = kc * jnp.exp(-bc)
        ke_all = kc * jnp.exp(bl - bc)
        dec_all = jnp.exp(bl)
        for hd in range(GLA_HEADS):
            ks = slice(hd * GLA_HEAD_K, (hd + 1) * GLA_HEAD_K)
            vs = slice(hd * GLA_HEAD_V, (hd + 1) * GLA_HEAD_V)
            qd = qd_all[:, ks].astype(BF16)
            ki = ki_all[:, ks].astype(BF16)
            ke = ke_all[:, ks].astype(BF16)
            vv = v[r0:r0 + GLA_CHUNK, vs].astype(BF16)
            sc = jnp.where(causal, _dot_nt(qd, ki), 0.0)
            st = st_ref[hd]
            o = _dot(sc.astype(BF16), vv) + _dot_nt(qd, st.astype(BF16))
            obuf[r0:r0 + GLA_CHUNK, vs] = o
            st_ref[hd] = st * dec_all[:, ks] + _dot_tn(vv, ke)

    g = proj(C_G, C_G + GLA_DV)
    o_parts = []
    for hd in range(GLA_HEADS):
        vs = slice(hd * GLA_HEAD_V, (hd + 1) * GLA_HEAD_V)
        oh = obuf[:, vs]
        ms = jnp.mean(oh * oh, axis=-1, keepdims=True)
        o_parts.append(oh * lax.rsqrt(ms + RMS_EPS) * gnorm_ref[:, vs])
    on = jnp.concatenate(o_parts, axis=1) * (g * _sigmoid(g))
    out_b = _dot(on.astype(BF16), wpgla_ref[...])

    ga = proj(C_GA, C_GA + D_MODEL)
    gb = proj(C_GB, C_GB + D_MODEL)
    merged = _sigmoid(ga) * out_a + _sigmoid(gb) * out_b
    y = _dot(merged.astype(BF16), wo_ref[...]) + bo_ref[...]
    x1 = _layer_norm(DN_ALPHA * x + y, ln1g_ref[...], ln1b_ref[...])
    x1_ref[0] = x1

    bits = pltpu.bitcast(x1.astype(BF16).astype(F32), jnp.uint32)
    half = D_MODEL // 2
    xpk_ref[0] = (bits[:, :half] >> 16) | bits[:, half:]


def _mixer(x, win, b_in, conv_w, conv_b, wgate, bgate, lam, wa2, ba2, gnorm, wprnn, wpgla, wo, bo, ln1g, ln1b):
    B, S, _ = x.shape
    T = min(MIX_TILE, S)
    assert S % T == 0 and T % GLA_CHUNK == 0
    weights = (win, b_in, conv_w, conv_b, wgate, bgate, lam, wa2, ba2, gnorm, wprnn, wpgla, wo, bo, ln1g, ln1b)
    return pl.pallas_call(
        functools.partial(_mixer_kernel, tile=T),
        grid=(B, S // T),
        in_specs=[pl.BlockSpec((1, T, D_MODEL), lambda b, s: (b, s, 0))] + [_const_spec(w.shape) for w in weights],
        out_specs=[pl.BlockSpec((1, T, D_MODEL), lambda b, s: (b, s, 0)),
                   pl.BlockSpec((1, T, D_MODEL // 2), lambda b, s: (b, s, 0))],
        out_shape=[jax.ShapeDtypeStruct((B, S, D_MODEL), F32),
                   jax.ShapeDtypeStruct((B, S, D_MODEL // 2), jnp.uint32)],
        scratch_shapes=[pltpu.VMEM((T + SUBLANES, RNN_WIDTH), F32),
                        pltpu.VMEM((SUBLANES, RNN_WIDTH), F32),
                        pltpu.VMEM((GLA_HEADS, GLA_HEAD_V, GLA_HEAD_K), F32),
                        pltpu.VMEM((T, GLA_DV), F32)],
        compiler_params=pltpu.CompilerParams(dimension_semantics=("arbitrary", "arbitrary"),
                                             vmem_limit_bytes=VMEM_LIMIT),
        name="mixer",
    )(x, *weights)


def _router_kernel(x1_ref, wr_ref, br_ref, info_ref, wts_ref, cnt_ref, *, tile):
    T = tile
    logits = _dot_nt(wr_ref[...], x1_ref[...].astype(BF16)) + br_ref[...]
    row8 = lax.broadcasted_iota(jnp.int32, (SUBLANES, T), 0)
    row8f = row8.astype(F32)
    neg = jnp.float32(-jnp.inf)
    first = lambda hit: jnp.min(jnp.where(hit, row8f, float(SUBLANES)), axis=0, keepdims=True)

    gl = jnp.where(row8 < N_GROUPS, logits[0:SUBLANES, :], neg)
    gmax = jnp.max(gl, axis=0, keepdims=True)
    grp = first(gl == gmax)
    p_grp = 1.0 / jnp.sum(jnp.exp(gl - gmax), axis=0, keepdims=True)

    e_sel = jnp.zeros((EXPERTS_PER_GROUP, T), F32)
    for gi in range(N_GROUPS):
        lo = SUBLANES + gi * EXPERTS_PER_GROUP
        e_sel = jnp.where(grp == float(gi), logits[lo:lo + EXPERTS_PER_GROUP, :], e_sel)
    m1 = jnp.max(e_sel, axis=0, keepdims=True)
    i1 = first(e_sel == m1)
    e_rest = jnp.where(row8f == i1, neg, e_sel)
    m2 = jnp.max(e_rest, axis=0, keepdims=True)
    i2 = first(e_rest == m2)
    e21 = jnp.exp(m2 - m1)
    p1 = 1.0 / (1.0 + e21)
    w0 = p_grp * p1
    w1 = p_grp * (e21 * p1)
    eid0 = (grp * EXPERTS_PER_GROUP + i1).astype(jnp.int32)
    eid1 = (grp * EXPERTS_PER_GROUP + i2).astype(jnp.int32)

    erow = lax.broadcasted_iota(jnp.int32, (N_EXPERTS, T), 0)
    oh0 = jnp.where(erow == eid0, 1.0, 0.0)
    oh1 = jnp.where(erow == eid1, 1.0, 0.0)
    both = oh0 + oh1
    ti = lax.broadcasted_iota(jnp.int32, (T, T), 0)
    tj = lax.broadcasted_iota(jnp.int32, (T, T), 1)
    before = jnp.where(ti < tj, 1.0, 0.0).astype(BF16)
    prior = _dot(both.astype(BF16), before)
    rank0 = jnp.sum(prior * oh0, axis=0, keepdims=True).astype(jnp.int32)
    rank1 = jnp.sum(prior * oh1, axis=0, keepdims=True).astype(jnp.int32)
    total = (prior[:, T - 1:T] + both[:, T - 1:T]).astype(jnp.int32)

    info = jnp.where(row8 == 0, eid0, jnp.where(row8 == 1, eid1, jnp.where(row8 == 2, rank0,
                     jnp.where(row8 == 3, rank1, 0))))
    info_ref[...] = info
    wts_ref[...] = jnp.where(row8 == 0, w0, jnp.where(row8 == 1, w1, 0.0))
    cnt_ref[0] = jnp.broadcast_to(total, (N_EXPERTS, LANES))


def _router(x1f, wr, br):
    N = x1f.shape[0]
    T = min(ROUTE_TILE, N)
    assert N % T == 0
    nt = N // T
    return pl.pallas_call(
        functools.partial(_router_kernel, tile=T),
        grid=(nt,),
        in_specs=[pl.BlockSpec((T, D_MODEL), lambda i: (i, 0)), _const_spec(wr.shape), _const_spec(br.shape)],
        out_specs=[pl.BlockSpec((SUBLANES, T), lambda i: (0, i)),
                   pl.BlockSpec((SUBLANES, T), lambda i: (0, i)),
                   pl.BlockSpec((1, N_EXPERTS, LANES), lambda i: (i, 0, 0))],
        out_shape=[jax.ShapeDtypeStruct((SUBLANES, N), jnp.int32),
                   jax.ShapeDtypeStruct((SUBLANES, N), F32),
                   jax.ShapeDtypeStruct((nt, N_EXPERTS, LANES), jnp.int32)],
        compiler_params=pltpu.CompilerParams(dimension_semantics=("arbitrary",)),
        name="router",
    )(x1f, wr, br)


def _dispatch_kernel(dest_ref, xpk_ref, xs_in_ref, xs_ref, sem, *, tile):
    del xs_in_ref
    T = tile
    base = pl.program_id(0) * T

    def row_copy(t, kk):
        return pltpu.make_async_copy(xpk_ref.at[pl.ds(base + t, 1)], xs_ref.at[pl.ds(dest_ref[kk, t], 1)], sem)

    def start(t):
        row_copy(t, 0).start()
        row_copy(t, 1).start()

    def wait(t):
        row_copy(t, 0).wait()
        row_copy(t, 1).wait()

    lag = min(DMA_LAG, T)

    def head(t, c):
        start(t)
        return c

    def steady(t, c):
        start(t)
        wait(t - lag)
        return c

    def tail(t, c):
        wait(t)
        return c

    lax.fori_loop(0, lag, head, 0)
    lax.fori_loop(lag, T, steady, 0)
    lax.fori_loop(T - lag, T, tail, 0)


def _dispatch(dest, xpk, xs_zero):
    N = xpk.shape[0]
    T = min(DISPATCH_TILE, N)
    assert N % T == 0
    return pl.pallas_call(
        functools.partial(_dispatch_kernel, tile=T),
        grid=(N // T,),
        in_specs=[pl.BlockSpec((2, T), lambda i: (0, i), memory_space=pltpu.SMEM),
                  pl.BlockSpec(memory_space=pl.ANY),
                  pl.BlockSpec(memory_space=pl.ANY)],
        out_specs=pl.BlockSpec(memory_space=pl.ANY),
        out_shape=jax.ShapeDtypeStruct(xs_zero.shape, xs_zero.dtype),
        scratch_shapes=[pltpu.SemaphoreType.DMA(())],
        input_output_aliases={2: 0},
        compiler_params=pltpu.CompilerParams(dimension_semantics=("arbitrary",), has_side_effects=True),
        name="dispatch",
    )(dest, xpk, xs_zero)


def _experts_kernel(blk_e_ref, nblk_ref, xs_ref, w1_ref, w3_ref, w2_ref, ys_ref):
    del blk_e_ref
    i = pl.program_id(0)

    @pl.when(i < nblk_ref[0])
    def _():
        half = D_MODEL // 2
        u = xs_ref[...]
        lo = pltpu.bitcast(u << 16, F32).astype(BF16)
        hi = pltpu.bitcast(u & jnp.uint32(0xFFFF0000), F32).astype(BF16)
        h1 = _dot(lo, w1_ref[0, :half, :]) + _dot(hi, w1_ref[0, half:, :])
        h3 = _dot(lo, w3_ref[0, :half, :]) + _dot(hi, w3_ref[0, half:, :])
        hact = (h1 * _sigmoid(h1)) * h3
        ys_ref[...] = _dot(hact.astype(BF16), w2_ref[0])

    @pl.when(i >= nblk_ref[0])
    def _():
        ys_ref[...] = jnp.zeros_like(ys_ref)


def _experts(blk_e, nblk, xs, w1, w3, w2):
    P = xs.shape[0]
    nb = P // MOE_ROWS
    grid_spec = pltpu.PrefetchScalarGridSpec(
        num_scalar_prefetch=2,
        grid=(nb,),
        in_specs=[pl.BlockSpec((MOE_ROWS, D_MODEL // 2), lambda i, be, n: (i, 0)),
                  pl.BlockSpec((1, D_MODEL, EXPERT_FF), lambda i, be, n: (be[i], 0, 0)),
                  pl.BlockSpec((1, D_MODEL, EXPERT_FF), lambda i, be, n: (be[i], 0, 0)),
                  pl.BlockSpec((1, EXPERT_FF, D_MODEL), lambda i, be, n: (be[i], 0, 0))],
        out_specs=pl.BlockSpec((MOE_ROWS, D_MODEL), lambda i, be, n: (i, 0)),
    )
    return pl.pallas_call(
        _experts_kernel,
        grid_spec=grid_spec,
        out_shape=jax.ShapeDtypeStruct((P, D_MODEL), F32),
        compiler_params=pltpu.CompilerParams(dimension_semantics=("arbitrary",), vmem_limit_bytes=VMEM_LIMIT),
        name="experts",
    )(blk_e, nblk, xs, w1, w3, w2)


def _combine_kernel(dest_ref, x1_ref, wts_ref, ys_ref, g_ref, b_ref, out_ref, gbuf, sem, *, tile):
    T = tile

    def row_copy(t, kk):
        return pltpu.make_async_copy(ys_ref.at[pl.ds(dest_ref[kk, t], 1)], gbuf.at[kk, pl.ds(t, 1)], sem)

    def start(t, c):
        row_copy(t, 0).start()
        row_copy(t, 1).start()
        return c

    def wait(t, c):
        row_copy(t, 0).wait()
        row_copy(t, 1).wait()
        return c

    lax.fori_loop(0, T, start, 0)
    lax.fori_loop(0, T, wait, 0)

    wpad = jnp.concatenate([wts_ref[...], jnp.zeros((LANES - SUBLANES, T), F32)], axis=0)
    wt = wpad.T
    y = wt[:, 0:1] * gbuf[0] + wt[:, 1:2] * gbuf[1]
    out_ref[...] = _layer_norm(DN_ALPHA * x1_ref[...] + y, g_ref[...], b_ref[...])


def _combine(dest, x1f, wts, ys, g, b):
    N = x1f.shape[0]
    T = min(COMBINE_TILE, N)
    assert N % T == 0
    return pl.pallas_call(
        functools.partial(_combine_kernel, tile=T),
        grid=(N // T,),
        in_specs=[pl.BlockSpec((2, T), lambda i: (0, i), memory_space=pltpu.SMEM),
                  pl.BlockSpec((T, D_MODEL), lambda i: (i, 0)),
                  pl.BlockSpec((SUBLANES, T), lambda i: (0, i)),
                  pl.BlockSpec(memory_space=pl.ANY),
                  _const_spec(g.shape), _const_spec(b.shape)],
        out_specs=pl.BlockSpec((T, D_MODEL), lambda i: (i, 0)),
        out_shape=jax.ShapeDtypeStruct((N, D_MODEL), F32),
        scratch_shapes=[pltpu.VMEM((2, T, D_MODEL), F32), pltpu.SemaphoreType.DMA(())],
        compiler_params=pltpu.CompilerParams(dimension_semantics=("arbitrary",)),
        name="combine",
    )(dest, x1f, wts, ys, g, b)


def _pack_mixer_weights(w_in, b_in, rg_w_a, rg_w_x, rg_b_a, rg_b_x, gla_w_a2):
    rx, ry, q, k, v, g, alr, ga, gb = jnp.split(w_in, _SPLIT_POINTS, axis=1)
    pad = jnp.zeros((D_MODEL, LANES - GLA_RANK), w_in.dtype)
    win = jnp.concatenate([rx, ry, q, k, v, g, ga, gb, alr, pad], axis=1).astype(BF16)
    brx, bry, bq, bk, bvv, bg, balr, bga, bgb = jnp.split(b_in, _SPLIT_POINTS)
    bpk = jnp.concatenate([brx, bry, bq, bk, bvv, bg, bga, bgb, balr, jnp.zeros((LANES - GLA_RANK,), b_in.dtype)])
    zero = jnp.zeros((RNN_BLOCK_W, RNN_BLOCK_W), w_in.dtype)
    tiles = []
    for p in range(RNN_BLOCKS // 2):
        top = jnp.concatenate([rg_w_a[2 * p], zero, rg_w_x[2 * p], zero], axis=1)
        bot = jnp.concatenate([zero, rg_w_a[2 * p + 1], zero, rg_w_x[2 * p + 1]], axis=1)
        tiles.append(jnp.concatenate([top, bot], axis=0))
    wgate = jnp.stack(tiles).astype(BF16)
    bgate = jnp.concatenate([rg_b_a, rg_b_x])[None, :]
    wa2 = jnp.concatenate([gla_w_a2, jnp.zeros((LANES - GLA_RANK, GLA_DK), gla_w_a2.dtype)], axis=0).astype(BF16)
    return win, bpk[None, :], wgate, bgate, wa2


_IN_SPLITS = (RNN_WIDTH, RNN_WIDTH, GLA_DK, GLA_DK, GLA_DV, GLA_DV, GLA_RANK, D_MODEL, D_MODEL)
_SPLIT_POINTS = tuple(sum(_IN_SPLITS[:i + 1]) for i in range(len(_IN_SPLITS) - 1))


def _layer(x, w_in, b_in, conv_w, conv_b, rg_w_a, rg_b_a, rg_w_x, rg_b_x, rg_lambda, gla_w_a2, gla_b_a,
           gla_norm_g, w_proj_rnn, w_proj_gla, w_o, b_o, ln1_g, ln1_b, router_w_group, router_b_group,
           router_w_expert, router_b_expert, exp_w1, exp_w3, exp_w2, ln2_g, ln2_b):
    B, S, _ = x.shape
    N = B * S
    row = lambda p: p[None, :]

    win, bpk, wgate, bgate, wa2 = _pack_mixer_weights(w_in, b_in, rg_w_a, rg_w_x, rg_b_a, rg_b_x, gla_w_a2)
    x1, xpk = _mixer(x, win, bpk, conv_w, row(conv_b), wgate, bgate, row(rg_lambda), wa2, row(gla_b_a),
                     row(gla_norm_g), w_proj_rnn.astype(BF16), w_proj_gla.astype(BF16), w_o.astype(BF16),
                     row(b_o), row(ln1_g), row(ln1_b))
    x1f = x1.reshape(N, D_MODEL)
    xpk = xpk.reshape(N, D_MODEL // 2)

    wr = jnp.concatenate([router_w_group.T, jnp.zeros((SUBLANES - N_GROUPS, D_MODEL), F32), router_w_expert.T],
                         axis=0).astype(BF16)
    br = jnp.concatenate([router_b_group, jnp.zeros((SUBLANES - N_GROUPS,), F32), router_b_expert])[:, None]
    info, wts, tcnt = _router(x1f, wr, br)

    tcnt = tcnt[:, :, 0]
    nt = tcnt.shape[0]
    tot = jnp.sum(tcnt, axis=0)
    pcount = (tot + MOE_ROWS - 1) // MOE_ROWS * MOE_ROWS
    pend = jnp.cumsum(pcount)
    base = (pend - pcount)[None, :] + jnp.cumsum(tcnt, axis=0) - tcnt
    eid = info[0:2].reshape(2, nt, N // nt)
    onehot = eid[..., None] == jnp.arange(N_EXPERTS, dtype=jnp.int32)
    dest = jnp.sum(jnp.where(onehot, base[None, :, None, :], 0), axis=-1).reshape(2, N) + info[2:4]
    nb = -(-(2 * N) // MOE_ROWS) + N_EXPERTS
    P = nb * MOE_ROWS
    nblk = (pend[-1] // MOE_ROWS).astype(jnp.int32)
    blk_start = jnp.arange(nb, dtype=jnp.int32) * MOE_ROWS
    blk_e = jnp.minimum(jnp.searchsorted(pend, blk_start, side="right"), N_EXPERTS - 1).astype(jnp.int32)
    blk_e = jnp.where(blk_start < pend[-1], blk_e, blk_e[jnp.maximum(nblk - 1, 0)])

    xs = _dispatch(dest, xpk, jnp.zeros((P, D_MODEL // 2), jnp.uint32))
    ys = _experts(blk_e, nblk[None], xs, exp_w1.astype(BF16), exp_w3.astype(BF16), exp_w2.astype(BF16))
    out = _combine(dest, x1f, wts, ys, row(ln2_g), row(ln2_b))
    return out.reshape(B, S, D_MODEL)


def kernel(x, w_in, b_in, conv_w, conv_b, rg_w_a, rg_b_a, rg_w_x, rg_b_x, rg_lambda, gla_w_a2, gla_b_a, gla_norm_g, w_proj_rnn, w_proj_gla, w_o, b_o, ln1_g, ln1_b, router_w_group, router_b_group, router_w_expert, router_b_expert, exp_w1, exp_w3, exp_w2, ln2_g, ln2_b):
    h = x
    for l in range(w_in.shape[0]):
        h = _layer(h, w_in[l], b_in[l], conv_w[l], conv_b[l], rg_w_a[l], rg_b_a[l], rg_w_x[l], rg_b_x[l],
                   rg_lambda[l], gla_w_a2[l], gla_b_a[l], gla_norm_g[l], w_proj_rnn[l], w_proj_gla[l], w_o[l],
                   b_o[l], ln1_g[l], ln1_b[l], router_w_group[l], router_b_group[l], router_w_expert[l],
                   router_b_expert[l], exp_w1[l], exp_w3[l], exp_w2[l], ln2_g[l], ln2_b[l])
    return h
```

```python
import functools

import jax
import jax.numpy as jnp
from jax import lax
from jax.experimental import pallas as pl
from jax.experimental.pallas import tpu as pltpu

F32 = jnp.float32
BF16 = jnp.bfloat16

D_MODEL = 1024
RNN_WIDTH = 1024
RNN_BLOCKS = 8
RNN_BLOCK_W = RNN_WIDTH // RNN_BLOCKS
CONV_WIDTH = 4
LRU_C = 8.0
GLA_HEADS = 4
GLA_DK = D_MODEL // 2
GLA_DV = D_MODEL
GLA_HEAD_K = GLA_DK // GLA_HEADS
GLA_HEAD_V = GLA_DV // GLA_HEADS
GLA_RANK = 16
GLA_TAU = 16.0
GLA_CHUNK = 64
N_GROUPS = 4
EXPERTS_PER_GROUP = 8
N_EXPERTS = N_GROUPS * EXPERTS_PER_GROUP
EXPERT_FF = 512
DN_ALPHA = 2.0 ** 0.25
LN_EPS = 1e-5
RMS_EPS = 1e-6

LANES = 128
SUBLANES = 8
VMEM_LIMIT = 56 * 1024 * 1024

C_RX, C_RY, C_Q, C_K, C_V, C_G, C_GA, C_GB, C_ALR = 0, 1024, 2048, 2560, 3072, 4096, 5120, 6144, 7168
IN_COLS_PACKED = C_ALR + LANES

MIX_TILE = 256
ROUTE_TILE = 256
MOE_ROWS = 256
DISPATCH_TILE = 512
COMBINE_TILE = 256
ROUTE_ROWS = 8 + N_EXPERTS


def _sigmoid(v):
    return 1.0 / (1.0 + jnp.exp(-v))


def _softplus(v):
    return jnp.maximum(v, 0.0) + jnp.log1p(jnp.exp(-jnp.abs(v)))


def _layer_norm(v, g, b):
    mu = jnp.mean(v, axis=-1, keepdims=True)
    c = v - mu
    var = jnp.mean(c * c, axis=-1, keepdims=True)
    return c * lax.rsqrt(var + LN_EPS) * g + b


def _dot(a, b):
    return jnp.dot(a, b, preferred_element_type=F32)


def _dot_nt(a, b):
    return lax.dot_general(a, b, (((1,), (1,)), ((), ())), preferred_element_type=F32)


def _dot_tn(a, b):
    return lax.dot_general(a, b, (((0,), (0,)), ((), ())), preferred_element_type=F32)


def _const_spec(shape):
    nd = len(shape)
    return pl.BlockSpec(shape, lambda *_: (0,) * nd, pipeline_mode=pl.Buffered(1))


def _mixer_kernel(x_ref, win_ref, bin_ref, convw_ref, convb_ref, wgate_ref, bgate_ref, lam_ref,
                  wa2_ref, ba2_ref, gnorm_ref, wprnn_ref, wpgla_ref, wo_ref, bo_ref, ln1g_ref, ln1b_ref,
                  x1_ref, xpk_ref, rxbuf, hcar, st_ref, obuf, *, tile):
    T = tile

    @pl.when(pl.program_id(1) == 0)
    def _():
        rxbuf[0:SUBLANES, :] = jnp.zeros((SUBLANES, RNN_WIDTH), F32)
        hcar[...] = jnp.zeros_like(hcar)
        st_ref[...] = jnp.zeros_like(st_ref)

    x = x_ref[0]
    xb = x.astype(BF16)

    def proj(c0, c1):
        return _dot(xb, win_ref[:, c0:c1]) + bin_ref[:, c0:c1]

    rx = proj(C_RX, C_RX + RNN_WIDTH)
    rxbuf[SUBLANES:SUBLANES + T, :] = rx
    u = convb_ref[...] + convw_ref[CONV_WIDTH - 1:CONV_WIDTH, :] * rx
    for j in range(1, CONV_WIDTH):
        u = u + convw_ref[CONV_WIDTH - 1 - j:CONV_WIDTH - j, :] * rxbuf[SUBLANES - j:SUBLANES - j + T, :]
    rxbuf[0:SUBLANES, :] = rxbuf[T:T + SUBLANES, :]

    r_parts, i_parts = [], []
    for p in range(RNN_BLOCKS // 2):
        up = u[:, 256 * p:256 * (p + 1)].astype(BF16)
        gp = _dot(up, wgate_ref[p])
        r_parts.append(gp[:, :256])
        i_parts.append(gp[:, 256:])
    r = _sigmoid(jnp.concatenate(r_parts, axis=1) + bgate_ref[:, :RNN_WIDTH])
    ig = _sigmoid(jnp.concatenate(i_parts, axis=1) + bgate_ref[:, RNN_WIDTH:])
    log_a = (-LRU_C) * r * _softplus(-lam_ref[...])
    a = jnp.exp(log_a)
    y2 = 2.0 * log_a
    series = y2 * (1.0 + y2 * (0.5 + y2 * (1.0 / 6.0 + y2 * (1.0 / 24.0 + y2 * (1.0 / 120.0)))))
    em1 = jnp.where(y2 > -0.03125, series, a * a - 1.0)
    bv = jnp.sqrt(-em1) * (ig * u)

    rows = lax.broadcasted_iota(jnp.int32, (T, RNN_WIDTH), 0)
    sa, sb = a, bv
    s = 1
    while s < T:
        keep = rows >= s
        sb = sa * jnp.where(keep, pltpu.roll(sb, s, 0), 0.0) + sb
        sa = sa * jnp.where(keep, pltpu.roll(sa, s, 0), 1.0)
        s *= 2
    h = sb + sa * hcar[0:1, :]
    hcar[0:1, :] = h[T - 1:T, :]

    ry = proj(C_RY, C_RY + RNN_WIDTH)
    out_a = _dot((h * jax.nn.gelu(ry)).astype(BF16), wprnn_ref[...])

    qk = proj(C_Q, C_Q + 2 * GLA_DK)
    q = qk[:, :GLA_DK] * (GLA_HEAD_K ** -0.5)
    k = qk[:, GLA_DK:]
    v = proj(C_V, C_V + GLA_DV)
    alr = proj(C_ALR, C_ALR + LANES)
    z = _dot(alr.astype(BF16), wa2_ref[...]) + ba2_ref[...]
    la = -_softplus(-z) * (1.0 / GLA_TAU)

    ri = lax.broadcasted_iota(jnp.int32, (T, T), 0)
    ci = lax.broadcasted_iota(jnp.int32, (T, T), 1)
    chunk_start = (ri >> 6) << 6
    tri = jnp.where((ci <= ri) & (ci >= chunk_start), 1.0, 0.0).astype(BF16)
    la_hi = la.astype(BF16)
    la_lo = (la - la_hi.astype(F32)).astype(BF16)
    bcum = _dot(tri, la_hi) + _dot(tri, la_lo)

    cr = lax.broadcasted_iota(jnp.int32, (GLA_CHUNK, GLA_CHUNK), 0)
    cc = lax.broadcasted_iota(jnp.int32, (GLA_CHUNK, GLA_CHUNK), 1)
    causal = cr >= cc
    for c in range(T // GLA_CHUNK):
        r0 = c * GLA_CHUNK
        bc = bcum[r0:r0 + GLA_CHUNK, :]
        bl = bcum[r0 + GLA_CHUNK - 1:r0 + GLA_CHUNK, :]
        qd_all = q[r0:r0 + GLA_CHUNK, :] * jnp.exp(bc)
        kc = k[r0:r0 + GLA_CHUNK, :]
        ki_all = kc * jnp.exp(-bc)
        ke_all = kc * jnp.exp(bl - bc)
        dec_all = jnp.exp(bl)
        for hd in range(GLA_HEADS):
            ks = slice(hd * GLA_HEAD_K, (hd + 1) * GLA_HEAD_K)
            vs = slice(hd * GLA_HEAD_V, (hd + 1) * GLA_HEAD_V)
            qd = qd_all[:, ks].astype(BF16)
            ki = ki_all[:, ks].astype(BF16)
            ke = ke_all[:, ks].astype(BF16)
            vv = v[r0:r0 + GLA_CHUNK, vs].astype(BF16)
            sc = jnp.where(causal, _dot_nt(qd, ki), 0.0)
            st = st_ref[hd]
            o = _dot(sc.astype(BF16), vv) + _dot_nt(qd, st.astype(BF16))
            obuf[r0:r0 + GLA_CHUNK, vs] = o
            st_ref[hd] = st * dec_all[:, ks] + _dot_tn(vv, ke)

    g = proj(C_G, C_G + GLA_DV)
    o_parts = []
    for hd in range(GLA_HEADS):
        vs = slice(hd * GLA_HEAD_V, (hd + 1) * GLA_HEAD_V)
        oh = obuf[:, vs]
        ms = jnp.mean(oh * oh, axis=-1, keepdims=True)
        o_parts.append(oh * lax.rsqrt(ms + RMS_EPS) * gnorm_ref[:, vs])
    on = jnp.concatenate(o_parts, axis=1) * (g * _sigmoid(g))
    out_b = _dot(on.astype(BF16), wpgla_ref[...])

    ga = proj(C_GA, C_GA + D_MODEL)
    gb = proj(C_GB, C_GB + D_MODEL)
    merged = _sigmoid(ga) * out_a + _sigmoid(gb) * out_b
    y = _dot(merged.astype(BF16), wo_ref[...]) + bo_ref[...]
    x1 = _layer_norm(DN_ALPHA * x + y, ln1g_ref[...], ln1b_ref[...])
    x1_ref[0] = x1

    bits = pltpu.bitcast(x1.astype(BF16).astype(F32), jnp.uint32)
    half = D_MODEL // 2
    xpk_ref[0] = (bits[:, :half] >> 16) | bits[:, half:]


def _mixer(x, win, b_in, conv_w, conv_b, wgate, bgate, lam, wa2, ba2, gnorm, wprnn, wpgla, wo, bo, ln1g, ln1b):
    B, S, _ = x.shape
    T = min(MIX_TILE, S)
    assert S % T == 0 and T % GLA_CHUNK == 0
    weights = (win, b_in, conv_w, conv_b, wgate, bgate, lam, wa2, ba2, gnorm, wprnn, wpgla, wo, bo, ln1g, ln1b)
    return pl.pallas_call(
        functools.partial(_mixer_kernel, tile=T),
        grid=(B, S // T),
        in_specs=[pl.BlockSpec((1, T, D_MODEL), lambda b, s: (b, s, 0))] + [_const_spec(w.shape) for w in weights],
        out_specs=[pl.BlockSpec((1, T, D_MODEL), lambda b, s: (b, s, 0)),
                   pl.BlockSpec((1, T, D_MODEL // 2), lambda b, s: (b, s, 0))],
        out_shape=[jax.ShapeDtypeStruct((B, S, D_MODEL), F32),
                   jax.ShapeDtypeStruct((B, S, D_MODEL // 2), jnp.uint32)],
        scratch_shapes=[pltpu.VMEM((T + SUBLANES, RNN_WIDTH), F32),
                        pltpu.VMEM((SUBLANES, RNN_WIDTH), F32),
                        pltpu.VMEM((GLA_HEADS, GLA_HEAD_V, GLA_HEAD_K), F32),
                        pltpu.VMEM((T, GLA_DV), F32)],
        compiler_params=pltpu.CompilerParams(dimension_semantics=("arbitrary", "arbitrary"),
                                             vmem_limit_bytes=VMEM_LIMIT),
        name="mixer",
    )(x, *weights)


def _router_kernel(x1_ref, wr_ref, br_ref, info_ref, wts_ref, cnt_ref, *, tile):
    T = tile
    logits = _dot_nt(wr_ref[...], x1_ref[...].astype(BF16)) + br_ref[...]
    row8 = lax.broadcasted_iota(jnp.int32, (SUBLANES, T), 0)
    row8f = row8.astype(F32)
    neg = jnp.float32(-jnp.inf)
    first = lambda hit: jnp.min(jnp.where(hit, row8f, float(SUBLANES)), axis=0, keepdims=True)

    gl = jnp.where(row8 < N_GROUPS, logits[0:SUBLANES, :], neg)
    gmax = jnp.max(gl, axis=0, keepdims=True)
    grp = first(gl == gmax)
    p_grp = 1.0 / jnp.sum(jnp.exp(gl - gmax), axis=0, keepdims=True)

    e_sel = jnp.zeros((EXPERTS_PER_GROUP, T), F32)
    for gi in range(N_GROUPS):
        lo = SUBLANES + gi * EXPERTS_PER_GROUP
        e_sel = jnp.where(grp == float(gi), logits[lo:lo + EXPERTS_PER_GROUP, :], e_sel)
    m1 = jnp.max(e_sel, axis=0, keepdims=True)
    i1 = first(e_sel == m1)
    e_rest = jnp.where(row8f == i1, neg, e_sel)
    m2 = jnp.max(e_rest, axis=0, keepdims=True)
    i2 = first(e_rest == m2)
    e21 = jnp.exp(m2 - m1)
    p1 = 1.0 / (1.0 + e21)
    w0 = p_grp * p1
    w1 = p_grp * (e21 * p1)
    eid0 = (grp * EXPERTS_PER_GROUP + i1).astype(jnp.int32)
    eid1 = (grp * EXPERTS_PER_GROUP + i2).astype(jnp.int32)

    erow = lax.broadcasted_iota(jnp.int32, (N_EXPERTS, T), 0)
    oh0 = jnp.where(erow == eid0, 1.0, 0.0)
    oh1 = jnp.where(erow == eid1, 1.0, 0.0)
    both = oh0 + oh1
    ti = lax.broadcasted_iota(jnp.int32, (T, T), 0)
    tj = lax.broadcasted_iota(jnp.int32, (T, T), 1)
    before = jnp.where(ti < tj, 1.0, 0.0).astype(BF16)
    prior = _dot(both.astype(BF16), before)
    rank0 = jnp.sum(prior * oh0, axis=0, keepdims=True).astype(jnp.int32)
    rank1 = jnp.sum(prior * oh1, axis=0, keepdims=True).astype(jnp.int32)
    total = (prior[:, T - 1:T] + both[:, T - 1:T]).astype(jnp.int32)

    info = jnp.where(row8 == 0, eid0, jnp.where(row8 == 1, eid1, jnp.where(row8 == 2, rank0,
                     jnp.where(row8 == 3, rank1, 0))))
    info_ref[...] = info
    wts_ref[...] = jnp.where(row8 == 0, w0, jnp.where(row8 == 1, w1, 0.0))
    cnt_ref[0] = jnp.broadcast_to(total, (N_EXPERTS, LANES))


def _router(x1f, wr, br):
    N = x1f.shape[0]
    T = min(ROUTE_TILE, N)
    assert N % T == 0
    nt = N // T
    return pl.pallas_call(
        functools.partial(_router_kernel, tile=T),
        grid=(nt,),
        in_specs=[pl.BlockSpec((T, D_MODEL), lambda i: (i, 0)), _const_spec(wr.shape), _const_spec(br.shape)],
        out_specs=[pl.BlockSpec((SUBLANES, T), lambda i: (0, i)),
                   pl.BlockSpec((SUBLANES, T), lambda i: (0, i)),
                   pl.BlockSpec((1, N_EXPERTS, LANES), lambda i: (i, 0, 0))],
        out_shape=[jax.ShapeDtypeStruct((SUBLANES, N), jnp.int32),
                   jax.ShapeDtypeStruct((SUBLANES, N), F32),
                   jax.ShapeDtypeStruct((nt, N_EXPERTS, LANES), jnp.int32)],
        compiler_params=pltpu.CompilerParams(dimension_semantics=("arbitrary",)),
        name="router",
    )(x1f, wr, br)


def _dispatch_kernel(dest_ref, xpk_ref, xs_in_ref, xs_ref, sem, *, tile):
    del xs_in_ref
    T = tile

    def start(t, c):
        for kk in range(2):
            pltpu.make_async_copy(xpk_ref.at[pl.ds(t, 1)], xs_ref.at[pl.ds(dest_ref[kk, t], 1)], sem).start()
        return c

    lax.fori_loop(0, T, start, 0, unroll=8)
    for kk in range(2):
        pltpu.make_async_copy(xpk_ref, xs_ref.at[pl.ds(0, T)], sem).wait()


def _dispatch(dest, xpk, xs_zero):
    N = xpk.shape[0]
    T = min(DISPATCH_TILE, N)
    assert N % T == 0
    return pl.pallas_call(
        functools.partial(_dispatch_kernel, tile=T),
        grid=(N // T,),
        in_specs=[pl.BlockSpec((2, T), lambda i: (0, i), memory_space=pltpu.SMEM),
                  pl.BlockSpec((T, D_MODEL // 2), lambda i: (i, 0)),
                  pl.BlockSpec(memory_space=pl.ANY)],
        out_specs=pl.BlockSpec(memory_space=pl.ANY),
        out_shape=jax.ShapeDtypeStruct(xs_zero.shape, xs_zero.dtype),
        scratch_shapes=[pltpu.SemaphoreType.DMA(())],
        input_output_aliases={2: 0},
        compiler_params=pltpu.CompilerParams(dimension_semantics=("arbitrary",), has_side_effects=True),
        name="dispatch",
    )(dest, xpk, xs_zero)


def _experts_kernel(blk_e_ref, nblk_ref, xs_ref, w1_ref, w3_ref, w2_ref, ys_ref):
    del blk_e_ref
    i = pl.program_id(0)

    @pl.when(i < nblk_ref[0])
    def _():
        half = D_MODEL // 2
        u = xs_ref[...]
        lo = pltpu.bitcast(u << 16, F32).astype(BF16)
        hi = pltpu.bitcast(u & jnp.uint32(0xFFFF0000), F32).astype(BF16)
        h1 = _dot(lo, w1_ref[0, :half, :]) + _dot(hi, w1_ref[0, half:, :])
        h3 = _dot(lo, w3_ref[0, :half, :]) + _dot(hi, w3_ref[0, half:, :])
        hact = (h1 * _sigmoid(h1)) * h3
        ys_ref[...] = _dot(hact.astype(BF16), w2_ref[0])

    @pl.when(i >= nblk_ref[0])
    def _():
        ys_ref[...] = jnp.zeros_like(ys_ref)


def _experts(blk_e, nblk, xs, w1, w3, w2):
    P = xs.shape[0]
    nb = P // MOE_ROWS
    grid_spec = pltpu.PrefetchScalarGridSpec(
        num_scalar_prefetch=2,
        grid=(nb,),
        in_specs=[pl.BlockSpec((MOE_ROWS, D_MODEL // 2), lambda i, be, n: (i, 0)),
                  pl.BlockSpec((1, D_MODEL, EXPERT_FF), lambda i, be, n: (be[i], 0, 0)),
                  pl.BlockSpec((1, D_MODEL, EXPERT_FF), lambda i, be, n: (be[i], 0, 0)),
                  pl.BlockSpec((1, EXPERT_FF, D_MODEL), lambda i, be, n: (be[i], 0, 0))],
        out_specs=pl.BlockSpec((MOE_ROWS, D_MODEL), lambda i, be, n: (i, 0)),
    )
    return pl.pallas_call(
        _experts_kernel,
        grid_spec=grid_spec,
        out_shape=jax.ShapeDtypeStruct((P, D_MODEL), F32),
        compiler_params=pltpu.CompilerParams(dimension_semantics=("arbitrary",), vmem_limit_bytes=VMEM_LIMIT),
        name="experts",
    )(blk_e, nblk, xs, w1, w3, w2)


def _combine_kernel(dest_ref, x1_ref, wts_ref, ys_ref, g_ref, b_ref, out_ref, gbuf, sem, *, tile):
    T = tile

    def start(t, c):
        for kk in range(2):
            pltpu.make_async_copy(ys_ref.at[pl.ds(dest_ref[kk, t], 1)], gbuf.at[kk, pl.ds(t, 1)], sem).start()
        return c

    lax.fori_loop(0, T, start, 0, unroll=8)
    for kk in range(2):
        pltpu.make_async_copy(ys_ref.at[pl.ds(0, T)], gbuf.at[kk], sem).wait()

    wpad = jnp.concatenate([wts_ref[...], jnp.zeros((LANES - SUBLANES, T), F32)], axis=0)
    wt = wpad.T
    y = wt[:, 0:1] * gbuf[0] + wt[:, 1:2] * gbuf[1]
    out_ref[...] = _layer_norm(DN_ALPHA * x1_ref[...] + y, g_ref[...], b_ref[...])


def _combine(dest, x1f, wts, ys, g, b):
    N = x1f.shape[0]
    T = min(COMBINE_TILE, N)
    assert N % T == 0
    return pl.pallas_call(
        functools.partial(_combine_kernel, tile=T),
        grid=(N // T,),
        in_specs=[pl.BlockSpec((2, T), lambda i: (0, i), memory_space=pltpu.SMEM),
                  pl.BlockSpec((T, D_MODEL), lambda i: (i, 0)),
                  pl.BlockSpec((SUBLANES, T), lambda i: (0, i)),
                  pl.BlockSpec(memory_space=pl.ANY),
                  _const_spec(g.shape), _const_spec(b.shape)],
        out_specs=pl.BlockSpec((T, D_MODEL), lambda i: (i, 0)),
        out_shape=jax.ShapeDtypeStruct((N, D_MODEL), F32),
        scratch_shapes=[pltpu.VMEM((2, T, D_MODEL), F32), pltpu.SemaphoreType.DMA(())],
        compiler_params=pltpu.CompilerParams(dimension_semantics=("arbitrary",)),
        name="combine",
    )(dest, x1f, wts, ys, g, b)


def _pack_mixer_weights(w_in, b_in, rg_w_a, rg_w_x, rg_b_a, rg_b_x, gla_w_a2):
    rx, ry, q, k, v, g, alr, ga, gb = jnp.split(w_in, _SPLIT_POINTS, axis=1)
    pad = jnp.zeros((D_MODEL, LANES - GLA_RANK), w_in.dtype)
    win = jnp.concatenate([rx, ry, q, k, v, g, ga, gb, alr, pad], axis=1).astype(BF16)
    brx, bry, bq, bk, bvv, bg, balr, bga, bgb = jnp.split(b_in, _SPLIT_POINTS)
    bpk = jnp.concatenate([brx, bry, bq, bk, bvv, bg, bga, bgb, balr, jnp.zeros((LANES - GLA_RANK,), b_in.dtype)])
    zero = jnp.zeros((RNN_BLOCK_W, RNN_BLOCK_W), w_in.dtype)
    tiles = []
    for p in range(RNN_BLOCKS // 2):
        top = jnp.concatenate([rg_w_a[2 * p], zero, rg_w_x[2 * p], zero], axis=1)
        bot = jnp.concatenate([zero, rg_w_a[2 * p + 1], zero, rg_w_x[2 * p + 1]], axis=1)
        tiles.append(jnp.concatenate([top, bot], axis=0))
    wgate = jnp.stack(tiles).astype(BF16)
    bgate = jnp.concatenate([rg_b_a, rg_b_x])[None, :]
    wa2 = jnp.concatenate([gla_w_a2, jnp.zeros((LANES - GLA_RANK, GLA_DK), gla_w_a2.dtype)], axis=0).astype(BF16)
    return win, bpk[None, :], wgate, bgate, wa2


_IN_SPLITS = (RNN_WIDTH, RNN_WIDTH, GLA_DK, GLA_DK, GLA_DV, GLA_DV, GLA_RANK, D_MODEL, D_MODEL)
_SPLIT_POINTS = tuple(sum(_IN_SPLITS[:i + 1]) for i in range(len(_IN_SPLITS) - 1))


def _layer(x, w_in, b_in, conv_w, conv_b, rg_w_a, rg_b_a, rg_w_x, rg_b_x, rg_lambda, gla_w_a2, gla_b_a,
           gla_norm_g, w_proj_rnn, w_proj_gla, w_o, b_o, ln1_g, ln1_b, router_w_group, router_b_group,
           router_w_expert, router_b_expert, exp_w1, exp_w3, exp_w2, ln2_g, ln2_b):
    B, S, _ = x.shape
    N = B * S
    row = lambda p: p[None, :]

    win, bpk, wgate, bgate, wa2 = _pack_mixer_weights(w_in, b_in, rg_w_a, rg_w_x, rg_b_a, rg_b_x, gla_w_a2)
    x1, xpk = _mixer(x, win, bpk, conv_w, row(conv_b), wgate, bgate, row(rg_lambda), wa2, row(gla_b_a),
                     row(gla_norm_g), w_proj_rnn.astype(BF16), w_proj_gla.astype(BF16), w_o.astype(BF16),
                     row(b_o), row(ln1_g), row(ln1_b))
    x1f = x1.reshape(N, D_MODEL)
    xpk = xpk.reshape(N, D_MODEL // 2)

    wr = jnp.concatenate([router_w_group.T, jnp.zeros((SUBLANES - N_GROUPS, D_MODEL), F32), router_w_expert.T],
                         axis=0).astype(BF16)
    br = jnp.concatenate([router_b_group, jnp.zeros((SUBLANES - N_GROUPS,), F32), router_b_expert])[:, None]
    info, wts, tcnt = _router(x1f, wr, br)

    tcnt = tcnt[:, :, 0]
    nt = tcnt.shape[0]
    tot = jnp.sum(tcnt, axis=0)
    pcount = (tot + MOE_ROWS - 1) // MOE_ROWS * MOE_ROWS
    pend = jnp.cumsum(pcount)
    base = (pend - pcount)[None, :] + jnp.cumsum(tcnt, axis=0) - tcnt
    base_tok = jnp.repeat(base.T, N // nt, axis=1)
    experts_col = jnp.arange(N_EXPERTS, dtype=jnp.int32)[:, None, None]
    dest = jnp.sum(jnp.where(info[None, 0:2] == experts_col, base_tok[:, None, :], 0), axis=0) + info[2:4]
    nb = -(-(2 * N) // MOE_ROWS) + N_EXPERTS
    P = nb * MOE_ROWS
    nblk = (pend[-1] // MOE_ROWS).astype(jnp.int32)
    blk_start = jnp.minimum(jnp.arange(nb, dtype=jnp.int32), nblk - 1) * MOE_ROWS
    blk_e = jnp.sum((blk_start[:, None] >= pend[None, :]).astype(jnp.int32), axis=1)
    blk_e = jnp.minimum(blk_e, N_EXPERTS - 1)

    xs = _dispatch(dest, xpk, jnp.zeros((P, D_MODEL // 2), jnp.uint32))
    ys = _experts(blk_e, nblk[None], xs, exp_w1.astype(BF16), exp_w3.astype(BF16), exp_w2.astype(BF16))
    out = _combine(dest, x1f, wts, ys, row(ln2_g), row(ln2_b))
    return out.reshape(B, S, D_MODEL)


def kernel(x, w_in, b_in, conv_w, conv_b, rg_w_a, rg_b_a, rg_w_x, rg_b_x, rg_lambda, gla_w_a2, gla_b_a, gla_norm_g, w_proj_rnn, w_proj_gla, w_o, b_o, ln1_g, ln1_b, router_w_group, router_b_group, router_w_expert, router_b_expert, exp_w1, exp_w3, exp_w2, ln2_g, ln2_b):
    h = x
    for l in range(w_in.shape[0]):
        h = _layer(h, w_in[l], b_in[l], conv_w[l], conv_b[l], rg_w_a[l], rg_b_a[l], rg_w_x[l], rg_b_x[l],
                   rg_lambda[l], gla_w_a2[l], gla_b_a[l], gla_norm_g[l], w_proj_rnn[l], w_proj_gla[l], w_o[l],
                   b_o[l], ln1_g[l], ln1_b[l], router_w_group[l], router_b_group[l], router_w_expert[l],
                   router_b_expert[l], exp_w1[l], exp_w3[l], exp_w2[l], ln2_g[l], ln2_b[l])
    return h
```

```python
import functools

import jax
import jax.numpy as jnp
from jax import lax
from jax.experimental import pallas as pl
from jax.experimental.pallas import tpu as pltpu

F32 = jnp.float32
BF16 = jnp.bfloat16

D_MODEL = 1024
RNN_WIDTH = 1024
RNN_BLOCKS = 8
RNN_BLOCK_W = RNN_WIDTH // RNN_BLOCKS
CONV_WIDTH = 4
LRU_C = 8.0
GLA_HEADS = 4
GLA_DK = D_MODEL // 2
GLA_DV = D_MODEL
GLA_HEAD_K = GLA_DK // GLA_HEADS
GLA_HEAD_V = GLA_DV // GLA_HEADS
GLA_RANK = 16
GLA_TAU = 16.0
GLA_CHUNK = 64
N_GROUPS = 4
EXPERTS_PER_GROUP = 8
N_EXPERTS = N_GROUPS * EXPERTS_PER_GROUP
EXPERT_FF = 512
DN_ALPHA = 2.0 ** 0.25
LN_EPS = 1e-5
RMS_EPS = 1e-6

LANES = 128
SUBLANES = 8
VMEM_LIMIT = 56 * 1024 * 1024

C_RX, C_RY, C_Q, C_K, C_V, C_G = 0, 1024, 2048, 2560, 3072, 4096
C_GA, C_GB = 0, 1024
C_MAIN_END = 5120
C_GATES_START = C_MAIN_END + GLA_RANK

MIX_TILE = 256
ROUTE_TILE = 256
MOE_ROWS = 256
DISPATCH_TILE = 512
COMBINE_TILE = 256
ROUTE_ROWS = 8 + N_EXPERTS
PK_TILES = D_MODEL // 2 // LANES
ROW_TILES = D_MODEL // LANES


def _sigmoid(v):
    return 1.0 / (1.0 + jnp.exp(-v))


def _softplus(v):
    return jnp.maximum(v, 0.0) + jnp.log1p(jnp.exp(-jnp.abs(v)))


def _layer_norm(v, g, b):
    mu = jnp.mean(v, axis=-1, keepdims=True)
    c = v - mu
    var = jnp.mean(c * c, axis=-1, keepdims=True)
    return c * lax.rsqrt(var + LN_EPS) * g + b


def _dot(a, b):
    return jnp.dot(a, b, preferred_element_type=F32)


def _dot_nt(a, b):
    return lax.dot_general(a, b, (((1,), (1,)), ((), ())), preferred_element_type=F32)


def _dot_tn(a, b):
    return lax.dot_general(a, b, (((0,), (0,)), ((), ())), preferred_element_type=F32)


def _const_spec(shape):
    nd = len(shape)
    return pl.BlockSpec(shape, lambda *_: (0,) * nd, pipeline_mode=pl.Buffered(1))


def _mixer_kernel(x_ref, wmain_ref, bmain_ref, wbg_ref, bbg_ref, walr_ref, balr_ref,
                  convw_ref, convb_ref, wgate_ref, bgate_ref, lam_ref,
                  wa2_ref, ba2_ref, gnorm_ref, wprnn_ref, wpgla_ref, wo_ref, bo_ref, ln1g_ref, ln1b_ref,
                  x1_ref, xpk_ref, rxbuf, hcar, st_ref, hbuf, *, tile):
    T = tile

    @pl.when(pl.program_id(1) == 0)
    def _():
        rxbuf[0:SUBLANES, :] = jnp.zeros((SUBLANES, RNN_WIDTH), F32)
        hcar[...] = jnp.zeros_like(hcar)
        st_ref[...] = jnp.zeros_like(st_ref)

    x = x_ref[0]
    xb = x.astype(BF16)

    def proj(c0, c1, w_ref=wmain_ref, b_ref=bmain_ref):
        return _dot(xb, w_ref[:, c0:c1]) + b_ref[:, c0:c1]

    rx = proj(C_RX, C_RX + RNN_WIDTH)
    rxbuf[SUBLANES:SUBLANES + T, :] = rx
    u = convb_ref[...] + convw_ref[CONV_WIDTH - 1:CONV_WIDTH, :] * rx
    for j in range(1, CONV_WIDTH):
        u = u + convw_ref[CONV_WIDTH - 1 - j:CONV_WIDTH - j, :] * rxbuf[SUBLANES - j:SUBLANES - j + T, :]
    rxbuf[0:SUBLANES, :] = rxbuf[T:T + SUBLANES, :]

    qk = proj(C_Q, C_Q + 2 * GLA_DK)
    q = qk[:, :GLA_DK] * (GLA_HEAD_K ** -0.5)
    k = qk[:, GLA_DK:]
    v = proj(C_V, C_V + GLA_DV)
    alr = proj(0, LANES, walr_ref, balr_ref)

    r_parts, i_parts = [], []
    for p in range(RNN_BLOCKS // 2):
        up = u[:, 256 * p:256 * (p + 1)].astype(BF16)
        gp = _dot(up, wgate_ref[p])
        r_parts.append(gp[:, :256])
        i_parts.append(gp[:, 256:])
    r = _sigmoid(jnp.concatenate(r_parts, axis=1) + bgate_ref[:, :RNN_WIDTH])
    ig = _sigmoid(jnp.concatenate(i_parts, axis=1) + bgate_ref[:, RNN_WIDTH:])

    z = _dot(alr.astype(BF16), wa2_ref[...]) + ba2_ref[...]
    la = -_softplus(-z) * (1.0 / GLA_TAU)
    ri = lax.broadcasted_iota(jnp.int32, (T, T), 0)
    ci = lax.broadcasted_iota(jnp.int32, (T, T), 1)
    chunk_start = (ri >> 6) << 6
    tri = jnp.where((ci <= ri) & (ci >= chunk_start), 1.0, 0.0).astype(BF16)
    la_hi = la.astype(BF16)
    la_lo = (la - la_hi.astype(F32)).astype(BF16)
    bcum = _dot(tri, la_hi) + _dot(tri, la_lo)

    log_a = (-LRU_C) * r * _softplus(-lam_ref[...])
    a = jnp.exp(log_a)
    m2 = -jnp.tanh(log_a) * (1.0 + a * a)
    bv = jnp.where(m2 > 0.0, m2 * lax.rsqrt(m2), 0.0) * (ig * u)

    cr = lax.broadcasted_iota(jnp.int32, (GLA_CHUNK, GLA_CHUNK), 0)
    cc = lax.broadcasted_iota(jnp.int32, (GLA_CHUNK, GLA_CHUNK), 1)
    causal = cr >= cc
    n_chunks = T // GLA_CHUNK
    heads = [(slice(hd * GLA_HEAD_K, (hd + 1) * GLA_HEAD_K), slice(hd * GLA_HEAD_V, (hd + 1) * GLA_HEAD_V))
             for hd in range(GLA_HEADS)]
    qd_c, ki_c, ke_c, vv_c, dec_c = [], [], [], [], []
    for c in range(n_chunks):
        r0 = c * GLA_CHUNK
        bc = bcum[r0:r0 + GLA_CHUNK, :]
        bl = bcum[r0 + GLA_CHUNK - 1:r0 + GLA_CHUNK, :]
        kc = k[r0:r0 + GLA_CHUNK, :]
        qd_c.append((q[r0:r0 + GLA_CHUNK, :] * jnp.exp(bc)).astype(BF16))
        ki_c.append((kc * jnp.exp(-bc)).astype(BF16))
        ke_c.append((kc * jnp.exp(bl - bc)).astype(BF16))
        vv_c.append(v[r0:r0 + GLA_CHUNK, :].astype(BF16))
        dec_c.append(jnp.exp(bl))
    scores = [[jnp.where(causal, _dot_nt(qd_c[c][:, ks], ki_c[c][:, ks]), 0.0).astype(BF16) for ks, _ in heads]
              for c in range(n_chunks)]
    intra = [[_dot(scores[c][hd], vv_c[c][:, vs]) for hd, (_, vs) in enumerate(heads)] for c in range(n_chunks)]
    incr = [[_dot_tn(vv_c[c][:, vs], ke_c[c][:, ks]) for ks, vs in heads] for c in range(n_chunks)]

    n_groups = T // SUBLANES
    sub = lax.broadcasted_iota(jnp.int32, (n_groups, SUBLANES, RNN_WIDTH), 1)
    sa = a.reshape(n_groups, SUBLANES, RNN_WIDTH)
    sb = bv.reshape(n_groups, SUBLANES, RNN_WIDTH)
    for s in (1, 2, 4):
        keep = sub >= s
        sb = sa * jnp.where(keep, pltpu.roll(sb, s, 1), 0.0) + sb
        sa = sa * jnp.where(keep, pltpu.roll(sa, s, 1), 1.0)
    carry = hcar[0:1, :]
    for gi in range(n_groups):
        hg = sb[gi] + sa[gi] * carry
        hbuf[gi * SUBLANES:(gi + 1) * SUBLANES, :] = hg
        carry = hg[SUBLANES - 1:SUBLANES, :]
    hcar[0:1, :] = carry
    h = hbuf[...]

    ry = proj(C_RY, C_RY + RNN_WIDTH)
    g = proj(C_G, C_G + GLA_DV)
    ga = proj(C_GA, C_GA + D_MODEL, wbg_ref, bbg_ref)
    gb = proj(C_GB, C_GB + D_MODEL, wbg_ref, bbg_ref)

    out_a = _dot((h * jax.nn.gelu(ry)).astype(BF16), wprnn_ref[:, :D_MODEL])

    states = [st_ref[hd] for hd in range(GLA_HEADS)]
    o_chunks = []
    for c in range(n_chunks):
        o_heads = []
        for hd, (ks, _) in enumerate(heads):
            o_heads.append(intra[c][hd] + _dot_nt(qd_c[c][:, ks], states[hd].astype(BF16)))
            states[hd] = states[hd] * dec_c[c][:, ks] + incr[c][hd]
        o_chunks.append(jnp.concatenate(o_heads, axis=1))
    for hd in range(GLA_HEADS):
        st_ref[hd] = states[hd]
    o_all = jnp.concatenate(o_chunks, axis=0)

    o_parts = []
    for hd in range(GLA_HEADS):
        vs = slice(hd * GLA_HEAD_V, (hd + 1) * GLA_HEAD_V)
        oh = o_all[:, vs]
        ms = jnp.mean(oh * oh, axis=-1, keepdims=True)
        o_parts.append(oh * lax.rsqrt(ms + RMS_EPS) * gnorm_ref[:, vs])
    on = jnp.concatenate(o_parts, axis=1) * (g * _sigmoid(g))
    out_b = _dot(on.astype(BF16), wpgla_ref[:, :D_MODEL])

    merged = _sigmoid(ga) * out_a + _sigmoid(gb) * out_b
    y = _dot(merged.astype(BF16), wo_ref[:, :D_MODEL]) + bo_ref[...]
    x1 = _layer_norm(DN_ALPHA * x + y, ln1g_ref[...], ln1b_ref[...])
    x1_ref[0] = x1

    bits = pltpu.bitcast(x1.astype(BF16).astype(F32), jnp.uint32)
    half = D_MODEL // 2
    pk = (bits[:, :half] >> 16) | bits[:, half:]
    for c in range(PK_TILES):
        xpk_ref[pl.ds(c, T, stride=PK_TILES), :] = pk[:, c * LANES:(c + 1) * LANES]


def _mixer(x, wmain, bmain, wbg, bbg, walr, balr, conv_w, conv_b, wgate, bgate, lam, wa2, ba2, gnorm,
           wprnn, wpgla, wo, bo, ln1g, ln1b):
    B, S, _ = x.shape
    T = min(MIX_TILE, S)
    assert S % T == 0 and T % GLA_CHUNK == 0
    weights = (wmain, bmain, wbg, bbg, walr, balr, conv_w, conv_b, wgate, bgate, lam, wa2, ba2, gnorm,
               wprnn, wpgla, wo, bo, ln1g, ln1b)
    return pl.pallas_call(
        functools.partial(_mixer_kernel, tile=T),
        grid=(B, S // T),
        in_specs=[pl.BlockSpec((1, T, D_MODEL), lambda b, s: (b, s, 0))] + [_const_spec(w.shape) for w in weights],
        out_specs=[pl.BlockSpec((1, T, D_MODEL), lambda b, s: (b, s, 0)),
                   pl.BlockSpec((T * PK_TILES, LANES), lambda b, s: (b * (S // T) + s, 0))],
        out_shape=[jax.ShapeDtypeStruct((B, S, D_MODEL), F32),
                   jax.ShapeDtypeStruct((B * S * PK_TILES, LANES), jnp.uint32)],
        scratch_shapes=[pltpu.VMEM((T + SUBLANES, RNN_WIDTH), F32),
                        pltpu.VMEM((SUBLANES, RNN_WIDTH), F32),
                        pltpu.VMEM((GLA_HEADS, GLA_HEAD_V, GLA_HEAD_K), F32),
                        pltpu.VMEM((T, RNN_WIDTH), F32)],
        compiler_params=pltpu.CompilerParams(dimension_semantics=("arbitrary", "arbitrary"),
                                             vmem_limit_bytes=VMEM_LIMIT),
        name="mixer",
    )(x, *weights)


def _router_kernel(x1_ref, wr_ref, br_ref, info_ref, wts_ref, cnt_ref, *, tile):
    T = tile
    logits = _dot_nt(wr_ref[...], x1_ref[...].astype(BF16)) + br_ref[...]
    row8 = lax.broadcasted_iota(jnp.int32, (SUBLANES, T), 0)
    row8f = row8.astype(F32)
    neg = jnp.float32(-jnp.inf)
    first = lambda hit: jnp.min(jnp.where(hit, row8f, float(SUBLANES)), axis=0, keepdims=True)

    gl = jnp.where(row8 < N_GROUPS, logits[0:SUBLANES, :], neg)
    gmax = jnp.max(gl, axis=0, keepdims=True)
    grp = first(gl == gmax)
    p_grp = 1.0 / jnp.sum(jnp.exp(gl - gmax), axis=0, keepdims=True)

    e_sel = jnp.zeros((EXPERTS_PER_GROUP, T), F32)
    for gi in range(N_GROUPS):
        lo = SUBLANES + gi * EXPERTS_PER_GROUP
        e_sel = jnp.where(grp == float(gi), logits[lo:lo + EXPERTS_PER_GROUP, :], e_sel)
    m1 = jnp.max(e_sel, axis=0, keepdims=True)
    i1 = first(e_sel == m1)
    e_rest = jnp.where(row8f == i1, neg, e_sel)
    m2 = jnp.max(e_rest, axis=0, keepdims=True)
    i2 = first(e_rest == m2)
    e21 = jnp.exp(m2 - m1)
    p1 = 1.0 / (1.0 + e21)
    w0 = p_grp * p1
    w1 = p_grp * (e21 * p1)
    eid0 = (grp * EXPERTS_PER_GROUP + i1).astype(jnp.int32)
    eid1 = (grp * EXPERTS_PER_GROUP + i2).astype(jnp.int32)

    erow = lax.broadcasted_iota(jnp.int32, (N_EXPERTS, T), 0)
    oh0 = jnp.where(erow == eid0, 1.0, 0.0)
    oh1 = jnp.where(erow == eid1, 1.0, 0.0)
    both = oh0 + oh1
    ti = lax.broadcasted_iota(jnp.int32, (T, T), 0)
    tj = lax.broadcasted_iota(jnp.int32, (T, T), 1)
    before = jnp.where(ti < tj, 1.0, 0.0).astype(BF16)
    prior = _dot(both.astype(BF16), before)
    rank0 = jnp.sum(prior * oh0, axis=0, keepdims=True).astype(jnp.int32)
    rank1 = jnp.sum(prior * oh1, axis=0, keepdims=True).astype(jnp.int32)
    total = (prior[:, T - 1:T] + both[:, T - 1:T]).astype(jnp.int32)

    info = jnp.where(row8 == 0, eid0, jnp.where(row8 == 1, eid1, jnp.where(row8 == 2, rank0,
                     jnp.where(row8 == 3, rank1, 0))))
    info_ref[...] = info
    wts_ref[...] = jnp.where(row8 == 0, w0, jnp.where(row8 == 1, w1, 0.0))
    cnt_ref[0] = jnp.broadcast_to(total, (N_EXPERTS, LANES))


def _router(x1f, wr, br):
    N = x1f.shape[0]
    T = min(ROUTE_TILE, N)
    assert N % T == 0
    nt = N // T
    return pl.pallas_call(
        functools.partial(_router_kernel, tile=T),
        grid=(nt,),
        in_specs=[pl.BlockSpec((T, D_MODEL), lambda i: (i, 0)), _const_spec(wr.shape), _const_spec(br.shape)],
        out_specs=[pl.BlockSpec((SUBLANES, T), lambda i: (0, i)),
                   pl.BlockSpec((SUBLANES, T), lambda i: (0, i)),
                   pl.BlockSpec((1, N_EXPERTS, LANES), lambda i: (i, 0, 0))],
        out_shape=[jax.ShapeDtypeStruct((SUBLANES, N), jnp.int32),
                   jax.ShapeDtypeStruct((SUBLANES, N), F32),
                   jax.ShapeDtypeStruct((nt, N_EXPERTS, LANES), jnp.int32)],
        compiler_params=pltpu.CompilerParams(dimension_semantics=("arbitrary",)),
        name="router",
    )(x1f, wr, br)


def _dispatch_kernel(dest_ref, xpk_ref, xs_in_ref, xs_ref, sem, *, tile):
    del xs_in_ref
    T = tile

    for t in range(T):
        for kk in range(2):
            row = pl.multiple_of(dest_ref[kk, t] * PK_TILES, PK_TILES)
            pltpu.make_async_copy(xpk_ref.at[pl.ds(t * PK_TILES, PK_TILES)], xs_ref.at[pl.ds(row, PK_TILES)],
                                  sem).start(priority=kk)
    for kk in range(2):
        pltpu.make_async_copy(xpk_ref, xs_ref.at[pl.ds(0, T * PK_TILES)], sem).wait()


def _dispatch(dest, xpk, xs_zero):
    N = xpk.shape[0] // PK_TILES
    T = min(DISPATCH_TILE, N)
    assert N % T == 0
    return pl.pallas_call(
        functools.partial(_dispatch_kernel, tile=T),
        grid=(N // T,),
        in_specs=[pl.BlockSpec((2, T), lambda i: (0, i), memory_space=pltpu.SMEM),
                  pl.BlockSpec((T * PK_TILES, LANES), lambda i: (i, 0)),
                  pl.BlockSpec(memory_space=pl.ANY)],
        out_specs=pl.BlockSpec(memory_space=pl.ANY),
        out_shape=jax.ShapeDtypeStruct(xs_zero.shape, xs_zero.dtype),
        scratch_shapes=[pltpu.SemaphoreType.DMA(())],
        input_output_aliases={2: 0},
        compiler_params=pltpu.CompilerParams(dimension_semantics=("arbitrary",), has_side_effects=True),
        name="dispatch",
    )(dest, xpk, xs_zero)


def _experts_kernel(blk_e_ref, nblk_ref, xs_ref, w1_ref, w3_ref, w2_ref, ys_ref, w1b, w3b, w2b):
    i = pl.program_id(0)
    R = MOE_ROWS

    @pl.when((i == 0) | (blk_e_ref[i] != blk_e_ref[jnp.maximum(i - 1, 0)]))
    def _():
        w1b[...] = w1_ref[0].astype(BF16)
        w3b[...] = w3_ref[0].astype(BF16)
        w2b[...] = w2_ref[0].astype(BF16)

    @pl.when(i < nblk_ref[0])
    def _():
        u = jnp.concatenate([xs_ref[pl.ds(c, R, stride=PK_TILES), :] for c in range(PK_TILES)], axis=1)
        lo = pltpu.bitcast(u << 16, F32).astype(BF16)
        hi = pltpu.bitcast(u & jnp.uint32(0xFFFF0000), F32).astype(BF16)
        xb = jnp.concatenate([lo, hi], axis=1)
        h1 = _dot(xb, w1b[...])
        h3 = _dot(xb, w3b[...])
        hact = (h1 * _sigmoid(h1)) * h3
        y = _dot(hact.astype(BF16), w2b[...])
        for c in range(ROW_TILES):
            ys_ref[pl.ds(c, R, stride=ROW_TILES), :] = y[:, c * LANES:(c + 1) * LANES]

    @pl.when(i >= nblk_ref[0])
    def _():
        ys_ref[...] = jnp.zeros_like(ys_ref)


def _experts(blk_e, nblk, xs, w1, w3, w2):
    P = xs.shape[0] // PK_TILES
    nb = P // MOE_ROWS
    grid_spec = pltpu.PrefetchScalarGridSpec(
        num_scalar_prefetch=2,
        grid=(nb,),
        in_specs=[pl.BlockSpec((MOE_ROWS * PK_TILES, LANES), lambda i, be, n: (i, 0)),
                  pl.BlockSpec((1, D_MODEL, EXPERT_FF), lambda i, be, n: (be[i], 0, 0)),
                  pl.BlockSpec((1, D_MODEL, EXPERT_FF), lambda i, be, n: (be[i], 0, 0)),
                  pl.BlockSpec((1, EXPERT_FF, D_MODEL), lambda i, be, n: (be[i], 0, 0))],
        out_specs=pl.BlockSpec((MOE_ROWS * ROW_TILES, LANES), lambda i, be, n: (i, 0)),
        scratch_shapes=[pltpu.VMEM((D_MODEL, EXPERT_FF), BF16), pltpu.VMEM((D_MODEL, EXPERT_FF), BF16),
                        pltpu.VMEM((EXPERT_FF, D_MODEL), BF16)],
    )
    return pl.pallas_call(
        _experts_kernel,
        grid_spec=grid_spec,
        out_shape=jax.ShapeDtypeStruct((P * ROW_TILES, LANES), F32),
        compiler_params=pltpu.CompilerParams(dimension_semantics=("arbitrary",), vmem_limit_bytes=VMEM_LIMIT),
        name="experts",
    )(blk_e, nblk, xs, w1, w3, w2)


def _combine_kernel(dest_ref, x1_ref, wts_ref, ys_ref, g_ref, b_ref, out_ref, gbuf0, gbuf1, sem, *, tile):
    T = tile
    gbufs = (gbuf0, gbuf1)

    for t in range(T):
        for kk in range(2):
            row = pl.multiple_of(dest_ref[kk, t] * ROW_TILES, ROW_TILES)
            pltpu.make_async_copy(ys_ref.at[pl.ds(row, ROW_TILES)], gbufs[kk].at[pl.ds(t * ROW_TILES, ROW_TILES)],
                                  sem).start(priority=kk)
    for kk in range(2):
        pltpu.make_async_copy(ys_ref.at[pl.ds(0, T * ROW_TILES)], gbufs[kk], sem).wait()

    wpad = jnp.concatenate([wts_ref[...], jnp.zeros((LANES - SUBLANES, T), F32)], axis=0)
    wt = wpad.T
    rows = lambda buf: jnp.concatenate([buf[pl.ds(c, T, stride=ROW_TILES), :] for c in range(ROW_TILES)], axis=1)
    y = wt[:, 0:1] * rows(gbuf0) + wt[:, 1:2] * rows(gbuf1)
    out_ref[...] = _layer_norm(DN_ALPHA * x1_ref[...] + y, g_ref[...], b_ref[...])


def _combine(dest, x1f, wts, ys, g, b):
    N = x1f.shape[0]
    T = min(COMBINE_TILE, N)
    assert N % T == 0
    return pl.pallas_call(
        functools.partial(_combine_kernel, tile=T),
        grid=(N // T,),
        in_specs=[pl.BlockSpec((2, T), lambda i: (0, i), memory_space=pltpu.SMEM),
                  pl.BlockSpec((T, D_MODEL), lambda i: (i, 0)),
                  pl.BlockSpec((SUBLANES, T), lambda i: (0, i)),
                  pl.BlockSpec(memory_space=pl.ANY),
                  _const_spec(g.shape), _const_spec(b.shape)],
        out_specs=pl.BlockSpec((T, D_MODEL), lambda i: (i, 0)),
        out_shape=jax.ShapeDtypeStruct((N, D_MODEL), F32),
        scratch_shapes=[pltpu.VMEM((T * ROW_TILES, LANES), F32), pltpu.VMEM((T * ROW_TILES, LANES), F32),
                        pltpu.SemaphoreType.DMA(())],
        compiler_params=pltpu.CompilerParams(dimension_semantics=("arbitrary",)),
        name="combine",
    )(dest, x1f, wts, ys, g, b)


def _odd_tiles(w):
    return jnp.pad(w, ((0, 0), (0, LANES))).astype(BF16)


def _pack_mixer_weights(w_in, b_in, rg_w_a, rg_w_x, rg_b_a, rg_b_x, gla_w_a2):
    wmain = _odd_tiles(w_in[:, :C_MAIN_END])
    wbg = _odd_tiles(w_in[:, C_GATES_START:])
    walr = jnp.pad(w_in[:, C_MAIN_END:C_GATES_START], ((0, 0), (0, LANES - GLA_RANK))).astype(BF16)
    bmain = b_in[None, :C_MAIN_END]
    bbg = b_in[None, C_GATES_START:]
    balr = jnp.pad(b_in[None, C_MAIN_END:C_GATES_START], ((0, 0), (0, LANES - GLA_RANK)))
    zero = jnp.zeros((RNN_BLOCK_W, RNN_BLOCK_W), w_in.dtype)
    tiles = []
    for p in range(RNN_BLOCKS // 2):
        top = jnp.concatenate([rg_w_a[2 * p], zero, rg_w_x[2 * p], zero], axis=1)
        bot = jnp.concatenate([zero, rg_w_a[2 * p + 1], zero, rg_w_x[2 * p + 1]], axis=1)
        tiles.append(jnp.concatenate([top, bot], axis=0))
    wgate = jnp.stack(tiles).astype(BF16)
    bgate = jnp.concatenate([rg_b_a, rg_b_x])[None, :]
    wa2 = jnp.concatenate([gla_w_a2, jnp.zeros((LANES - GLA_RANK, GLA_DK), gla_w_a2.dtype)], axis=0).astype(BF16)
    return (wmain, bmain, wbg, bbg, walr, balr), wgate, bgate, wa2


def _layer(x, w_in, b_in, conv_w, conv_b, rg_w_a, rg_b_a, rg_w_x, rg_b_x, rg_lambda, gla_w_a2, gla_b_a,
           gla_norm_g, w_proj_rnn, w_proj_gla, w_o, b_o, ln1_g, ln1_b, router_w_group, router_b_group,
           router_w_expert, router_b_expert, exp_w1, exp_w3, exp_w2, ln2_g, ln2_b):
    B, S, _ = x.shape
    N = B * S
    row = lambda p: p[None, :]

    w_slices, wgate, bgate, wa2 = _pack_mixer_weights(w_in, b_in, rg_w_a, rg_w_x, rg_b_a, rg_b_x, gla_w_a2)
    x1, xpk = _mixer(x, *w_slices, conv_w, row(conv_b), wgate, bgate, row(rg_lambda), wa2, row(gla_b_a),
                     row(gla_norm_g), _odd_tiles(w_proj_rnn), _odd_tiles(w_proj_gla), _odd_tiles(w_o),
                     row(b_o), row(ln1_g), row(ln1_b))
    x1f = x1.reshape(N, D_MODEL)

    wr = jnp.concatenate([router_w_group.T, jnp.zeros((SUBLANES - N_GROUPS, D_MODEL), F32), router_w_expert.T],
                         axis=0).astype(BF16)
    br = jnp.concatenate([router_b_group, jnp.zeros((SUBLANES - N_GROUPS,), F32), router_b_expert])[:, None]
    info, wts, tcnt = _router(x1f, wr, br)

    tcnt = tcnt[:, :, 0]
    nt = tcnt.shape[0]
    tot = jnp.sum(tcnt, axis=0)
    pcount = (tot + MOE_ROWS - 1) // MOE_ROWS * MOE_ROWS
    pend = jnp.cumsum(pcount)
    base = (pend - pcount)[None, :] + jnp.cumsum(tcnt, axis=0) - tcnt
    base_tok = jnp.repeat(base.T, N // nt, axis=1)
    experts_col = jnp.arange(N_EXPERTS, dtype=jnp.int32)[:, None, None]
    dest = jnp.sum(jnp.where(info[None, 0:2] == experts_col, base_tok[:, None, :], 0), axis=0) + info[2:4]
    nb = -(-(2 * N) // MOE_ROWS) + N_EXPERTS
    P = nb * MOE_ROWS
    nblk = (pend[-1] // MOE_ROWS).astype(jnp.int32)
    blk_start = jnp.minimum(jnp.arange(nb, dtype=jnp.int32), nblk - 1) * MOE_ROWS
    blk_e = jnp.sum((blk_start[:, None] >= pend[None, :]).astype(jnp.int32), axis=1)
    blk_e = jnp.minimum(blk_e, N_EXPERTS - 1)

    xs = _dispatch(dest, xpk, jnp.zeros((P * PK_TILES, LANES), jnp.uint32))
    ys = _experts(blk_e, nblk[None], xs, exp_w1, exp_w3, exp_w2)
    out = _combine(dest, x1f, wts, ys, row(ln2_g), row(ln2_b))
    return out.reshape(B, S, D_MODEL)


def kernel(x, w_in, b_in, conv_w, conv_b, rg_w_a, rg_b_a, rg_w_x, rg_b_x, rg_lambda, gla_w_a2, gla_b_a, gla_norm_g, w_proj_rnn, w_proj_gla, w_o, b_o, ln1_g, ln1_b, router_w_group, router_b_group, router_w_expert, router_b_expert, exp_w1, exp_w3, exp_w2, ln2_g, ln2_b):
    h = x
    for l in range(w_in.shape[0]):
        h = _layer(h, w_in[l], b_in[l], conv_w[l], conv_b[l], rg_w_a[l], rg_b_a[l], rg_w_x[l], rg_b_x[l],
                   rg_lambda[l], gla_w_a2[l], gla_b_a[l], gla_norm_g[l], w_proj_rnn[l], w_proj_gla[l], w_o[l],
                   b_o[l], ln1_g[l], ln1_b[l], router_w_group[l], router_b_group[l], router_w_expert[l],
                   router_b_expert[l], exp_w1[l], exp_w3[l], exp_w2[l], ln2_g[l], ln2_b[l])
    return h
```

```python
import functools

import jax
import jax.numpy as jnp
from jax import lax
from jax.experimental import pallas as pl
from jax.experimental.pallas import tpu as pltpu

F32 = jnp.float32
BF16 = jnp.bfloat16

D_MODEL = 1024
RNN_WIDTH = 1024
RNN_BLOCKS = 8
RNN_BLOCK_W = RNN_WIDTH // RNN_BLOCKS
CONV_WIDTH = 4
LRU_C = 8.0
GLA_HEADS = 4
GLA_DK = D_MODEL // 2
GLA_DV = D_MODEL
GLA_HEAD_K = GLA_DK // GLA_HEADS
GLA_HEAD_V = GLA_DV // GLA_HEADS
GLA_RANK = 16
GLA_TAU = 16.0
GLA_CHUNK = 64
N_GROUPS = 4
EXPERTS_PER_GROUP = 8
N_EXPERTS = N_GROUPS * EXPERTS_PER_GROUP
EXPERT_FF = 512
DN_ALPHA = 2.0 ** 0.25
LN_EPS = 1e-5
RMS_EPS = 1e-6

LANES = 128
SUBLANES = 8
VMEM_LIMIT = 56 * 1024 * 1024

C_RX, C_RY, C_Q, C_K, C_V, C_G = 0, 1024, 2048, 2560, 3072, 4096
C_GA, C_GB = 0, 1024
C_MAIN_END = 5120
C_GATES_START = C_MAIN_END + GLA_RANK

MIX_TILE = 256
MOE_ROWS = 256
DISPATCH_TILE = 512
COMBINE_TILE = 256
ROUTE_ROWS = 8 + N_EXPERTS
PK_TILES = D_MODEL // 2 // LANES


def _sigmoid(v):
    return 1.0 / (1.0 + jnp.exp(-v))


def _softplus(v):
    return jnp.maximum(v, 0.0) + jnp.log1p(jnp.exp(-jnp.abs(v)))


def _layer_norm(v, g, b):
    mu = jnp.mean(v, axis=-1, keepdims=True)
    c = v - mu
    var = jnp.mean(c * c, axis=-1, keepdims=True)
    return c * lax.rsqrt(var + LN_EPS) * g + b


def _dot(a, b):
    return jnp.dot(a, b, preferred_element_type=F32)


def _dot_nt(a, b):
    return lax.dot_general(a, b, (((1,), (1,)), ((), ())), preferred_element_type=F32)


def _dot_tn(a, b):
    return lax.dot_general(a, b, (((0,), (0,)), ((), ())), preferred_element_type=F32)


def _pack_bf16_pairs(v):
    half = v.shape[1] // 2
    bits = pltpu.bitcast(v.astype(BF16).astype(F32), jnp.uint32)
    return (bits[:, :half] >> 16) | bits[:, half:]


def _unpack_bf16_pairs(u):
    lo = pltpu.bitcast(u << 16, F32)
    hi = pltpu.bitcast(u & jnp.uint32(0xFFFF0000), F32)
    return jnp.concatenate([lo, hi], axis=1)


def _const_spec(shape):
    nd = len(shape)
    return pl.BlockSpec(shape, lambda *_: (0,) * nd, pipeline_mode=pl.Buffered(1))


def _mixer_kernel(x_ref, wmain_ref, bmain_ref, wbg_ref, bbg_ref, walr_ref, balr_ref,
                  convw_ref, convb_ref, wgate_ref, bgate_ref, lam_ref,
                  wa2_ref, ba2_ref, gnorm_ref, wprnn_ref, wpgla_ref, wo_ref, bo_ref, ln1g_ref, ln1b_ref,
                  wr_ref, br_ref,
                  x1_ref, xpk_ref, info_ref, wts_ref, cnt_ref, rxbuf, hcar, st_ref, hbuf, *, tile):
    T = tile

    @pl.when(pl.program_id(1) == 0)
    def _():
        rxbuf[0:SUBLANES, :] = jnp.zeros((SUBLANES, RNN_WIDTH), F32)
        hcar[...] = jnp.zeros_like(hcar)
        st_ref[...] = jnp.zeros_like(st_ref)

    x = x_ref[0]
    xb = x.astype(BF16)

    def proj(c0, c1, w_ref=wmain_ref, b_ref=bmain_ref):
        return _dot(xb, w_ref[:, c0:c1]) + b_ref[:, c0:c1]

    rx = proj(C_RX, C_RX + RNN_WIDTH)
    rxbuf[SUBLANES:SUBLANES + T, :] = rx
    u = convb_ref[...] + convw_ref[CONV_WIDTH - 1:CONV_WIDTH, :] * rx
    for j in range(1, CONV_WIDTH):
        u = u + convw_ref[CONV_WIDTH - 1 - j:CONV_WIDTH - j, :] * rxbuf[SUBLANES - j:SUBLANES - j + T, :]
    rxbuf[0:SUBLANES, :] = rxbuf[T:T + SUBLANES, :]

    qk = proj(C_Q, C_Q + 2 * GLA_DK)
    q = qk[:, :GLA_DK] * (GLA_HEAD_K ** -0.5)
    k = qk[:, GLA_DK:]
    v = proj(C_V, C_V + GLA_DV)
    alr = proj(0, LANES, walr_ref, balr_ref)

    r_parts, i_parts = [], []
    for p in range(RNN_BLOCKS // 2):
        up = u[:, 256 * p:256 * (p + 1)].astype(BF16)
        gp = _dot(up, wgate_ref[p])
        r_parts.append(gp[:, :256])
        i_parts.append(gp[:, 256:])
    r = _sigmoid(jnp.concatenate(r_parts, axis=1) + bgate_ref[:, :RNN_WIDTH])
    ig = _sigmoid(jnp.concatenate(i_parts, axis=1) + bgate_ref[:, RNN_WIDTH:])

    z = _dot(alr.astype(BF16), wa2_ref[...]) + ba2_ref[...]
    la = -_softplus(-z) * (1.0 / GLA_TAU)
    ri = lax.broadcasted_iota(jnp.int32, (T, T), 0)
    ci = lax.broadcasted_iota(jnp.int32, (T, T), 1)
    chunk_start = (ri >> 6) << 6
    tri = jnp.where((ci <= ri) & (ci >= chunk_start), 1.0, 0.0).astype(BF16)
    la_hi = la.astype(BF16)
    la_lo = (la - la_hi.astype(F32)).astype(BF16)
    bcum = _dot(tri, la_hi) + _dot(tri, la_lo)

    log_a = (-LRU_C) * r * _softplus(-lam_ref[...])
    a = jnp.exp(log_a)
    m2 = -jnp.tanh(log_a) * (1.0 + a * a)
    bv = jnp.where(m2 > 0.0, m2 * lax.rsqrt(m2), 0.0) * (ig * u)

    cr = lax.broadcasted_iota(jnp.int32, (GLA_CHUNK, GLA_CHUNK), 0)
    cc = lax.broadcasted_iota(jnp.int32, (GLA_CHUNK, GLA_CHUNK), 1)
    causal = cr >= cc
    n_chunks = T // GLA_CHUNK
    heads = [(slice(hd * GLA_HEAD_K, (hd + 1) * GLA_HEAD_K), slice(hd * GLA_HEAD_V, (hd + 1) * GLA_HEAD_V))
             for hd in range(GLA_HEADS)]
    qd_c, ki_c, ke_c, vv_c, dec_c = [], [], [], [], []
    for c in range(n_chunks):
        r0 = c * GLA_CHUNK
        bc = bcum[r0:r0 + GLA_CHUNK, :]
        bl = bcum[r0 + GLA_CHUNK - 1:r0 + GLA_CHUNK, :]
        kc = k[r0:r0 + GLA_CHUNK, :]
        qd_c.append((q[r0:r0 + GLA_CHUNK, :] * jnp.exp(bc)).astype(BF16))
        ki_c.append((kc * jnp.exp(-bc)).astype(BF16))
        ke_c.append((kc * jnp.exp(bl - bc)).astype(BF16))
        vv_c.append(v[r0:r0 + GLA_CHUNK, :].astype(BF16))
        dec_c.append(jnp.exp(bl))
    scores = [[jnp.where(causal, _dot_nt(qd_c[c][:, ks], ki_c[c][:, ks]), 0.0).astype(BF16) for ks, _ in heads]
              for c in range(n_chunks)]
    intra = [[_dot(scores[c][hd], vv_c[c][:, vs]) for hd, (_, vs) in enumerate(heads)] for c in range(n_chunks)]
    incr = [[_dot_tn(vv_c[c][:, vs], ke_c[c][:, ks]) for ks, vs in heads] for c in range(n_chunks)]

    n_groups = T // SUBLANES
    sub = lax.broadcasted_iota(jnp.int32, (n_groups, SUBLANES, RNN_WIDTH), 1)
    sa = a.reshape(n_groups, SUBLANES, RNN_WIDTH)
    sb = bv.reshape(n_groups, SUBLANES, RNN_WIDTH)
    for s in (1, 2, 4):
        keep = sub >= s
        sb = sa * jnp.where(keep, pltpu.roll(sb, s, 1), 0.0) + sb
        sa = sa * jnp.where(keep, pltpu.roll(sa, s, 1), 1.0)
    carry = hcar[0:1, :]
    for gi in range(n_groups):
        hg = sb[gi] + sa[gi] * carry
        hbuf[gi * SUBLANES:(gi + 1) * SUBLANES, :] = hg
        carry = hg[SUBLANES - 1:SUBLANES, :]
    hcar[0:1, :] = carry
    h = hbuf[...]

    ry = proj(C_RY, C_RY + RNN_WIDTH)
    g = proj(C_G, C_G + GLA_DV)
    ga = proj(C_GA, C_GA + D_MODEL, wbg_ref, bbg_ref)
    gb = proj(C_GB, C_GB + D_MODEL, wbg_ref, bbg_ref)

    out_a = _dot((h * jax.nn.gelu(ry)).astype(BF16), wprnn_ref[:, :D_MODEL])

    states = [st_ref[hd] for hd in range(GLA_HEADS)]
    o_chunks = []
    for c in range(n_chunks):
        o_heads = []
        for hd, (ks, _) in enumerate(heads):
            o_heads.append(intra[c][hd] + _dot_nt(qd_c[c][:, ks], states[hd].astype(BF16)))
            states[hd] = states[hd] * dec_c[c][:, ks] + incr[c][hd]
        o_chunks.append(jnp.concatenate(o_heads, axis=1))
    for hd in range(GLA_HEADS):
        st_ref[hd] = states[hd]
    o_all = jnp.concatenate(o_chunks, axis=0)

    o_parts = []
    for hd in range(GLA_HEADS):
        vs = slice(hd * GLA_HEAD_V, (hd + 1) * GLA_HEAD_V)
        oh = o_all[:, vs]
        ms = jnp.mean(oh * oh, axis=-1, keepdims=True)
        o_parts.append(oh * lax.rsqrt(ms + RMS_EPS) * gnorm_ref[:, vs])
    on = jnp.concatenate(o_parts, axis=1) * (g * _sigmoid(g))
    out_b = _dot(on.astype(BF16), wpgla_ref[:, :D_MODEL])

    merged = _sigmoid(ga) * out_a + _sigmoid(gb) * out_b
    y = _dot(merged.astype(BF16), wo_ref[:, :D_MODEL]) + bo_ref[...]
    x1 = _layer_norm(DN_ALPHA * x + y, ln1g_ref[...], ln1b_ref[...])
    x1_ref[0] = x1

    x1b = x1.astype(BF16)
    info, wts, total = _route_tile(x1b, wr_ref[...], br_ref[...])
    info_ref[...] = info
    wts_ref[...] = wts
    cnt_ref[0] = jnp.broadcast_to(total, (N_EXPERTS, LANES))

    pk = _pack_bf16_pairs(x1)
    for c in range(PK_TILES):
        xpk_ref[pl.ds(c, T, stride=PK_TILES), :] = pk[:, c * LANES:(c + 1) * LANES]


def _mixer(x, wmain, bmain, wbg, bbg, walr, balr, conv_w, conv_b, wgate, bgate, lam, wa2, ba2, gnorm,
           wprnn, wpgla, wo, bo, ln1g, ln1b, wr, br):
    B, S, _ = x.shape
    T = min(MIX_TILE, S)
    assert S % T == 0 and T % GLA_CHUNK == 0
    nt = B * (S // T)
    tok_tile = lambda b, s: (0, b * (S // T) + s)
    weights = (wmain, bmain, wbg, bbg, walr, balr, conv_w, conv_b, wgate, bgate, lam, wa2, ba2, gnorm,
               wprnn, wpgla, wo, bo, ln1g, ln1b, wr, br)
    return pl.pallas_call(
        functools.partial(_mixer_kernel, tile=T),
        grid=(B, S // T),
        in_specs=[pl.BlockSpec((1, T, D_MODEL), lambda b, s: (b, s, 0))] + [_const_spec(w.shape) for w in weights],
        out_specs=[pl.BlockSpec((1, T, D_MODEL), lambda b, s: (b, s, 0)),
                   pl.BlockSpec((T * PK_TILES, LANES), lambda b, s: (b * (S // T) + s, 0)),
                   pl.BlockSpec((SUBLANES, T), tok_tile),
                   pl.BlockSpec((SUBLANES, T), tok_tile),
                   pl.BlockSpec((1, N_EXPERTS, LANES), lambda b, s: (b * (S // T) + s, 0, 0))],
        out_shape=[jax.ShapeDtypeStruct((B, S, D_MODEL), F32),
                   jax.ShapeDtypeStruct((B * S * PK_TILES, LANES), jnp.uint32),
                   jax.ShapeDtypeStruct((SUBLANES, B * S), jnp.int32),
                   jax.ShapeDtypeStruct((SUBLANES, B * S), F32),
                   jax.ShapeDtypeStruct((nt, N_EXPERTS, LANES), jnp.int32)],
        scratch_shapes=[pltpu.VMEM((T + SUBLANES, RNN_WIDTH), F32),
                        pltpu.VMEM((SUBLANES, RNN_WIDTH), F32),
                        pltpu.VMEM((GLA_HEADS, GLA_HEAD_V, GLA_HEAD_K), F32),
                        pltpu.VMEM((T, RNN_WIDTH), F32)],
        compiler_params=pltpu.CompilerParams(dimension_semantics=("arbitrary", "arbitrary"),
                                             vmem_limit_bytes=VMEM_LIMIT),
        name="mixer",
    )(x, *weights)


def _route_tile(x1b, wr, br):
    T = x1b.shape[0]
    logits = _dot_nt(wr, x1b) + br
    row8 = lax.broadcasted_iota(jnp.int32, (SUBLANES, T), 0)
    row8f = row8.astype(F32)
    neg = jnp.float32(-jnp.inf)
    first = lambda hit: jnp.min(jnp.where(hit, row8f, float(SUBLANES)), axis=0, keepdims=True)

    gl = jnp.where(row8 < N_GROUPS, logits[0:SUBLANES, :], neg)
    gmax = jnp.max(gl, axis=0, keepdims=True)
    grp = first(gl == gmax)
    p_grp = 1.0 / jnp.sum(jnp.exp(gl - gmax), axis=0, keepdims=True)

    e_sel = jnp.zeros((EXPERTS_PER_GROUP, T), F32)
    for gi in range(N_GROUPS):
        lo = SUBLANES + gi * EXPERTS_PER_GROUP
        e_sel = jnp.where(grp == float(gi), logits[lo:lo + EXPERTS_PER_GROUP, :], e_sel)
    m1 = jnp.max(e_sel, axis=0, keepdims=True)
    i1 = first(e_sel == m1)
    e_rest = jnp.where(row8f == i1, neg, e_sel)
    m2 = jnp.max(e_rest, axis=0, keepdims=True)
    i2 = first(e_rest == m2)
    e21 = jnp.exp(m2 - m1)
    p1 = 1.0 / (1.0 + e21)
    w0 = p_grp * p1
    w1 = p_grp * (e21 * p1)
    eid0 = (grp * EXPERTS_PER_GROUP + i1).astype(jnp.int32)
    eid1 = (grp * EXPERTS_PER_GROUP + i2).astype(jnp.int32)

    erow = lax.broadcasted_iota(jnp.int32, (N_EXPERTS, T), 0)
    oh0 = jnp.where(erow == eid0, 1.0, 0.0)
    oh1 = jnp.where(erow == eid1, 1.0, 0.0)
    both = oh0 + oh1
    ti = lax.broadcasted_iota(jnp.int32, (T, T), 0)
    tj = lax.broadcasted_iota(jnp.int32, (T, T), 1)
    before = jnp.where(ti < tj, 1.0, 0.0).astype(BF16)
    prior = _dot(both.astype(BF16), before)
    rank0 = jnp.sum(prior * oh0, axis=0, keepdims=True).astype(jnp.int32)
    rank1 = jnp.sum(prior * oh1, axis=0, keepdims=True).astype(jnp.int32)
    total = (prior[:, T - 1:T] + both[:, T - 1:T]).astype(jnp.int32)

    info = jnp.where(row8 == 0, eid0, jnp.where(row8 == 1, eid1, jnp.where(row8 == 2, rank0,
                     jnp.where(row8 == 3, rank1, 0))))
    return info, jnp.where(row8 == 0, w0, jnp.where(row8 == 1, w1, 0.0)), total


def _dispatch_kernel(zero_blk_ref, dest_ref, xpk_ref, xs_ref, zbuf, sem, zsem, *, tile):
    T = tile

    @pl.when(pl.program_id(0) == 0)
    def _():
        zbuf[...] = jnp.zeros_like(zbuf)

        def zero_copy(j):
            row = pl.multiple_of(jnp.maximum(zero_blk_ref[j], 0) * PK_TILES, MOE_ROWS * PK_TILES)
            return pltpu.make_async_copy(zbuf, xs_ref.at[pl.ds(row, MOE_ROWS * PK_TILES)], zsem)

        for j in range(2 * N_EXPERTS):
            @pl.when(zero_blk_ref[j] >= 0)
            def _():
                zero_copy(j).start()
        for j in range(2 * N_EXPERTS):
            @pl.when(zero_blk_ref[j] >= 0)
            def _():
                zero_copy(j).wait()

    for t in range(T):
        for kk in range(2):
            row = pl.multiple_of(dest_ref[kk, t] * PK_TILES, PK_TILES)
            pltpu.make_async_copy(xpk_ref.at[pl.ds(t * PK_TILES, PK_TILES)], xs_ref.at[pl.ds(row, PK_TILES)],
                                  sem).start(priority=kk)
    for kk in range(2):
        pltpu.make_async_copy(xpk_ref, xs_ref.at[pl.ds(0, T * PK_TILES)], sem).wait()


def _dispatch(last_blk, dest, xpk, n_rows):
    N = xpk.shape[0] // PK_TILES
    T = min(DISPATCH_TILE, N)
    assert N % T == 0
    grid_spec = pltpu.PrefetchScalarGridSpec(
        num_scalar_prefetch=1,
        grid=(N // T,),
        in_specs=[pl.BlockSpec((2, T), lambda i, lb: (0, i), memory_space=pltpu.SMEM),
                  pl.BlockSpec((T * PK_TILES, LANES), lambda i, lb: (i, 0))],
        out_specs=pl.BlockSpec(memory_space=pl.ANY),
        scratch_shapes=[pltpu.VMEM((MOE_ROWS * PK_TILES, LANES), jnp.uint32),
                        pltpu.SemaphoreType.DMA(()), pltpu.SemaphoreType.DMA(())],
    )
    return pl.pallas_call(
        functools.partial(_dispatch_kernel, tile=T),
        grid_spec=grid_spec,
        out_shape=jax.ShapeDtypeStruct((n_rows * PK_TILES, LANES), jnp.uint32),
        compiler_params=pltpu.CompilerParams(dimension_semantics=("arbitrary",), has_side_effects=True),
        name="dispatch",
    )(last_blk, dest, xpk)


def _experts_kernel(blk_e_ref, nblk_ref, xs_ref, w1_ref, w3_ref, w2_ref, ys_ref, w1b, w3b, w2b):
    i = pl.program_id(0)
    R = MOE_ROWS

    @pl.when((i == 0) | (blk_e_ref[i] != blk_e_ref[jnp.maximum(i - 1, 0)]))
    def _():
        w1b[...] = w1_ref[0].astype(BF16)
        w3b[...] = w3_ref[0].astype(BF16)
        w2b[...] = w2_ref[0].astype(BF16)

    @pl.when(i < nblk_ref[0])
    def _():
        u = jnp.concatenate([xs_ref[pl.ds(c, R, stride=PK_TILES), :] for c in range(PK_TILES)], axis=1)
        xb = _unpack_bf16_pairs(u).astype(BF16)
        h1 = _dot(xb, w1b[...])
        h3 = _dot(xb, w3b[...])
        hact = (h1 * _sigmoid(h1)) * h3
        y = _dot(hact.astype(BF16), w2b[...])
        pk = _pack_bf16_pairs(y)
        for c in range(PK_TILES):
            ys_ref[pl.ds(c, R, stride=PK_TILES), :] = pk[:, c * LANES:(c + 1) * LANES]

    @pl.when(i >= nblk_ref[0])
    def _():
        ys_ref[...] = jnp.zeros_like(ys_ref)


def _experts(blk_e, nblk, xs, w1, w3, w2):
    P = xs.shape[0] // PK_TILES
    nb = P // MOE_ROWS
    grid_spec = pltpu.PrefetchScalarGridSpec(
        num_scalar_prefetch=2,
        grid=(nb,),
        in_specs=[pl.BlockSpec((MOE_ROWS * PK_TILES, LANES), lambda i, be, n: (jnp.minimum(i, n[0] - 1), 0)),
                  pl.BlockSpec((1, D_MODEL, EXPERT_FF), lambda i, be, n: (be[i], 0, 0)),
                  pl.BlockSpec((1, D_MODEL, EXPERT_FF), lambda i, be, n: (be[i], 0, 0)),
                  pl.BlockSpec((1, EXPERT_FF, D_MODEL), lambda i, be, n: (be[i], 0, 0))],
        out_specs=pl.BlockSpec((MOE_ROWS * PK_TILES, LANES), lambda i, be, n: (i, 0)),
        scratch_shapes=[pltpu.VMEM((D_MODEL, EXPERT_FF), BF16), pltpu.VMEM((D_MODEL, EXPERT_FF), BF16),
                        pltpu.VMEM((EXPERT_FF, D_MODEL), BF16)],
    )
    return pl.pallas_call(
        _experts_kernel,
        grid_spec=grid_spec,
        out_shape=jax.ShapeDtypeStruct((P * PK_TILES, LANES), jnp.uint32),
        compiler_params=pltpu.CompilerParams(dimension_semantics=("arbitrary",), vmem_limit_bytes=VMEM_LIMIT),
        name="experts",
    )(blk_e, nblk, xs, w1, w3, w2)


def _combine_kernel(dest_ref, x1_ref, wts_ref, ys_ref, g_ref, b_ref, out_ref, gbuf0, gbuf1, sem, *, tile):
    T = tile
    gbufs = (gbuf0, gbuf1)

    for t in range(T):
        for kk in range(2):
            row = pl.multiple_of(dest_ref[kk, t] * PK_TILES, PK_TILES)
            pltpu.make_async_copy(ys_ref.at[pl.ds(row, PK_TILES)], gbufs[kk].at[pl.ds(t * PK_TILES, PK_TILES)],
                                  sem).start(priority=kk)
    for kk in range(2):
        pltpu.make_async_copy(ys_ref.at[pl.ds(0, T * PK_TILES)], gbufs[kk], sem).wait()

    wpad = jnp.concatenate([wts_ref[...], jnp.zeros((LANES - SUBLANES, T), F32)], axis=0)
    wt = wpad.T
    rows = lambda buf: _unpack_bf16_pairs(
        jnp.concatenate([buf[pl.ds(c, T, stride=PK_TILES), :] for c in range(PK_TILES)], axis=1))
    y = wt[:, 0:1] * rows(gbuf0) + wt[:, 1:2] * rows(gbuf1)
    out_ref[...] = _layer_norm(DN_ALPHA * x1_ref[...] + y, g_ref[...], b_ref[...])


def _combine(dest, x1f, wts, ys, g, b):
    N = x1f.shape[0]
    T = min(COMBINE_TILE, N)
    assert N % T == 0
    return pl.pallas_call(
        functools.partial(_combine_kernel, tile=T),
        grid=(N // T,),
        in_specs=[pl.BlockSpec((2, T), lambda i: (0, i), memory_space=pltpu.SMEM),
                  pl.BlockSpec((T, D_MODEL), lambda i: (i, 0)),
                  pl.BlockSpec((SUBLANES, T), lambda i: (0, i)),
                  pl.BlockSpec(memory_space=pl.ANY),
                  _const_spec(g.shape), _const_spec(b.shape)],
        out_specs=pl.BlockSpec((T, D_MODEL), lambda i: (i, 0)),
        out_shape=jax.ShapeDtypeStruct((N, D_MODEL), F32),
        scratch_shapes=[pltpu.VMEM((T * PK_TILES, LANES), jnp.uint32), pltpu.VMEM((T * PK_TILES, LANES), jnp.uint32),
                        pltpu.SemaphoreType.DMA(())],
        compiler_params=pltpu.CompilerParams(dimension_semantics=("arbitrary",)),
        name="combine",
    )(dest, x1f, wts, ys, g, b)


def _odd_tiles(w):
    return jnp.pad(w, ((0, 0), (0, LANES))).astype(BF16)


def _pack_mixer_weights(w_in, b_in, rg_w_a, rg_w_x, rg_b_a, rg_b_x, gla_w_a2):
    wmain = _odd_tiles(w_in[:, :C_MAIN_END])
    wbg = _odd_tiles(w_in[:, C_GATES_START:])
    walr = jnp.pad(w_in[:, C_MAIN_END:C_GATES_START], ((0, 0), (0, LANES - GLA_RANK))).astype(BF16)
    bmain = b_in[None, :C_MAIN_END]
    bbg = b_in[None, C_GATES_START:]
    balr = jnp.pad(b_in[None, C_MAIN_END:C_GATES_START], ((0, 0), (0, LANES - GLA_RANK)))
    zero = jnp.zeros((RNN_BLOCK_W, RNN_BLOCK_W), w_in.dtype)
    tiles = []
    for p in range(RNN_BLOCKS // 2):
        top = jnp.concatenate([rg_w_a[2 * p], zero, rg_w_x[2 * p], zero], axis=1)
        bot = jnp.concatenate([zero, rg_w_a[2 * p + 1], zero, rg_w_x[2 * p + 1]], axis=1)
        tiles.append(jnp.concatenate([top, bot], axis=0))
    wgate = jnp.stack(tiles).astype(BF16)
    bgate = jnp.concatenate([rg_b_a, rg_b_x])[None, :]
    wa2 = jnp.concatenate([gla_w_a2, jnp.zeros((LANES - GLA_RANK, GLA_DK), gla_w_a2.dtype)], axis=0).astype(BF16)
    return (wmain, bmain, wbg, bbg, walr, balr), wgate, bgate, wa2


def _layer(x, w_in, b_in, conv_w, conv_b, rg_w_a, rg_b_a, rg_w_x, rg_b_x, rg_lambda, gla_w_a2, gla_b_a,
           gla_norm_g, w_proj_rnn, w_proj_gla, w_o, b_o, ln1_g, ln1_b, router_w_group, router_b_group,
           router_w_expert, router_b_expert, exp_w1, exp_w3, exp_w2, ln2_g, ln2_b):
    B, S, _ = x.shape
    N = B * S
    row = lambda p: p[None, :]

    w_slices, wgate, bgate, wa2 = _pack_mixer_weights(w_in, b_in, rg_w_a, rg_w_x, rg_b_a, rg_b_x, gla_w_a2)
    wr = jnp.concatenate([router_w_group.T, jnp.zeros((SUBLANES - N_GROUPS, D_MODEL), F32), router_w_expert.T],
                         axis=0).astype(BF16)
    br = jnp.concatenate([router_b_group, jnp.zeros((SUBLANES - N_GROUPS,), F32), router_b_expert])[:, None]
    x1, xpk, info, wts, tcnt = _mixer(
        x, *w_slices, conv_w, row(conv_b), wgate, bgate, row(rg_lambda), wa2, row(gla_b_a), row(gla_norm_g),
        _odd_tiles(w_proj_rnn), _odd_tiles(w_proj_gla), _odd_tiles(w_o), row(b_o), row(ln1_g), row(ln1_b), wr, br)
    x1f = x1.reshape(N, D_MODEL)

    tcnt = tcnt[:, :, 0]
    nt = tcnt.shape[0]
    tot = jnp.sum(tcnt, axis=0)
    pcount = (tot + MOE_ROWS - 1) // MOE_ROWS * MOE_ROWS
    pend = jnp.cumsum(pcount)
    base = (pend - pcount)[None, :] + jnp.cumsum(tcnt, axis=0) - tcnt
    base_tok = jnp.repeat(base.T, N // nt, axis=1)
    experts_col = jnp.arange(N_EXPERTS, dtype=jnp.int32)[:, None, None]
    dest = jnp.sum(jnp.where(info[None, 0:2] == experts_col, base_tok[:, None, :], 0), axis=0) + info[2:4]
    nb = -(-(2 * N) // MOE_ROWS) + N_EXPERTS
    P = nb * MOE_ROWS
    nblk = (pend[-1] // MOE_ROWS).astype(jnp.int32)
    blk_start = jnp.minimum(jnp.arange(nb, dtype=jnp.int32), nblk - 1) * MOE_ROWS
    blk_e = jnp.sum((blk_start[:, None] >= pend[None, :]).astype(jnp.int32), axis=1)
    blk_e = jnp.minimum(blk_e, N_EXPERTS - 1)

    last_blk = jnp.where(tot > 0, pend - MOE_ROWS, -1)
    tail_blk = nblk + jnp.arange(N_EXPERTS, dtype=jnp.int32)
    tail_blk = jnp.where(tail_blk < nb, tail_blk * MOE_ROWS, -1)
    xs = _dispatch(jnp.concatenate([last_blk, tail_blk]).astype(jnp.int32), dest, xpk, P)
    ys = _experts(blk_e, nblk[None], xs, exp_w1, exp_w3, exp_w2)
    out = _combine(dest, x1f, wts, ys, row(ln2_g), row(ln2_b))
    return out.reshape(B, S, D_MODEL)


def kernel(x, w_in, b_in, conv_w, conv_b, rg_w_a, rg_b_a, rg_w_x, rg_b_x, rg_lambda, gla_w_a2, gla_b_a, gla_norm_g, w_proj_rnn, w_proj_gla, w_o, b_o, ln1_g, ln1_b, router_w_group, router_b_group, router_w_expert, router_b_expert, exp_w1, exp_w3, exp_w2, ln2_g, ln2_b):
    h = x
    for l in range(w_in.shape[0]):
        h = _layer(h, w_in[l], b_in[l], conv_w[l], conv_b[l], rg_w_a[l], rg_b_a[l], rg_w_x[l], rg_b_x[l],
                   rg_lambda[l], gla_w_a2[l], gla_b_a[l], gla_norm_g[l], w_proj_rnn[l], w_proj_gla[l], w_o[l],
                   b_o[l], ln1_g[l], ln1_b[l], router_w_group[l], router_b_group[l], router_w_expert[l],
                   router_b_expert[l], exp_w1[l], exp_w3[l], exp_w2[l], ln2_g[l], ln2_b[l])
    return h
```

```python
import functools

import jax
import jax.numpy as jnp
from jax import lax
from jax.experimental import pallas as pl
from jax.experimental.pallas import tpu as pltpu

F32 = jnp.float32
BF16 = jnp.bfloat16

D_MODEL = 1024
RNN_WIDTH = 1024
RNN_BLOCKS = 8
RNN_BLOCK_W = RNN_WIDTH // RNN_BLOCKS
CONV_WIDTH = 4
LRU_C = 8.0
GLA_HEADS = 4
GLA_DK = D_MODEL // 2
GLA_DV = D_MODEL
GLA_HEAD_K = GLA_DK // GLA_HEADS
GLA_HEAD_V = GLA_DV // GLA_HEADS
GLA_RANK = 16
GLA_TAU = 16.0
GLA_CHUNK = 64
N_GROUPS = 4
EXPERTS_PER_GROUP = 8
N_EXPERTS = N_GROUPS * EXPERTS_PER_GROUP
EXPERT_FF = 512
DN_ALPHA = 2.0 ** 0.25
LN_EPS = 1e-5
RMS_EPS = 1e-6

LANES = 128
SUBLANES = 8
VMEM_LIMIT = 56 * 1024 * 1024

C_RX, C_RY, C_Q, C_K, C_V, C_G = 0, 1024, 2048, 2560, 3072, 4096
C_GA, C_GB = 0, 1024
C_MAIN_END = 5120
C_GATES_START = C_MAIN_END + GLA_RANK

MIX_TILE = 256
MOE_ROWS = 256
DISPATCH_TILE = 512
COMBINE_TILE = 256
ROUTE_ROWS = 8 + N_EXPERTS
PK_TILES = D_MODEL // 2 // LANES


def _sigmoid(v):
    return 1.0 / (1.0 + jnp.exp(-v))


def _softplus(v):
    return jnp.maximum(v, 0.0) + jnp.log1p(jnp.exp(-jnp.abs(v)))


def _layer_norm(v, g, b):
    mu = jnp.mean(v, axis=-1, keepdims=True)
    c = v - mu
    var = jnp.mean(c * c, axis=-1, keepdims=True)
    return c * lax.rsqrt(var + LN_EPS) * g + b


def _dot(a, b):
    return jnp.dot(a, b, preferred_element_type=F32)


def _dot_nt(a, b):
    return lax.dot_general(a, b, (((1,), (1,)), ((), ())), preferred_element_type=F32)


def _dot_tn(a, b):
    return lax.dot_general(a, b, (((0,), (0,)), ((), ())), preferred_element_type=F32)


def _pack_bf16_pairs(v):
    half = v.shape[1] // 2
    bits = pltpu.bitcast(v.astype(BF16).astype(F32), jnp.uint32)
    return (bits[:, :half] >> 16) | bits[:, half:]


def _unpack_bf16_pairs(u):
    lo = pltpu.bitcast(u << 16, F32)
    hi = pltpu.bitcast(u & jnp.uint32(0xFFFF0000), F32)
    return jnp.concatenate([lo, hi], axis=1)


def _const_spec(shape):
    nd = len(shape)
    return pl.BlockSpec(shape, lambda *_: (0,) * nd, pipeline_mode=pl.Buffered(1))


def _mixer_kernel(x_ref, wmain_ref, bmain_ref, wbg_ref, bbg_ref, walr_ref, balr_ref,
                  convw_ref, convb_ref, wgate_ref, bgate_ref, lam_ref,
                  wa2_ref, ba2_ref, gnorm_ref, wprnn_ref, wpgla_ref, wo_ref, bo_ref, ln1g_ref, ln1b_ref,
                  wr_ref, br_ref,
                  x1_ref, xpk_ref, info_ref, wts_ref, cnt_ref, rxbuf, hcar, st_ref, hbuf, zbuf,
                  *, tile, steps_per_seq):
    T = tile
    step = pl.program_id(0)

    @pl.when(step == 0)
    def _():
        zbuf[...] = jnp.zeros_like(zbuf)

    @pl.when(step % steps_per_seq == 0)
    def _():
        rxbuf[0:SUBLANES, :] = jnp.zeros((SUBLANES, RNN_WIDTH), F32)
        hcar[...] = jnp.zeros_like(hcar)
        st_ref[...] = jnp.zeros_like(st_ref)

    x = x_ref[...]
    xb = x.astype(BF16)

    def proj(c0, c1, w_ref=wmain_ref, b_ref=bmain_ref):
        return _dot(xb, w_ref[:, c0:c1]) + b_ref[:, c0:c1]

    rx = proj(C_RX, C_RX + RNN_WIDTH)
    rxbuf[SUBLANES:SUBLANES + T, :] = rx
    u = convb_ref[...] + convw_ref[CONV_WIDTH - 1:CONV_WIDTH, :] * rx
    for j in range(1, CONV_WIDTH):
        u = u + convw_ref[CONV_WIDTH - 1 - j:CONV_WIDTH - j, :] * rxbuf[SUBLANES - j:SUBLANES - j + T, :]
    rxbuf[0:SUBLANES, :] = rxbuf[T:T + SUBLANES, :]

    qk = proj(C_Q, C_Q + 2 * GLA_DK)
    q = qk[:, :GLA_DK] * (GLA_HEAD_K ** -0.5)
    k = qk[:, GLA_DK:]
    alr = proj(0, LANES, walr_ref, balr_ref)

    x1 = _layer_norm(zbuf[...], ln1g_ref[...], ln1b_ref[...])
    x1_ref[...] = x1
    info, wts, total = _route_tile(x1.astype(BF16), wr_ref[...], br_ref[...])
    info_ref[...] = info
    wts_ref[...] = wts
    cnt_ref[0] = jnp.broadcast_to(total, (N_EXPERTS, LANES))
    pk = _pack_bf16_pairs(x1)
    for c in range(PK_TILES):
        xpk_ref[pl.ds(c, T, stride=PK_TILES), :] = pk[:, c * LANES:(c + 1) * LANES]

    r_parts, i_parts = [], []
    for p in range(RNN_BLOCKS // 2):
        up = u[:, 256 * p:256 * (p + 1)].astype(BF16)
        gp = _dot(up, wgate_ref[p])
        r_parts.append(gp[:, :256])
        i_parts.append(gp[:, 256:])
    r = _sigmoid(jnp.concatenate(r_parts, axis=1) + bgate_ref[:, :RNN_WIDTH])
    ig = _sigmoid(jnp.concatenate(i_parts, axis=1) + bgate_ref[:, RNN_WIDTH:])
    v = proj(C_V, C_V + GLA_DV)
    ry = proj(C_RY, C_RY + RNN_WIDTH)

    z = _dot(alr.astype(BF16), wa2_ref[...]) + ba2_ref[...]
    la = -_softplus(-z) * (1.0 / GLA_TAU)
    ri = lax.broadcasted_iota(jnp.int32, (T, T), 0)
    ci = lax.broadcasted_iota(jnp.int32, (T, T), 1)
    chunk_start = (ri >> 6) << 6
    tri = jnp.where((ci <= ri) & (ci >= chunk_start), 1.0, 0.0).astype(BF16)
    la_hi = la.astype(BF16)
    la_lo = (la - la_hi.astype(F32)).astype(BF16)
    bcum = _dot(tri, la_hi) + _dot(tri, la_lo)

    g = proj(C_G, C_G + GLA_DV)

    log_a = (-LRU_C) * r * _softplus(-lam_ref[...])
    a = jnp.exp(log_a)
    m2 = -jnp.tanh(log_a) * (1.0 + a * a)
    bv = jnp.where(m2 > 0.0, m2 * lax.rsqrt(m2), 0.0) * (ig * u)

    cr = lax.broadcasted_iota(jnp.int32, (GLA_CHUNK, GLA_CHUNK), 0)
    cc = lax.broadcasted_iota(jnp.int32, (GLA_CHUNK, GLA_CHUNK), 1)
    causal = cr >= cc
    n_chunks = T // GLA_CHUNK
    heads = [(slice(hd * GLA_HEAD_K, (hd + 1) * GLA_HEAD_K), slice(hd * GLA_HEAD_V, (hd + 1) * GLA_HEAD_V))
             for hd in range(GLA_HEADS)]
    qd_c, ki_c, ke_c, vv_c, dec_c = [], [], [], [], []
    for c in range(n_chunks):
        r0 = c * GLA_CHUNK
        bc = bcum[r0:r0 + GLA_CHUNK, :]
        bl = bcum[r0 + GLA_CHUNK - 1:r0 + GLA_CHUNK, :]
        kc = k[r0:r0 + GLA_CHUNK, :]
        qd_c.append((q[r0:r0 + GLA_CHUNK, :] * jnp.exp(bc)).astype(BF16))
        ki_c.append((kc * jnp.exp(-bc)).astype(BF16))
        ke_c.append((kc * jnp.exp(bl - bc)).astype(BF16))
        vv_c.append(v[r0:r0 + GLA_CHUNK, :].astype(BF16))
        dec_c.append(jnp.exp(bl))
    scores = [[jnp.where(causal, _dot_nt(qd_c[c][:, ks], ki_c[c][:, ks]), 0.0).astype(BF16) for ks, _ in heads]
              for c in range(n_chunks)]
    intra = [[_dot(scores[c][hd], vv_c[c][:, vs]) for hd, (_, vs) in enumerate(heads)] for c in range(n_chunks)]
    incr = [[_dot_tn(vv_c[c][:, vs], ke_c[c][:, ks]) for ks, vs in heads] for c in range(n_chunks)]

    n_groups = T // SUBLANES
    sub = lax.broadcasted_iota(jnp.int32, (n_groups, SUBLANES, RNN_WIDTH), 1)
    sa = a.reshape(n_groups, SUBLANES, RNN_WIDTH)
    sb = bv.reshape(n_groups, SUBLANES, RNN_WIDTH)
    for s in (1, 2, 4):
        keep = sub >= s
        sb = sa * jnp.where(keep, pltpu.roll(sb, s, 1), 0.0) + sb
        sa = sa * jnp.where(keep, pltpu.roll(sa, s, 1), 1.0)
    carry = hcar[0:1, :]
    for gi in range(n_groups):
        hg = sb[gi] + sa[gi] * carry
        hbuf[gi * SUBLANES:(gi + 1) * SUBLANES, :] = hg
        carry = hg[SUBLANES - 1:SUBLANES, :]
    hcar[0:1, :] = carry
    h = hbuf[...]
    ga = proj(C_GA, C_GA + D_MODEL, wbg_ref, bbg_ref)

    out_a = _dot((h * jax.nn.gelu(ry)).astype(BF16), wprnn_ref[:, :D_MODEL])
    gb = proj(C_GB, C_GB + D_MODEL, wbg_ref, bbg_ref)

    states = [st_ref[hd] for hd in range(GLA_HEADS)]
    o_chunks = []
    for c in range(n_chunks):
        o_heads = []
        for hd, (ks, _) in enumerate(heads):
            o_heads.append(intra[c][hd] + _dot_nt(qd_c[c][:, ks], states[hd].astype(BF16)))
            states[hd] = states[hd] * dec_c[c][:, ks] + incr[c][hd]
        o_chunks.append(jnp.concatenate(o_heads, axis=1))
    for hd in range(GLA_HEADS):
        st_ref[hd] = states[hd]
    o_all = jnp.concatenate(o_chunks, axis=0)

    o_parts = []
    for hd in range(GLA_HEADS):
        vs = slice(hd * GLA_HEAD_V, (hd + 1) * GLA_HEAD_V)
        oh = o_all[:, vs]
        ms = jnp.mean(oh * oh, axis=-1, keepdims=True)
        o_parts.append(oh * lax.rsqrt(ms + RMS_EPS) * gnorm_ref[:, vs])
    on = jnp.concatenate(o_parts, axis=1) * (g * _sigmoid(g))
    out_b = _dot(on.astype(BF16), wpgla_ref[:, :D_MODEL])

    merged = _sigmoid(ga) * out_a + _sigmoid(gb) * out_b
    y = _dot(merged.astype(BF16), wo_ref[:, :D_MODEL]) + bo_ref[...]
    zbuf[...] = DN_ALPHA * x + y


def _mixer(x, wmain, bmain, wbg, bbg, walr, balr, conv_w, conv_b, wgate, bgate, lam, wa2, ba2, gnorm,
           wprnn, wpgla, wo, bo, ln1g, ln1b, wr, br):
    B, S, _ = x.shape
    T = min(MIX_TILE, S)
    assert S % T == 0 and T % GLA_CHUNK == 0
    N = B * S
    nt = N // T
    mix_tile = lambda i: jnp.minimum(i, nt - 1)
    tail_tile = lambda i: jnp.maximum(i - 1, 0)
    weights = (wmain, bmain, wbg, bbg, walr, balr, conv_w, conv_b, wgate, bgate, lam, wa2, ba2, gnorm,
               wprnn, wpgla, wo, bo, ln1g, ln1b, wr, br)
    return pl.pallas_call(
        functools.partial(_mixer_kernel, tile=T, steps_per_seq=S // T),
        grid=(nt + 1,),
        in_specs=[pl.BlockSpec((T, D_MODEL), lambda i: (mix_tile(i), 0))] + [_const_spec(w.shape) for w in weights],
        out_specs=[pl.BlockSpec((T, D_MODEL), lambda i: (tail_tile(i), 0)),
                   pl.BlockSpec((T * PK_TILES, LANES), lambda i: (tail_tile(i), 0)),
                   pl.BlockSpec((SUBLANES, T), lambda i: (0, tail_tile(i))),
                   pl.BlockSpec((SUBLANES, T), lambda i: (0, tail_tile(i))),
                   pl.BlockSpec((1, N_EXPERTS, LANES), lambda i: (tail_tile(i), 0, 0))],
        out_shape=[jax.ShapeDtypeStruct((N, D_MODEL), F32),
                   jax.ShapeDtypeStruct((N * PK_TILES, LANES), jnp.uint32),
                   jax.ShapeDtypeStruct((SUBLANES, N), jnp.int32),
                   jax.ShapeDtypeStruct((SUBLANES, N), F32),
                   jax.ShapeDtypeStruct((nt, N_EXPERTS, LANES), jnp.int32)],
        scratch_shapes=[pltpu.VMEM((T + SUBLANES, RNN_WIDTH), F32),
                        pltpu.VMEM((SUBLANES, RNN_WIDTH), F32),
                        pltpu.VMEM((GLA_HEADS, GLA_HEAD_V, GLA_HEAD_K), F32),
                        pltpu.VMEM((T, RNN_WIDTH), F32),
                        pltpu.VMEM((T, D_MODEL), F32)],
        compiler_params=pltpu.CompilerParams(dimension_semantics=("arbitrary",), vmem_limit_bytes=VMEM_LIMIT),
        name="mixer",
    )(x.reshape(N, D_MODEL), *weights)


def _route_tile(x1b, wr, br):
    T = x1b.shape[0]
    logits = _dot_nt(wr, x1b) + br
    row8 = lax.broadcasted_iota(jnp.int32, (SUBLANES, T), 0)
    row8f = row8.astype(F32)
    neg = jnp.float32(-jnp.inf)
    first = lambda hit: jnp.min(jnp.where(hit, row8f, float(SUBLANES)), axis=0, keepdims=True)

    gl = jnp.where(row8 < N_GROUPS, logits[0:SUBLANES, :], neg)
    gmax = jnp.max(gl, axis=0, keepdims=True)
    grp = first(gl == gmax)
    p_grp = 1.0 / jnp.sum(jnp.exp(gl - gmax), axis=0, keepdims=True)

    e_sel = jnp.zeros((EXPERTS_PER_GROUP, T), F32)
    for gi in range(N_GROUPS):
        lo = SUBLANES + gi * EXPERTS_PER_GROUP
        e_sel = jnp.where(grp == float(gi), logits[lo:lo + EXPERTS_PER_GROUP, :], e_sel)
    m1 = jnp.max(e_sel, axis=0, keepdims=True)
    i1 = first(e_sel == m1)
    e_rest = jnp.where(row8f == i1, neg, e_sel)
    m2 = jnp.max(e_rest, axis=0, keepdims=True)
    i2 = first(e_rest == m2)
    e21 = jnp.exp(m2 - m1)
    p1 = 1.0 / (1.0 + e21)
    w0 = p_grp * p1
    w1 = p_grp * (e21 * p1)
    eid0 = (grp * EXPERTS_PER_GROUP + i1).astype(jnp.int32)
    eid1 = (grp * EXPERTS_PER_GROUP + i2).astype(jnp.int32)

    erow = lax.broadcasted_iota(jnp.int32, (N_EXPERTS, T), 0)
    oh0 = jnp.where(erow == eid0, 1.0, 0.0)
    oh1 = jnp.where(erow == eid1, 1.0, 0.0)
    both = oh0 + oh1
    ti = lax.broadcasted_iota(jnp.int32, (T, T), 0)
    tj = lax.broadcasted_iota(jnp.int32, (T, T), 1)
    before = jnp.where(ti < tj, 1.0, 0.0).astype(BF16)
    prior = _dot(both.astype(BF16), before)
    rank0 = jnp.sum(prior * oh0, axis=0, keepdims=True).astype(jnp.int32)
    rank1 = jnp.sum(prior * oh1, axis=0, keepdims=True).astype(jnp.int32)
    total = (prior[:, T - 1:T] + both[:, T - 1:T]).astype(jnp.int32)

    info = jnp.where(row8 == 0, eid0, jnp.where(row8 == 1, eid1, jnp.where(row8 == 2, rank0,
                     jnp.where(row8 == 3, rank1, 0))))
    return info, jnp.where(row8 == 0, w0, jnp.where(row8 == 1, w1, 0.0)), total


def _dispatch_kernel(zero_blk_ref, dest_ref, xpk_ref, xs_ref, zbuf, sem, zsem, *, tile):
    T = tile

    @pl.when(pl.program_id(0) == 0)
    def _():
        zbuf[...] = jnp.zeros_like(zbuf)

        def zero_copy(j):
            row = pl.multiple_of(jnp.maximum(zero_blk_ref[j], 0) * PK_TILES, MOE_ROWS * PK_TILES)
            return pltpu.make_async_copy(zbuf, xs_ref.at[pl.ds(row, MOE_ROWS * PK_TILES)], zsem)

        for j in range(2 * N_EXPERTS):
            @pl.when(zero_blk_ref[j] >= 0)
            def _():
                zero_copy(j).start()
        for j in range(2 * N_EXPERTS):
            @pl.when(zero_blk_ref[j] >= 0)
            def _():
                zero_copy(j).wait()

    for t in range(T):
        for kk in range(2):
            row = pl.multiple_of(dest_ref[kk, t] * PK_TILES, PK_TILES)
            pltpu.make_async_copy(xpk_ref.at[pl.ds(t * PK_TILES, PK_TILES)], xs_ref.at[pl.ds(row, PK_TILES)],
                                  sem).start(priority=kk)
    for kk in range(2):
        pltpu.make_async_copy(xpk_ref, xs_ref.at[pl.ds(0, T * PK_TILES)], sem).wait()


def _dispatch(last_blk, dest, xpk, n_rows):
    N = xpk.shape[0] // PK_TILES
    T = min(DISPATCH_TILE, N)
    assert N % T == 0
    grid_spec = pltpu.PrefetchScalarGridSpec(
        num_scalar_prefetch=1,
        grid=(N // T,),
        in_specs=[pl.BlockSpec((2, T), lambda i, lb: (0, i), memory_space=pltpu.SMEM),
                  pl.BlockSpec((T * PK_TILES, LANES), lambda i, lb: (i, 0))],
        out_specs=pl.BlockSpec(memory_space=pl.ANY),
        scratch_shapes=[pltpu.VMEM((MOE_ROWS * PK_TILES, LANES), jnp.uint32),
                        pltpu.SemaphoreType.DMA(()), pltpu.SemaphoreType.DMA(())],
    )
    return pl.pallas_call(
        functools.partial(_dispatch_kernel, tile=T),
        grid_spec=grid_spec,
        out_shape=jax.ShapeDtypeStruct((n_rows * PK_TILES, LANES), jnp.uint32),
        compiler_params=pltpu.CompilerParams(dimension_semantics=("arbitrary",), has_side_effects=True),
        name="dispatch",
    )(last_blk, dest, xpk)


def _experts_kernel(blk_e_ref, nblk_ref, xs_ref, w1_ref, w3_ref, w2_ref, ys_ref, w1b, w3b, w2b):
    i = pl.program_id(0)
    R = MOE_ROWS

    @pl.when((i == 0) | (blk_e_ref[i] != blk_e_ref[jnp.maximum(i - 1, 0)]))
    def _():
        w1b[...] = w1_ref[0].astype(BF16)
        w3b[...] = w3_ref[0].astype(BF16)
        w2b[...] = w2_ref[0].astype(BF16)

    @pl.when(i < nblk_ref[0])
    def _():
        u = jnp.concatenate([xs_ref[pl.ds(c, R, stride=PK_TILES), :] for c in range(PK_TILES)], axis=1)
        xb = _unpack_bf16_pairs(u).astype(BF16)
        h1 = _dot(xb, w1b[...])
        h3 = _dot(xb, w3b[...])
        hact = (h1 * _sigmoid(h1)) * h3
        y = _dot(hact.astype(BF16), w2b[...])
        pk = _pack_bf16_pairs(y)
        for c in range(PK_TILES):
            ys_ref[pl.ds(c, R, stride=PK_TILES), :] = pk[:, c * LANES:(c + 1) * LANES]

    @pl.when(i >= nblk_ref[0])
    def _():
        ys_ref[...] = jnp.zeros_like(ys_ref)


def _experts(blk_e, nblk, xs, w1, w3, w2):
    P = xs.shape[0] // PK_TILES
    nb = P // MOE_ROWS
    grid_spec = pltpu.PrefetchScalarGridSpec(
        num_scalar_prefetch=2,
        grid=(nb,),
        in_specs=[pl.BlockSpec((MOE_ROWS * PK_TILES, LANES), lambda i, be, n: (jnp.minimum(i, n[0] - 1), 0)),
                  pl.BlockSpec((1, D_MODEL, EXPERT_FF), lambda i, be, n: (be[i], 0, 0)),
                  pl.BlockSpec((1, D_MODEL, EXPERT_FF), lambda i, be, n: (be[i], 0, 0)),
                  pl.BlockSpec((1, EXPERT_FF, D_MODEL), lambda i, be, n: (be[i], 0, 0))],
        out_specs=pl.BlockSpec((MOE_ROWS * PK_TILES, LANES), lambda i, be, n: (i, 0)),
        scratch_shapes=[pltpu.VMEM((D_MODEL, EXPERT_FF), BF16), pltpu.VMEM((D_MODEL, EXPERT_FF), BF16),
                        pltpu.VMEM((EXPERT_FF, D_MODEL), BF16)],
    )
    return pl.pallas_call(
        _experts_kernel,
        grid_spec=grid_spec,
        out_shape=jax.ShapeDtypeStruct((P * PK_TILES, LANES), jnp.uint32),
        compiler_params=pltpu.CompilerParams(dimension_semantics=("arbitrary",), vmem_limit_bytes=VMEM_LIMIT),
        name="experts",
    )(blk_e, nblk, xs, w1, w3, w2)


def _combine_kernel(dest_ref, x1_ref, wts_ref, ys_ref, g_ref, b_ref, out_ref, gbuf0, gbuf1, sem, *, tile):
    T = tile
    gbufs = (gbuf0, gbuf1)

    for t in range(T):
        for kk in range(2):
            row = pl.multiple_of(dest_ref[kk, t] * PK_TILES, PK_TILES)
            pltpu.make_async_copy(ys_ref.at[pl.ds(row, PK_TILES)], gbufs[kk].at[pl.ds(t * PK_TILES, PK_TILES)],
                                  sem).start(priority=kk)
    for kk in range(2):
        pltpu.make_async_copy(ys_ref.at[pl.ds(0, T * PK_TILES)], gbufs[kk], sem).wait()

    wpad = jnp.concatenate([wts_ref[...], jnp.zeros((LANES - SUBLANES, T), F32)], axis=0)
    wt = wpad.T
    rows = lambda buf: _unpack_bf16_pairs(
        jnp.concatenate([buf[pl.ds(c, T, stride=PK_TILES), :] for c in range(PK_TILES)], axis=1))
    y = wt[:, 0:1] * rows(gbuf0) + wt[:, 1:2] * rows(gbuf1)
    out_ref[...] = _layer_norm(DN_ALPHA * x1_ref[...] + y, g_ref[...], b_ref[...])


def _combine(dest, x1f, wts, ys, g, b):
    N = x1f.shape[0]
    T = min(COMBINE_TILE, N)
    assert N % T == 0
    return pl.pallas_call(
        functools.partial(_combine_kernel, tile=T),
        grid=(N // T,),
        in_specs=[pl.BlockSpec((2, T), lambda i: (0, i), memory_space=pltpu.SMEM),
                  pl.BlockSpec((T, D_MODEL), lambda i: (i, 0)),
                  pl.BlockSpec((SUBLANES, T), lambda i: (0, i)),
                  pl.BlockSpec(memory_space=pl.ANY),
                  _const_spec(g.shape), _const_spec(b.shape)],
        out_specs=pl.BlockSpec((T, D_MODEL), lambda i: (i, 0)),
        out_shape=jax.ShapeDtypeStruct((N, D_MODEL), F32),
        scratch_shapes=[pltpu.VMEM((T * PK_TILES, LANES), jnp.uint32), pltpu.VMEM((T * PK_TILES, LANES), jnp.uint32),
                        pltpu.SemaphoreType.DMA(())],
        compiler_params=pltpu.CompilerParams(dimension_semantics=("arbitrary",)),
        name="combine",
    )(dest, x1f, wts, ys, g, b)


def _odd_tiles(w):
    return jnp.pad(w, ((0, 0), (0, LANES))).astype(BF16)


def _pack_mixer_weights(w_in, b_in, rg_w_a, rg_w_x, rg_b_a, rg_b_x, gla_w_a2):
    wmain = _odd_tiles(w_in[:, :C_MAIN_END])
    wbg = _odd_tiles(w_in[:, C_GATES_START:])
    walr = jnp.pad(w_in[:, C_MAIN_END:C_GATES_START], ((0, 0), (0, LANES - GLA_RANK))).astype(BF16)
    bmain = b_in[None, :C_MAIN_END]
    bbg = b_in[None, C_GATES_START:]
    balr = jnp.pad(b_in[None, C_MAIN_END:C_GATES_START], ((0, 0), (0, LANES - GLA_RANK)))
    zero = jnp.zeros((RNN_BLOCK_W, RNN_BLOCK_W), w_in.dtype)
    tiles = []
    for p in range(RNN_BLOCKS // 2):
        top = jnp.concatenate([rg_w_a[2 * p], zero, rg_w_x[2 * p], zero], axis=1)
        bot = jnp.concatenate([zero, rg_w_a[2 * p + 1], zero, rg_w_x[2 * p + 1]], axis=1)
        tiles.append(jnp.concatenate([top, bot], axis=0))
    wgate = jnp.stack(tiles).astype(BF16)
    bgate = jnp.concatenate([rg_b_a, rg_b_x])[None, :]
    wa2 = jnp.concatenate([gla_w_a2, jnp.zeros((LANES - GLA_RANK, GLA_DK), gla_w_a2.dtype)], axis=0).astype(BF16)
    return (wmain, bmain, wbg, bbg, walr, balr), wgate, bgate, wa2


def _layer(x, w_in, b_in, conv_w, conv_b, rg_w_a, rg_b_a, rg_w_x, rg_b_x, rg_lambda, gla_w_a2, gla_b_a,
           gla_norm_g, w_proj_rnn, w_proj_gla, w_o, b_o, ln1_g, ln1_b, router_w_group, router_b_group,
           router_w_expert, router_b_expert, exp_w1, exp_w3, exp_w2, ln2_g, ln2_b):
    B, S, _ = x.shape
    N = B * S
    row = lambda p: p[None, :]

    w_slices, wgate, bgate, wa2 = _pack_mixer_weights(w_in, b_in, rg_w_a, rg_w_x, rg_b_a, rg_b_x, gla_w_a2)
    wr = jnp.concatenate([router_w_group.T, jnp.zeros((SUBLANES - N_GROUPS, D_MODEL), F32), router_w_expert.T],
                         axis=0).astype(BF16)
    br = jnp.concatenate([router_b_group, jnp.zeros((SUBLANES - N_GROUPS,), F32), router_b_expert])[:, None]
    x1f, xpk, info, wts, tcnt = _mixer(
        x, *w_slices, conv_w, row(conv_b), wgate, bgate, row(rg_lambda), wa2, row(gla_b_a), row(gla_norm_g),
        _odd_tiles(w_proj_rnn), _odd_tiles(w_proj_gla), _odd_tiles(w_o), row(b_o), row(ln1_g), row(ln1_b), wr, br)

    tcnt = tcnt[:, :, 0]
    nt = tcnt.shape[0]
    tot = jnp.sum(tcnt, axis=0)
    pcount = (tot + MOE_ROWS - 1) // MOE_ROWS * MOE_ROWS
    pend = jnp.cumsum(pcount)
    base = (pend - pcount)[None, :] + jnp.cumsum(tcnt, axis=0) - tcnt
    base_tok = jnp.repeat(base.T, N // nt, axis=1)
    experts_col = jnp.arange(N_EXPERTS, dtype=jnp.int32)[:, None, None]
    dest = jnp.sum(jnp.where(info[None, 0:2] == experts_col, base_tok[:, None, :], 0), axis=0) + info[2:4]
    nb = -(-(2 * N) // MOE_ROWS) + N_EXPERTS
    P = nb * MOE_ROWS
    nblk = (pend[-1] // MOE_ROWS).astype(jnp.int32)
    blk_start = jnp.minimum(jnp.arange(nb, dtype=jnp.int32), nblk - 1) * MOE_ROWS
    blk_e = jnp.sum((blk_start[:, None] >= pend[None, :]).astype(jnp.int32), axis=1)
    blk_e = jnp.minimum(blk_e, N_EXPERTS - 1)

    last_blk = jnp.where(tot > 0, pend - MOE_ROWS, -1)
    tail_blk = nblk + jnp.arange(N_EXPERTS, dtype=jnp.int32)
    tail_blk = jnp.where(tail_blk < nb, tail_blk * MOE_ROWS, -1)
    xs = _dispatch(jnp.concatenate([last_blk, tail_blk]).astype(jnp.int32), dest, xpk, P)
    ys = _experts(blk_e, nblk[None], xs, exp_w1, exp_w3, exp_w2)
    out = _combine(dest, x1f, wts, ys, row(ln2_g), row(ln2_b))
    return out.reshape(B, S, D_MODEL)


def kernel(x, w_in, b_in, conv_w, conv_b, rg_w_a, rg_b_a, rg_w_x, rg_b_x, rg_lambda, gla_w_a2, gla_b_a, gla_norm_g, w_proj_rnn, w_proj_gla, w_o, b_o, ln1_g, ln1_b, router_w_group, router_b_group, router_w_expert, router_b_expert, exp_w1, exp_w3, exp_w2, ln2_g, ln2_b):
    h = x
    for l in range(w_in.shape[0]):
        h = _layer(h, w_in[l], b_in[l], conv_w[l], conv_b[l], rg_w_a[l], rg_b_a[l], rg_w_x[l], rg_b_x[l],
                   rg_lambda[l], gla_w_a2[l], gla_b_a[l], gla_norm_g[l], w_proj_rnn[l], w_proj_gla[l], w_o[l],
                   b_o[l], ln1_g[l], ln1_b[l], router_w_group[l], router_b_group[l], router_w_expert[l],
                   router_b_expert[l], exp_w1[l], exp_w3[l], exp_w2[l], ln2_g[l], ln2_b[l])
    return h
```

```python
import functools

import jax
import jax.numpy as jnp
from jax import lax
from jax.experimental import pallas as pl
from jax.experimental.pallas import tpu as pltpu

F32 = jnp.float32
BF16 = jnp.bfloat16

D_MODEL = 1024
RNN_WIDTH = 1024
RNN_BLOCKS = 8
RNN_BLOCK_W = RNN_WIDTH // RNN_BLOCKS
CONV_WIDTH = 4
LRU_C = 8.0
GLA_HEADS = 4
GLA_DK = D_MODEL // 2
GLA_DV = D_MODEL
GLA_HEAD_K = GLA_DK // GLA_HEADS
GLA_HEAD_V = GLA_DV // GLA_HEADS
GLA_RANK = 16
GLA_TAU = 16.0
GLA_CHUNK = 64
N_GROUPS = 4
EXPERTS_PER_GROUP = 8
N_EXPERTS = N_GROUPS * EXPERTS_PER_GROUP
EXPERT_FF = 512
DN_ALPHA = 2.0 ** 0.25
LN_EPS = 1e-5
RMS_EPS = 1e-6

LANES = 128
SUBLANES = 8
VMEM_LIMIT = 56 * 1024 * 1024

C_RX, C_RY, C_Q, C_K, C_V, C_G = 0, 1024, 2048, 2560, 3072, 4096
C_GA, C_GB = 0, 1024
C_MAIN_END = 5120
C_GATES_START = C_MAIN_END + GLA_RANK

MIX_TILE = 256
MOE_ROWS = 256
DISPATCH_TILE = 512
COMBINE_TILE = 256
ROUTE_ROWS = 8 + N_EXPERTS
PK_TILES = D_MODEL // 2 // LANES
TOK_ROWS = SUBLANES
ROW_EXPERT, ROW_TOKEN, ROW_VALID = PK_TILES, PK_TILES + 1, PK_TILES + 2


def _sigmoid(v):
    return 1.0 / (1.0 + jnp.exp(-v))


def _softplus(v):
    return jnp.maximum(v, 0.0) + jnp.log1p(jnp.exp(-jnp.abs(v)))


def _layer_norm(v, g, b):
    mu = jnp.mean(v, axis=-1, keepdims=True)
    c = v - mu
    var = jnp.mean(c * c, axis=-1, keepdims=True)
    return c * lax.rsqrt(var + LN_EPS) * g + b


def _dot(a, b):
    return jnp.dot(a, b, preferred_element_type=F32)


def _dot_nt(a, b):
    return lax.dot_general(a, b, (((1,), (1,)), ((), ())), preferred_element_type=F32)


def _dot_tn(a, b):
    return lax.dot_general(a, b, (((0,), (0,)), ((), ())), preferred_element_type=F32)


def _pack_bf16_pairs(v):
    half = v.shape[1] // 2
    bits = pltpu.bitcast(v.astype(BF16).astype(F32), jnp.uint32)
    return (bits[:, :half] >> 16) | bits[:, half:]


def _unpack_bf16_pairs(u):
    lo = pltpu.bitcast(u << 16, F32)
    hi = pltpu.bitcast(u & jnp.uint32(0xFFFF0000), F32)
    return jnp.concatenate([lo, hi], axis=1)


def _const_spec(shape):
    nd = len(shape)
    return pl.BlockSpec(shape, lambda *_: (0,) * nd, pipeline_mode=pl.Buffered(1))


def _mixer_kernel(x_ref, wmain_ref, bmain_ref, wbg_ref, bbg_ref, walr_ref, balr_ref,
                  convw_ref, convb_ref, wgate_ref, bgate_ref, lam_ref,
                  wa2_ref, ba2_ref, gnorm_ref, wprnn_ref, wpgla_ref, wo_ref, bo_ref, ln1g_ref, ln1b_ref,
                  wr_ref, br_ref,
                  x1_ref, xpk_ref, info_ref, wts_ref, cnt_ref, rxbuf, hcar, st_ref, hbuf, zbuf,
                  *, tile, steps_per_seq):
    T = tile
    step = pl.program_id(0)

    @pl.when(step == 0)
    def _():
        zbuf[...] = jnp.zeros_like(zbuf)

    @pl.when(step % steps_per_seq == 0)
    def _():
        rxbuf[0:SUBLANES, :] = jnp.zeros((SUBLANES, RNN_WIDTH), F32)
        hcar[...] = jnp.zeros_like(hcar)
        st_ref[...] = jnp.zeros_like(st_ref)

    x = x_ref[...]
    xb = x.astype(BF16)

    def proj(c0, c1, w_ref=wmain_ref, b_ref=bmain_ref):
        return _dot(xb, w_ref[:, c0:c1]) + b_ref[:, c0:c1]

    rx = proj(C_RX, C_RX + RNN_WIDTH)
    rxbuf[SUBLANES:SUBLANES + T, :] = rx
    u = convb_ref[...] + convw_ref[CONV_WIDTH - 1:CONV_WIDTH, :] * rx
    for j in range(1, CONV_WIDTH):
        u = u + convw_ref[CONV_WIDTH - 1 - j:CONV_WIDTH - j, :] * rxbuf[SUBLANES - j:SUBLANES - j + T, :]
    rxbuf[0:SUBLANES, :] = rxbuf[T:T + SUBLANES, :]

    qk = proj(C_Q, C_Q + 2 * GLA_DK)
    q = qk[:, :GLA_DK] * (GLA_HEAD_K ** -0.5)
    k = qk[:, GLA_DK:]
    alr = proj(0, LANES, walr_ref, balr_ref)

    x1 = _layer_norm(zbuf[...], ln1g_ref[...], ln1b_ref[...])
    x1_ref[...] = x1
    info, wts, total = _route_tile(x1.astype(BF16), wr_ref[...], br_ref[...])
    info_ref[...] = info
    wts_ref[...] = wts
    cnt_ref[0] = jnp.broadcast_to(total, (N_EXPERTS, LANES))
    pk = _pack_bf16_pairs(x1)
    for c in range(PK_TILES):
        xpk_ref[pl.ds(c, T, stride=TOK_ROWS), :] = pk[:, c * LANES:(c + 1) * LANES]
    first_expert = jnp.broadcast_to(info[0:1, :].astype(F32), (LANES, T)).T.astype(jnp.int32)
    token = lax.broadcasted_iota(jnp.int32, (T, LANES), 0) + jnp.maximum(step - 1, 0) * T
    meta = {ROW_EXPERT: first_expert, ROW_TOKEN: token, ROW_VALID: jnp.ones((T, LANES), jnp.int32),
            TOK_ROWS - 1: jnp.zeros((T, LANES), jnp.int32)}
    for row, val in meta.items():
        xpk_ref[pl.ds(row, T, stride=TOK_ROWS), :] = pltpu.bitcast(val, jnp.uint32)

    r_parts, i_parts = [], []
    for p in range(RNN_BLOCKS // 2):
        up = u[:, 256 * p:256 * (p + 1)].astype(BF16)
        gp = _dot(up, wgate_ref[p])
        r_parts.append(gp[:, :256])
        i_parts.append(gp[:, 256:])
    r = _sigmoid(jnp.concatenate(r_parts, axis=1) + bgate_ref[:, :RNN_WIDTH])
    ig = _sigmoid(jnp.concatenate(i_parts, axis=1) + bgate_ref[:, RNN_WIDTH:])
    v = proj(C_V, C_V + GLA_DV)
    ry = proj(C_RY, C_RY + RNN_WIDTH)

    z = _dot(alr.astype(BF16), wa2_ref[...]) + ba2_ref[...]
    la = -_softplus(-z) * (1.0 / GLA_TAU)
    ri = lax.broadcasted_iota(jnp.int32, (T, T), 0)
    ci = lax.broadcasted_iota(jnp.int32, (T, T), 1)
    chunk_start = (ri >> 6) << 6
    tri = jnp.where((ci <= ri) & (ci >= chunk_start), 1.0, 0.0).astype(BF16)
    la_hi = la.astype(BF16)
    la_lo = (la - la_hi.astype(F32)).astype(BF16)
    bcum = _dot(tri, la_hi) + _dot(tri, la_lo)

    g = proj(C_G, C_G + GLA_DV)

    log_a = (-LRU_C) * r * _softplus(-lam_ref[...])
    a = jnp.exp(log_a)
    m2 = -jnp.tanh(log_a) * (1.0 + a * a)
    bv = jnp.where(m2 > 0.0, m2 * lax.rsqrt(m2), 0.0) * (ig * u)

    cr = lax.broadcasted_iota(jnp.int32, (GLA_CHUNK, GLA_CHUNK), 0)
    cc = lax.broadcasted_iota(jnp.int32, (GLA_CHUNK, GLA_CHUNK), 1)
    causal = cr >= cc
    n_chunks = T // GLA_CHUNK
    heads = [(slice(hd * GLA_HEAD_K, (hd + 1) * GLA_HEAD_K), slice(hd * GLA_HEAD_V, (hd + 1) * GLA_HEAD_V))
             for hd in range(GLA_HEADS)]
    qd_c, ki_c, ke_c, vv_c, dec_c = [], [], [], [], []
    for c in range(n_chunks):
        r0 = c * GLA_CHUNK
        bc = bcum[r0:r0 + GLA_CHUNK, :]
        bl = bcum[r0 + GLA_CHUNK - 1:r0 + GLA_CHUNK, :]
        kc = k[r0:r0 + GLA_CHUNK, :]
        qd_c.append((q[r0:r0 + GLA_CHUNK, :] * jnp.exp(bc)).astype(BF16))
        ki_c.append((kc * jnp.exp(-bc)).astype(BF16))
        ke_c.append((kc * jnp.exp(bl - bc)).astype(BF16))
        vv_c.append(v[r0:r0 + GLA_CHUNK, :].astype(BF16))
        dec_c.append(jnp.exp(bl))
    scores = [[jnp.where(causal, _dot_nt(qd_c[c][:, ks], ki_c[c][:, ks]), 0.0).astype(BF16) for ks, _ in heads]
              for c in range(n_chunks)]
    intra = [[_dot(scores[c][hd], vv_c[c][:, vs]) for hd, (_, vs) in enumerate(heads)] for c in range(n_chunks)]
    incr = [[_dot_tn(vv_c[c][:, vs], ke_c[c][:, ks]) for ks, vs in heads] for c in range(n_chunks)]

    n_groups = T // SUBLANES
    sub = lax.broadcasted_iota(jnp.int32, (n_groups, SUBLANES, RNN_WIDTH), 1)
    sa = a.reshape(n_groups, SUBLANES, RNN_WIDTH)
    sb = bv.reshape(n_groups, SUBLANES, RNN_WIDTH)
    for s in (1, 2, 4):
        keep = sub >= s
        sb = sa * jnp.where(keep, pltpu.roll(sb, s, 1), 0.0) + sb
        sa = sa * jnp.where(keep, pltpu.roll(sa, s, 1), 1.0)
    carry = hcar[0:1, :]
    for gi in range(n_groups):
        hg = sb[gi] + sa[gi] * carry
        hbuf[gi * SUBLANES:(gi + 1) * SUBLANES, :] = hg
        carry = hg[SUBLANES - 1:SUBLANES, :]
    hcar[0:1, :] = carry
    h = hbuf[...]
    ga = proj(C_GA, C_GA + D_MODEL, wbg_ref, bbg_ref)

    out_a = _dot((h * jax.nn.gelu(ry)).astype(BF16), wprnn_ref[:, :D_MODEL])
    gb = proj(C_GB, C_GB + D_MODEL, wbg_ref, bbg_ref)

    states = [st_ref[hd] for hd in range(GLA_HEADS)]
    o_chunks = []
    for c in range(n_chunks):
        o_heads = []
        for hd, (ks, _) in enumerate(heads):
            o_heads.append(intra[c][hd] + _dot_nt(qd_c[c][:, ks], states[hd].astype(BF16)))
            states[hd] = states[hd] * dec_c[c][:, ks] + incr[c][hd]
        o_chunks.append(jnp.concatenate(o_heads, axis=1))
    for hd in range(GLA_HEADS):
        st_ref[hd] = states[hd]
    o_all = jnp.concatenate(o_chunks, axis=0)

    o_parts = []
    for hd in range(GLA_HEADS):
        vs = slice(hd * GLA_HEAD_V, (hd + 1) * GLA_HEAD_V)
        oh = o_all[:, vs]
        ms = jnp.mean(oh * oh, axis=-1, keepdims=True)
        o_parts.append(oh * lax.rsqrt(ms + RMS_EPS) * gnorm_ref[:, vs])
    on = jnp.concatenate(o_parts, axis=1) * (g * _sigmoid(g))
    out_b = _dot(on.astype(BF16), wpgla_ref[:, :D_MODEL])

    merged = _sigmoid(ga) * out_a + _sigmoid(gb) * out_b
    y = _dot(merged.astype(BF16), wo_ref[:, :D_MODEL]) + bo_ref[...]
    zbuf[...] = DN_ALPHA * x + y


def _mixer(x, wmain, bmain, wbg, bbg, walr, balr, conv_w, conv_b, wgate, bgate, lam, wa2, ba2, gnorm,
           wprnn, wpgla, wo, bo, ln1g, ln1b, wr, br):
    B, S, _ = x.shape
    T = min(MIX_TILE, S)
    assert S % T == 0 and T % GLA_CHUNK == 0
    N = B * S
    nt = N // T
    mix_tile = lambda i: jnp.minimum(i, nt - 1)
    tail_tile = lambda i: jnp.maximum(i - 1, 0)
    weights = (wmain, bmain, wbg, bbg, walr, balr, conv_w, conv_b, wgate, bgate, lam, wa2, ba2, gnorm,
               wprnn, wpgla, wo, bo, ln1g, ln1b, wr, br)
    return pl.pallas_call(
        functools.partial(_mixer_kernel, tile=T, steps_per_seq=S // T),
        grid=(nt + 1,),
        in_specs=[pl.BlockSpec((T, D_MODEL), lambda i: (mix_tile(i), 0))] + [_const_spec(w.shape) for w in weights],
        out_specs=[pl.BlockSpec((T, D_MODEL), lambda i: (tail_tile(i), 0)),
                   pl.BlockSpec((T * TOK_ROWS, LANES), lambda i: (tail_tile(i), 0)),
                   pl.BlockSpec((SUBLANES, T), lambda i: (0, tail_tile(i))),
                   pl.BlockSpec((SUBLANES, T), lambda i: (0, tail_tile(i))),
                   pl.BlockSpec((1, N_EXPERTS, LANES), lambda i: (tail_tile(i), 0, 0))],
        out_shape=[jax.ShapeDtypeStruct((N, D_MODEL), F32),
                   jax.ShapeDtypeStruct((N * TOK_ROWS, LANES), jnp.uint32),
                   jax.ShapeDtypeStruct((SUBLANES, N), jnp.int32),
                   jax.ShapeDtypeStruct((SUBLANES, N), F32),
                   jax.ShapeDtypeStruct((nt, N_EXPERTS, LANES), jnp.int32)],
        scratch_shapes=[pltpu.VMEM((T + SUBLANES, RNN_WIDTH), F32),
                        pltpu.VMEM((SUBLANES, RNN_WIDTH), F32),
                        pltpu.VMEM((GLA_HEADS, GLA_HEAD_V, GLA_HEAD_K), F32),
                        pltpu.VMEM((T, RNN_WIDTH), F32),
                        pltpu.VMEM((T, D_MODEL), F32)],
        compiler_params=pltpu.CompilerParams(dimension_semantics=("arbitrary",), vmem_limit_bytes=VMEM_LIMIT),
        name="mixer",
    )(x.reshape(N, D_MODEL), *weights)


def _route_tile(x1b, wr, br):
    T = x1b.shape[0]
    logits = _dot_nt(wr, x1b) + br
    row8 = lax.broadcasted_iota(jnp.int32, (SUBLANES, T), 0)
    row8f = row8.astype(F32)
    neg = jnp.float32(-jnp.inf)
    first = lambda hit: jnp.min(jnp.where(hit, row8f, float(SUBLANES)), axis=0, keepdims=True)

    gl = jnp.where(row8 < N_GROUPS, logits[0:SUBLANES, :], neg)
    gmax = jnp.max(gl, axis=0, keepdims=True)
    grp = first(gl == gmax)
    p_grp = 1.0 / jnp.sum(jnp.exp(gl - gmax), axis=0, keepdims=True)

    e_sel = jnp.zeros((EXPERTS_PER_GROUP, T), F32)
    for gi in range(N_GROUPS):
        lo = SUBLANES + gi * EXPERTS_PER_GROUP
        e_sel = jnp.where(grp == float(gi), logits[lo:lo + EXPERTS_PER_GROUP, :], e_sel)
    m1 = jnp.max(e_sel, axis=0, keepdims=True)
    i1 = first(e_sel == m1)
    e_rest = jnp.where(row8f == i1, neg, e_sel)
    m2 = jnp.max(e_rest, axis=0, keepdims=True)
    i2 = first(e_rest == m2)
    e21 = jnp.exp(m2 - m1)
    p1 = 1.0 / (1.0 + e21)
    w0 = p_grp * p1
    w1 = p_grp * (e21 * p1)
    eid0 = (grp * EXPERTS_PER_GROUP + i1).astype(jnp.int32)
    eid1 = (grp * EXPERTS_PER_GROUP + i2).astype(jnp.int32)

    erow = lax.broadcasted_iota(jnp.int32, (N_EXPERTS, T), 0)
    oh0 = jnp.where(erow == eid0, 1.0, 0.0)
    oh1 = jnp.where(erow == eid1, 1.0, 0.0)
    both = oh0 + oh1
    ti = lax.broadcasted_iota(jnp.int32, (T, T), 0)
    tj = lax.broadcasted_iota(jnp.int32, (T, T), 1)
    before = jnp.where(ti < tj, 1.0, 0.0).astype(BF16)
    prior = _dot(both.astype(BF16), before)
    rank0 = jnp.sum(prior * oh0, axis=0, keepdims=True).astype(jnp.int32)
    rank1 = jnp.sum(prior * oh1, axis=0, keepdims=True).astype(jnp.int32)
    total = (prior[:, T - 1:T] + both[:, T - 1:T]).astype(jnp.int32)

    info = jnp.where(row8 == 0, eid0, jnp.where(row8 == 1, eid1, jnp.where(row8 == 2, rank0,
                     jnp.where(row8 == 3, rank1, 0))))
    return info, jnp.where(row8 == 0, w0, jnp.where(row8 == 1, w1, 0.0)), total


def _dispatch_kernel(zero_blk_ref, dest_ref, xpk_ref, xs_ref, zbuf, sem, zsem, *, tile):
    T = tile

    @pl.when(pl.program_id(0) == 0)
    def _():
        zbuf[...] = jnp.zeros_like(zbuf)

        def zero_copy(j):
            row = pl.multiple_of(jnp.maximum(zero_blk_ref[j], 0) * TOK_ROWS, MOE_ROWS * TOK_ROWS)
            return pltpu.make_async_copy(zbuf, xs_ref.at[pl.ds(row, MOE_ROWS * TOK_ROWS)], zsem)

        for j in range(2 * N_EXPERTS):
            @pl.when(zero_blk_ref[j] >= 0)
            def _():
                zero_copy(j).start()
        for j in range(2 * N_EXPERTS):
            @pl.when(zero_blk_ref[j] >= 0)
            def _():
                zero_copy(j).wait()

    for t in range(T):
        for kk in range(2):
            row = pl.multiple_of(dest_ref[kk, t] * TOK_ROWS, TOK_ROWS)
            pltpu.make_async_copy(xpk_ref.at[pl.ds(t * TOK_ROWS, TOK_ROWS)], xs_ref.at[pl.ds(row, TOK_ROWS)],
                                  sem).start(priority=kk)
    for kk in range(2):
        pltpu.make_async_copy(xpk_ref, xs_ref.at[pl.ds(0, T * TOK_ROWS)], sem).wait()


def _dispatch(last_blk, dest, xpk, n_rows):
    N = xpk.shape[0] // TOK_ROWS
    T = min(DISPATCH_TILE, N)
    assert N % T == 0
    grid_spec = pltpu.PrefetchScalarGridSpec(
        num_scalar_prefetch=1,
        grid=(N // T,),
        in_specs=[pl.BlockSpec((2, T), lambda i, lb: (0, i), memory_space=pltpu.SMEM),
                  pl.BlockSpec((T * TOK_ROWS, LANES), lambda i, lb: (i, 0))],
        out_specs=pl.BlockSpec(memory_space=pl.ANY),
        scratch_shapes=[pltpu.VMEM((MOE_ROWS * TOK_ROWS, LANES), jnp.uint32),
                        pltpu.SemaphoreType.DMA(()), pltpu.SemaphoreType.DMA(())],
    )
    return pl.pallas_call(
        functools.partial(_dispatch_kernel, tile=T),
        grid_spec=grid_spec,
        out_shape=jax.ShapeDtypeStruct((n_rows * TOK_ROWS, LANES), jnp.uint32),
        compiler_params=pltpu.CompilerParams(dimension_semantics=("arbitrary",), has_side_effects=True),
        name="dispatch",
    )(last_blk, dest, xpk)


def _experts_kernel(blk_e_ref, nblk_ref, xs_ref, w1_ref, w3_ref, w2_ref, yt_ref,
                    w1b, w3b, w2b, stage, ids_v, ids_s, row_sem, ids_sem, *, n_tokens):
    i = pl.program_id(0)
    R = MOE_ROWS
    SR = R * PK_TILES
    n_used = nblk_ref[0]
    spare = 2 * n_tokens
    slot = i % 2
    prev = 1 - slot

    def ids_copy(s):
        return pltpu.make_async_copy(ids_v.at[pl.ds(s, 1)], ids_s.at[pl.ds(s, 1)], ids_sem)

    def row_copy(src_row, dst_row, r):
        return pltpu.make_async_copy(stage.at[pl.ds(pl.multiple_of(src_row * PK_TILES, PK_TILES), PK_TILES)],
                                     yt_ref.at[pl.ds(pl.multiple_of(dst_row * PK_TILES, PK_TILES), PK_TILES)],
                                     row_sem).start(priority=r % 2)

    def drain_rows():
        pltpu.make_async_copy(stage.at[pl.ds(0, SR)], yt_ref.at[pl.ds(0, SR)], row_sem).wait()

    def send_rows(s):
        for r in range(R):
            row_copy(s * R + r, ids_s[s, r], r)

    @pl.when(i == 0)
    def _():
        stage[...] = jnp.zeros_like(stage)
        ids_v[...] = (spare + lax.broadcasted_iota(jnp.int32, (SUBLANES, R), 0) * R
                      + lax.broadcasted_iota(jnp.int32, (SUBLANES, R), 1))
        ids_copy(1).start()
        for r in range(R):
            row_copy(r, spare + r, r)

    @pl.when((i == 0) | (blk_e_ref[i] != blk_e_ref[jnp.maximum(i - 1, 0)]))
    def _():
        w1b[...] = w1_ref[0].astype(BF16)
        w3b[...] = w3_ref[0].astype(BF16)
        w2b[...] = w2_ref[0].astype(BF16)

    @pl.when(i < n_used)
    def _():
        ids_copy(prev).wait()
        drain_rows()
        send_rows(prev)

        meta = lambda row: pltpu.bitcast(xs_ref[pl.ds(row, R, stride=TOK_ROWS), :], jnp.int32)
        second = jnp.where(meta(ROW_EXPERT) != blk_e_ref[i], 1, 0)
        pad_row = spare + slot * R + lax.broadcasted_iota(jnp.int32, (R, LANES), 0)
        dst = jnp.where(meta(ROW_VALID) != 0, 2 * meta(ROW_TOKEN) + second, pad_row)
        ids_v[pl.ds(slot, 1), :] = dst.astype(F32).T[0:1, :].astype(jnp.int32)
        ids_copy(slot).start()

        u = jnp.concatenate([xs_ref[pl.ds(c, R, stride=TOK_ROWS), :] for c in range(PK_TILES)], axis=1)
        xb = _unpack_bf16_pairs(u).astype(BF16)
        h1 = _dot(xb, w1b[...])
        h3 = _dot(xb, w3b[...])
        hact = (h1 * _sigmoid(h1)) * h3
        y = _dot(hact.astype(BF16), w2b[...])
        pk = _pack_bf16_pairs(y)
        for c in range(PK_TILES):
            stage[pl.ds(slot * SR + c, R, stride=PK_TILES), :] = pk[:, c * LANES:(c + 1) * LANES]

    @pl.when(i == n_used)
    def _():
        ids_copy(prev).wait()
        drain_rows()
        send_rows(prev)
        drain_rows()


def _experts(blk_e, nblk, xs, w1, w3, w2, n_tokens):
    P = xs.shape[0] // TOK_ROWS
    nb = P // MOE_ROWS
    assert blk_e.shape[0] == nb + 1
    grid_spec = pltpu.PrefetchScalarGridSpec(
        num_scalar_prefetch=2,
        grid=(nb + 1,),
        in_specs=[pl.BlockSpec((MOE_ROWS * TOK_ROWS, LANES), lambda i, be, n: (jnp.minimum(i, n[0] - 1), 0)),
                  pl.BlockSpec((1, D_MODEL, EXPERT_FF), lambda i, be, n: (be[i], 0, 0)),
                  pl.BlockSpec((1, D_MODEL, EXPERT_FF), lambda i, be, n: (be[i], 0, 0)),
                  pl.BlockSpec((1, EXPERT_FF, D_MODEL), lambda i, be, n: (be[i], 0, 0))],
        out_specs=pl.BlockSpec(memory_space=pl.ANY),
        scratch_shapes=[pltpu.VMEM((D_MODEL, EXPERT_FF), BF16), pltpu.VMEM((D_MODEL, EXPERT_FF), BF16),
                        pltpu.VMEM((EXPERT_FF, D_MODEL), BF16),
                        pltpu.VMEM((2 * MOE_ROWS * PK_TILES, LANES), jnp.uint32),
                        pltpu.VMEM((SUBLANES, MOE_ROWS), jnp.int32),
                        pltpu.SMEM((2, MOE_ROWS), jnp.int32),
                        pltpu.SemaphoreType.DMA(()), pltpu.SemaphoreType.DMA(())],
    )
    return pl.pallas_call(
        functools.partial(_experts_kernel, n_tokens=n_tokens),
        grid_spec=grid_spec,
        out_shape=jax.ShapeDtypeStruct(((2 * n_tokens + 2 * MOE_ROWS) * PK_TILES, LANES), jnp.uint32),
        compiler_params=pltpu.CompilerParams(dimension_semantics=("arbitrary",), vmem_limit_bytes=VMEM_LIMIT,
                                             has_side_effects=True),
        name="experts",
    )(blk_e, nblk, xs, w1, w3, w2)


def _combine_kernel(x1_ref, wts_ref, yt_ref, g_ref, b_ref, out_ref, *, tile):
    T = tile
    wpad = jnp.concatenate([wts_ref[...], jnp.zeros((LANES - SUBLANES, T), F32)], axis=0)
    wt = wpad.T
    slot = lambda s: _unpack_bf16_pairs(jnp.concatenate(
        [yt_ref[pl.ds(s * PK_TILES + c, T, stride=2 * PK_TILES), :] for c in range(PK_TILES)], axis=1))
    y = wt[:, 0:1] * slot(0) + wt[:, 1:2] * slot(1)
    out_ref[...] = _layer_norm(DN_ALPHA * x1_ref[...] + y, g_ref[...], b_ref[...])


def _combine(x1f, wts, yt, g, b):
    N = x1f.shape[0]
    T = min(COMBINE_TILE, N)
    assert N % T == 0
    return pl.pallas_call(
        functools.partial(_combine_kernel, tile=T),
        grid=(N // T,),
        in_specs=[pl.BlockSpec((T, D_MODEL), lambda i: (i, 0)),
                  pl.BlockSpec((SUBLANES, T), lambda i: (0, i)),
                  pl.BlockSpec((T * 2 * PK_TILES, LANES), lambda i: (i, 0)),
                  _const_spec(g.shape), _const_spec(b.shape)],
        out_specs=pl.BlockSpec((T, D_MODEL), lambda i: (i, 0)),
        out_shape=jax.ShapeDtypeStruct((N, D_MODEL), F32),
        compiler_params=pltpu.CompilerParams(dimension_semantics=("arbitrary",)),
        name="combine",
    )(x1f, wts, yt, g, b)


def _odd_tiles(w):
    return jnp.pad(w, ((0, 0), (0, LANES))).astype(BF16)


def _pack_mixer_weights(w_in, b_in, rg_w_a, rg_w_x, rg_b_a, rg_b_x, gla_w_a2):
    wmain = _odd_tiles(w_in[:, :C_MAIN_END])
    wbg = _odd_tiles(w_in[:, C_GATES_START:])
    walr = jnp.pad(w_in[:, C_MAIN_END:C_GATES_START], ((0, 0), (0, LANES - GLA_RANK))).astype(BF16)
    bmain = b_in[None, :C_MAIN_END]
    bbg = b_in[None, C_GATES_START:]
    balr = jnp.pad(b_in[None, C_MAIN_END:C_GATES_START], ((0, 0), (0, LANES - GLA_RANK)))
    zero = jnp.zeros((RNN_BLOCK_W, RNN_BLOCK_W), w_in.dtype)
    tiles = []
    for p in range(RNN_BLOCKS // 2):
        top = jnp.concatenate([rg_w_a[2 * p], zero, rg_w_x[2 * p], zero], axis=1)
        bot = jnp.concatenate([zero, rg_w_a[2 * p + 1], zero, rg_w_x[2 * p + 1]], axis=1)
        tiles.append(jnp.concatenate([top, bot], axis=0))
    wgate = jnp.stack(tiles).astype(BF16)
    bgate = jnp.concatenate([rg_b_a, rg_b_x])[None, :]
    wa2 = jnp.concatenate([gla_w_a2, jnp.zeros((LANES - GLA_RANK, GLA_DK), gla_w_a2.dtype)], axis=0).astype(BF16)
    return (wmain, bmain, wbg, bbg, walr, balr), wgate, bgate, wa2


def _layer(x, w_in, b_in, conv_w, conv_b, rg_w_a, rg_b_a, rg_w_x, rg_b_x, rg_lambda, gla_w_a2, gla_b_a,
           gla_norm_g, w_proj_rnn, w_proj_gla, w_o, b_o, ln1_g, ln1_b, router_w_group, router_b_group,
           router_w_expert, router_b_expert, exp_w1, exp_w3, exp_w2, ln2_g, ln2_b):
    B, S, _ = x.shape
    N = B * S
    row = lambda p: p[None, :]

    w_slices, wgate, bgate, wa2 = _pack_mixer_weights(w_in, b_in, rg_w_a, rg_w_x, rg_b_a, rg_b_x, gla_w_a2)
    wr = jnp.concatenate([router_w_group.T, jnp.zeros((SUBLANES - N_GROUPS, D_MODEL), F32), router_w_expert.T],
                         axis=0).astype(BF16)
    br = jnp.concatenate([router_b_group, jnp.zeros((SUBLANES - N_GROUPS,), F32), router_b_expert])[:, None]
    x1f, xpk, info, wts, tcnt = _mixer(
        x, *w_slices, conv_w, row(conv_b), wgate, bgate, row(rg_lambda), wa2, row(gla_b_a), row(gla_norm_g),
        _odd_tiles(w_proj_rnn), _odd_tiles(w_proj_gla), _odd_tiles(w_o), row(b_o), row(ln1_g), row(ln1_b), wr, br)

    tcnt = tcnt[:, :, 0]
    nt = tcnt.shape[0]
    tot = jnp.sum(tcnt, axis=0)
    pcount = (tot + MOE_ROWS - 1) // MOE_ROWS * MOE_ROWS
    pend = jnp.cumsum(pcount)
    base = (pend - pcount)[None, :] + jnp.cumsum(tcnt, axis=0) - tcnt
    base_tok = jnp.repeat(base.T, N // nt, axis=1)
    experts_col = jnp.arange(N_EXPERTS, dtype=jnp.int32)[:, None, None]
    dest = jnp.sum(jnp.where(info[None, 0:2] == experts_col, base_tok[:, None, :], 0), axis=0) + info[2:4]
    nb = -(-(2 * N) // MOE_ROWS) + N_EXPERTS
    P = nb * MOE_ROWS
    nblk = (pend[-1] // MOE_ROWS).astype(jnp.int32)
    blk_start = jnp.minimum(jnp.arange(nb + 1, dtype=jnp.int32), nblk - 1) * MOE_ROWS
    blk_e = jnp.sum((blk_start[:, None] >= pend[None, :]).astype(jnp.int32), axis=1)
    blk_e = jnp.minimum(blk_e, N_EXPERTS - 1)

    last_blk = jnp.where(tot > 0, pend - MOE_ROWS, -1)
    tail_blk = nblk + jnp.arange(N_EXPERTS, dtype=jnp.int32)
    tail_blk = jnp.where(tail_blk < nb, tail_blk * MOE_ROWS, -1)
    xs = _dispatch(jnp.concatenate([last_blk, tail_blk]).astype(jnp.int32), dest, xpk, P)
    yt = _experts(blk_e, nblk[None], xs, exp_w1, exp_w3, exp_w2, N)
    out = _combine(x1f, wts, yt, row(ln2_g), row(ln2_b))
    return out.reshape(B, S, D_MODEL)


def kernel(x, w_in, b_in, conv_w, conv_b, rg_w_a, rg_b_a, rg_w_x, rg_b_x, rg_lambda, gla_w_a2, gla_b_a, gla_norm_g, w_proj_rnn, w_proj_gla, w_o, b_o, ln1_g, ln1_b, router_w_group, router_b_group, router_w_expert, router_b_expert, exp_w1, exp_w3, exp_w2, ln2_g, ln2_b):
    h = x
    for l in range(w_in.shape[0]):
        h = _layer(h, w_in[l], b_in[l], conv_w[l], conv_b[l], rg_w_a[l], rg_b_a[l], rg_w_x[l], rg_b_x[l],
                   rg_lambda[l], gla_w_a2[l], gla_b_a[l], gla_norm_g[l], w_proj_rnn[l], w_proj_gla[l], w_o[l],
                   b_o[l], ln1_g[l], ln1_b[l], router_w_group[l], router_b_group[l], router_w_expert[l],
                   router_b_expert[l], exp_w1[l], exp_w3[l], exp_w2[l], ln2_g[l], ln2_b[l])
    return h
```

```python
import functools

import jax
import jax.numpy as jnp
from jax import lax
from jax.experimental import pallas as pl
from jax.experimental.pallas import tpu as pltpu

F32 = jnp.float32
BF16 = jnp.bfloat16

D_MODEL = 1024
RNN_WIDTH = 1024
RNN_BLOCKS = 8
RNN_BLOCK_W = RNN_WIDTH // RNN_BLOCKS
CONV_WIDTH = 4
LRU_C = 8.0
GLA_HEADS = 4
GLA_DK = D_MODEL // 2
GLA_DV = D_MODEL
GLA_HEAD_K = GLA_DK // GLA_HEADS
GLA_HEAD_V = GLA_DV // GLA_HEADS
GLA_RANK = 16
GLA_TAU = 16.0
GLA_CHUNK = 64
N_GROUPS = 4
EXPERTS_PER_GROUP = 8
N_EXPERTS = N_GROUPS * EXPERTS_PER_GROUP
EXPERT_FF = 512
DN_ALPHA = 2.0 ** 0.25
LN_EPS = 1e-5
RMS_EPS = 1e-6

LANES = 128
SUBLANES = 8
VMEM_LIMIT = 56 * 1024 * 1024

C_RX, C_RY, C_Q, C_K, C_V, C_G = 0, 1024, 2048, 2560, 3072, 4096
C_GA, C_GB = 0, 1024
C_MAIN_END = 5120
C_GATES_START = C_MAIN_END + GLA_RANK

MIX_TILE = 256
MOE_ROWS = 256
DISPATCH_TILE = 512
COMBINE_TILE = 256
ROUTE_ROWS = 8 + N_EXPERTS
PK_TILES = D_MODEL // 2 // LANES
TOK_ROWS = PK_TILES + 1
META_EXPERT, META_TOKEN, META_VALID = 0, 1, 2


def _sigmoid(v):
    return 1.0 / (1.0 + jnp.exp(-v))


def _softplus(v):
    return jnp.maximum(v, 0.0) + jnp.log1p(jnp.exp(-jnp.abs(v)))


def _layer_norm(v, g, b):
    mu = jnp.mean(v, axis=-1, keepdims=True)
    c = v - mu
    var = jnp.mean(c * c, axis=-1, keepdims=True)
    return c * lax.rsqrt(var + LN_EPS) * g + b


def _dot(a, b):
    return jnp.dot(a, b, preferred_element_type=F32)


def _dot_nt(a, b):
    return lax.dot_general(a, b, (((1,), (1,)), ((), ())), preferred_element_type=F32)


def _dot_tn(a, b):
    return lax.dot_general(a, b, (((0,), (0,)), ((), ())), preferred_element_type=F32)


def _pack_bf16_pairs(v):
    half = v.shape[1] // 2
    bits = pltpu.bitcast(v.astype(BF16).astype(F32), jnp.uint32)
    return (bits[:, :half] >> 16) | bits[:, half:]


def _unpack_bf16_pairs(u):
    lo = pltpu.bitcast(u << 16, F32)
    hi = pltpu.bitcast(u & jnp.uint32(0xFFFF0000), F32)
    return jnp.concatenate([lo, hi], axis=1)


def _const_spec(shape):
    nd = len(shape)
    return pl.BlockSpec(shape, lambda *_: (0,) * nd, pipeline_mode=pl.Buffered(1))


def _mixer_kernel(x_ref, wmain_ref, bmain_ref, wbg_ref, bbg_ref, walr_ref, balr_ref,
                  convw_ref, convb_ref, wgate_ref, bgate_ref, lam_ref,
                  wa2_ref, ba2_ref, gnorm_ref, wprnn_ref, wpgla_ref, wo_ref, bo_ref, ln1g_ref, ln1b_ref,
                  wr_ref, br_ref,
                  x1_ref, xpk_ref, info_ref, wts_ref, cnt_ref, rxbuf, hcar, st_ref, hbuf, zbuf,
                  *, tile, steps_per_seq):
    T = tile
    step = pl.program_id(0)

    @pl.when(step == 0)
    def _():
        zbuf[...] = jnp.zeros_like(zbuf)

    @pl.when(step % steps_per_seq == 0)
    def _():
        rxbuf[0:SUBLANES, :] = jnp.zeros((SUBLANES, RNN_WIDTH), F32)
        hcar[...] = jnp.zeros_like(hcar)
        st_ref[...] = jnp.zeros_like(st_ref)

    x = x_ref[...]
    xb = x.astype(BF16)

    def proj(c0, c1, w_ref=wmain_ref, b_ref=bmain_ref):
        return _dot(xb, w_ref[:, c0:c1]) + b_ref[:, c0:c1]

    rx = proj(C_RX, C_RX + RNN_WIDTH)
    rxbuf[SUBLANES:SUBLANES + T, :] = rx
    u = convb_ref[...] + convw_ref[CONV_WIDTH - 1:CONV_WIDTH, :] * rx
    for j in range(1, CONV_WIDTH):
        u = u + convw_ref[CONV_WIDTH - 1 - j:CONV_WIDTH - j, :] * rxbuf[SUBLANES - j:SUBLANES - j + T, :]
    rxbuf[0:SUBLANES, :] = rxbuf[T:T + SUBLANES, :]

    qk = proj(C_Q, C_Q + 2 * GLA_DK)
    q = qk[:, :GLA_DK] * (GLA_HEAD_K ** -0.5)
    k = qk[:, GLA_DK:]
    alr = proj(0, LANES, walr_ref, balr_ref)

    x1 = _layer_norm(zbuf[...], ln1g_ref[...], ln1b_ref[...])
    x1_ref[...] = x1
    info, wts, total = _route_tile(x1.astype(BF16), wr_ref[...], br_ref[...])
    info_ref[...] = info
    wts_ref[...] = wts
    cnt_ref[0] = jnp.broadcast_to(total, (N_EXPERTS, LANES))
    pk = _pack_bf16_pairs(x1)
    for c in range(PK_TILES):
        xpk_ref[pl.ds(c, T, stride=TOK_ROWS), :] = pk[:, c * LANES:(c + 1) * LANES]
    first_expert = jnp.broadcast_to(info[0:1, :].astype(F32), (LANES, T)).T.astype(jnp.int32)
    token = lax.broadcasted_iota(jnp.int32, (T, LANES), 0) + jnp.maximum(step - 1, 0) * T
    lane = lax.broadcasted_iota(jnp.int32, (T, LANES), 1)
    meta = jnp.where(lane == META_EXPERT, first_expert,
                     jnp.where(lane == META_TOKEN, token, jnp.where(lane == META_VALID, 1, 0)))
    xpk_ref[pl.ds(PK_TILES, T, stride=TOK_ROWS), :] = pltpu.bitcast(meta, jnp.uint32)

    r_parts, i_parts = [], []
    for p in range(RNN_BLOCKS // 2):
        up = u[:, 256 * p:256 * (p + 1)].astype(BF16)
        gp = _dot(up, wgate_ref[p])
        r_parts.append(gp[:, :256])
        i_parts.append(gp[:, 256:])
    r = _sigmoid(jnp.concatenate(r_parts, axis=1) + bgate_ref[:, :RNN_WIDTH])
    ig = _sigmoid(jnp.concatenate(i_parts, axis=1) + bgate_ref[:, RNN_WIDTH:])
    v = proj(C_V, C_V + GLA_DV)
    ry = proj(C_RY, C_RY + RNN_WIDTH)

    z = _dot(alr.astype(BF16), wa2_ref[...]) + ba2_ref[...]
    la = -_softplus(-z) * (1.0 / GLA_TAU)
    ri = lax.broadcasted_iota(jnp.int32, (T, T), 0)
    ci = lax.broadcasted_iota(jnp.int32, (T, T), 1)
    chunk_start = (ri >> 6) << 6
    tri = jnp.where((ci <= ri) & (ci >= chunk_start), 1.0, 0.0).astype(BF16)
    la_hi = la.astype(BF16)
    la_lo = (la - la_hi.astype(F32)).astype(BF16)
    bcum = _dot(tri, la_hi) + _dot(tri, la_lo)

    g = proj(C_G, C_G + GLA_DV)

    log_a = (-LRU_C) * r * _softplus(-lam_ref[...])
    a = jnp.exp(log_a)
    m2 = -jnp.tanh(log_a) * (1.0 + a * a)
    bv = jnp.where(m2 > 0.0, m2 * lax.rsqrt(m2), 0.0) * (ig * u)

    cr = lax.broadcasted_iota(jnp.int32, (GLA_CHUNK, GLA_CHUNK), 0)
    cc = lax.broadcasted_iota(jnp.int32, (GLA_CHUNK, GLA_CHUNK), 1)
    causal = cr >= cc
    n_chunks = T // GLA_CHUNK
    heads = [(slice(hd * GLA_HEAD_K, (hd + 1) * GLA_HEAD_K), slice(hd * GLA_HEAD_V, (hd + 1) * GLA_HEAD_V))
             for hd in range(GLA_HEADS)]
    qd_c, ki_c, ke_c, vv_c, dec_c = [], [], [], [], []
    for c in range(n_chunks):
        r0 = c * GLA_CHUNK
        bc = bcum[r0:r0 + GLA_CHUNK, :]
        bl = bcum[r0 + GLA_CHUNK - 1:r0 + GLA_CHUNK, :]
        kc = k[r0:r0 + GLA_CHUNK, :]
        qd_c.append((q[r0:r0 + GLA_CHUNK, :] * jnp.exp(bc)).astype(BF16))
        ki_c.append((kc * jnp.exp(-bc)).astype(BF16))
        ke_c.append((kc * jnp.exp(bl - bc)).astype(BF16))
        vv_c.append(v[r0:r0 + GLA_CHUNK, :].astype(BF16))
        dec_c.append(jnp.exp(bl))
    scores = [[jnp.where(causal, _dot_nt(qd_c[c][:, ks], ki_c[c][:, ks]), 0.0).astype(BF16) for ks, _ in heads]
              for c in range(n_chunks)]
    intra = [[_dot(scores[c][hd], vv_c[c][:, vs]) for hd, (_, vs) in enumerate(heads)] for c in range(n_chunks)]
    incr = [[_dot_tn(vv_c[c][:, vs], ke_c[c][:, ks]) for ks, vs in heads] for c in range(n_chunks)]

    n_groups = T // SUBLANES
    sub = lax.broadcasted_iota(jnp.int32, (n_groups, SUBLANES, RNN_WIDTH), 1)
    sa = a.reshape(n_groups, SUBLANES, RNN_WIDTH)
    sb = bv.reshape(n_groups, SUBLANES, RNN_WIDTH)
    for s in (1, 2, 4):
        keep = sub >= s
        sb = sa * jnp.where(keep, pltpu.roll(sb, s, 1), 0.0) + sb
        sa = sa * jnp.where(keep, pltpu.roll(sa, s, 1), 1.0)
    carry = hcar[0:1, :]
    for gi in range(n_groups):
        hg = sb[gi] + sa[gi] * carry
        hbuf[gi * SUBLANES:(gi + 1) * SUBLANES, :] = hg
        carry = hg[SUBLANES - 1:SUBLANES, :]
    hcar[0:1, :] = carry
    h = hbuf[...]
    ga = proj(C_GA, C_GA + D_MODEL, wbg_ref, bbg_ref)

    out_a = _dot((h * jax.nn.gelu(ry)).astype(BF16), wprnn_ref[:, :D_MODEL])
    gb = proj(C_GB, C_GB + D_MODEL, wbg_ref, bbg_ref)

    states = [st_ref[hd] for hd in range(GLA_HEADS)]
    o_chunks = []
    for c in range(n_chunks):
        o_heads = []
        for hd, (ks, _) in enumerate(heads):
            o_heads.append(intra[c][hd] + _dot_nt(qd_c[c][:, ks], states[hd].astype(BF16)))
            states[hd] = states[hd] * dec_c[c][:, ks] + incr[c][hd]
        o_chunks.append(jnp.concatenate(o_heads, axis=1))
    for hd in range(GLA_HEADS):
        st_ref[hd] = states[hd]
    o_all = jnp.concatenate(o_chunks, axis=0)

    o_parts = []
    for hd in range(GLA_HEADS):
        vs = slice(hd * GLA_HEAD_V, (hd + 1) * GLA_HEAD_V)
        oh = o_all[:, vs]
        ms = jnp.mean(oh * oh, axis=-1, keepdims=True)
        o_parts.append(oh * lax.rsqrt(ms + RMS_EPS) * gnorm_ref[:, vs])
    on = jnp.concatenate(o_parts, axis=1) * (g * _sigmoid(g))
    out_b = _dot(on.astype(BF16), wpgla_ref[:, :D_MODEL])

    merged = _sigmoid(ga) * out_a + _sigmoid(gb) * out_b
    y = _dot(merged.astype(BF16), wo_ref[:, :D_MODEL]) + bo_ref[...]
    zbuf[...] = DN_ALPHA * x + y


def _mixer(x, wmain, bmain, wbg, bbg, walr, balr, conv_w, conv_b, wgate, bgate, lam, wa2, ba2, gnorm,
           wprnn, wpgla, wo, bo, ln1g, ln1b, wr, br):
    B, S, _ = x.shape
    T = min(MIX_TILE, S)
    assert S % T == 0 and T % GLA_CHUNK == 0
    N = B * S
    nt = N // T
    mix_tile = lambda i: jnp.minimum(i, nt - 1)
    tail_tile = lambda i: jnp.maximum(i - 1, 0)
    weights = (wmain, bmain, wbg, bbg, walr, balr, conv_w, conv_b, wgate, bgate, lam, wa2, ba2, gnorm,
               wprnn, wpgla, wo, bo, ln1g, ln1b, wr, br)
    return pl.pallas_call(
        functools.partial(_mixer_kernel, tile=T, steps_per_seq=S // T),
        grid=(nt + 1,),
        in_specs=[pl.BlockSpec((T, D_MODEL), lambda i: (mix_tile(i), 0))] + [_const_spec(w.shape) for w in weights],
        out_specs=[pl.BlockSpec((T, D_MODEL), lambda i: (tail_tile(i), 0)),
                   pl.BlockSpec((T * TOK_ROWS, LANES), lambda i: (tail_tile(i), 0)),
                   pl.BlockSpec((SUBLANES, T), lambda i: (0, tail_tile(i))),
                   pl.BlockSpec((SUBLANES, T), lambda i: (0, tail_tile(i))),
                   pl.BlockSpec((1, N_EXPERTS, LANES), lambda i: (tail_tile(i), 0, 0))],
        out_shape=[jax.ShapeDtypeStruct((N, D_MODEL), F32),
                   jax.ShapeDtypeStruct((N * TOK_ROWS, LANES), jnp.uint32),
                   jax.ShapeDtypeStruct((SUBLANES, N), jnp.int32),
                   jax.ShapeDtypeStruct((SUBLANES, N), F32),
                   jax.ShapeDtypeStruct((nt, N_EXPERTS, LANES), jnp.int32)],
        scratch_shapes=[pltpu.VMEM((T + SUBLANES, RNN_WIDTH), F32),
                        pltpu.VMEM((SUBLANES, RNN_WIDTH), F32),
                        pltpu.VMEM((GLA_HEADS, GLA_HEAD_V, GLA_HEAD_K), F32),
                        pltpu.VMEM((T, RNN_WIDTH), F32),
                        pltpu.VMEM((T, D_MODEL), F32)],
        compiler_params=pltpu.CompilerParams(dimension_semantics=("arbitrary",), vmem_limit_bytes=VMEM_LIMIT),
        name="mixer",
    )(x.reshape(N, D_MODEL), *weights)


def _route_tile(x1b, wr, br):
    T = x1b.shape[0]
    logits = _dot_nt(wr, x1b) + br
    row8 = lax.broadcasted_iota(jnp.int32, (SUBLANES, T), 0)
    row8f = row8.astype(F32)
    neg = jnp.float32(-jnp.inf)
    first = lambda hit: jnp.min(jnp.where(hit, row8f, float(SUBLANES)), axis=0, keepdims=True)

    gl = jnp.where(row8 < N_GROUPS, logits[0:SUBLANES, :], neg)
    gmax = jnp.max(gl, axis=0, keepdims=True)
    grp = first(gl == gmax)
    p_grp = 1.0 / jnp.sum(jnp.exp(gl - gmax), axis=0, keepdims=True)

    e_sel = jnp.zeros((EXPERTS_PER_GROUP, T), F32)
    for gi in range(N_GROUPS):
        lo = SUBLANES + gi * EXPERTS_PER_GROUP
        e_sel = jnp.where(grp == float(gi), logits[lo:lo + EXPERTS_PER_GROUP, :], e_sel)
    m1 = jnp.max(e_sel, axis=0, keepdims=True)
    i1 = first(e_sel == m1)
    e_rest = jnp.where(row8f == i1, neg, e_sel)
    m2 = jnp.max(e_rest, axis=0, keepdims=True)
    i2 = first(e_rest == m2)
    e21 = jnp.exp(m2 - m1)
    p1 = 1.0 / (1.0 + e21)
    w0 = p_grp * p1
    w1 = p_grp * (e21 * p1)
    eid0 = (grp * EXPERTS_PER_GROUP + i1).astype(jnp.int32)
    eid1 = (grp * EXPERTS_PER_GROUP + i2).astype(jnp.int32)

    erow = lax.broadcasted_iota(jnp.int32, (N_EXPERTS, T), 0)
    oh0 = jnp.where(erow == eid0, 1.0, 0.0)
    oh1 = jnp.where(erow == eid1, 1.0, 0.0)
    both = oh0 + oh1
    ti = lax.broadcasted_iota(jnp.int32, (T, T), 0)
    tj = lax.broadcasted_iota(jnp.int32, (T, T), 1)
    before = jnp.where(ti < tj, 1.0, 0.0).astype(BF16)
    prior = _dot(both.astype(BF16), before)
    rank0 = jnp.sum(prior * oh0, axis=0, keepdims=True).astype(jnp.int32)
    rank1 = jnp.sum(prior * oh1, axis=0, keepdims=True).astype(jnp.int32)
    total = (prior[:, T - 1:T] + both[:, T - 1:T]).astype(jnp.int32)

    info = jnp.where(row8 == 0, eid0, jnp.where(row8 == 1, eid1, jnp.where(row8 == 2, rank0,
                     jnp.where(row8 == 3, rank1, 0))))
    return info, jnp.where(row8 == 0, w0, jnp.where(row8 == 1, w1, 0.0)), total


def _dispatch_kernel(zero_blk_ref, dest_ref, xpk_ref, xs_ref, zbuf, sem, zsem, *, tile):
    T = tile

    @pl.when(pl.program_id(0) == 0)
    def _():
        zbuf[...] = jnp.zeros_like(zbuf)

        def zero_copy(j):
            row = pl.multiple_of(jnp.maximum(zero_blk_ref[j], 0) * TOK_ROWS, MOE_ROWS * TOK_ROWS)
            return pltpu.make_async_copy(zbuf, xs_ref.at[pl.ds(row, MOE_ROWS * TOK_ROWS)], zsem)

        for j in range(2 * N_EXPERTS):
            @pl.when(zero_blk_ref[j] >= 0)
            def _():
                zero_copy(j).start()
        for j in range(2 * N_EXPERTS):
            @pl.when(zero_blk_ref[j] >= 0)
            def _():
                zero_copy(j).wait()

    for t in range(T):
        for kk in range(2):
            row = dest_ref[kk, t] * TOK_ROWS
            pltpu.make_async_copy(xpk_ref.at[pl.ds(t * TOK_ROWS, TOK_ROWS)], xs_ref.at[pl.ds(row, TOK_ROWS)],
                                  sem).start(priority=kk)
    for kk in range(2):
        pltpu.make_async_copy(xpk_ref, xs_ref.at[pl.ds(0, T * TOK_ROWS)], sem).wait()


def _dispatch(last_blk, dest, xpk, n_rows):
    N = xpk.shape[0] // TOK_ROWS
    T = min(DISPATCH_TILE, N)
    assert N % T == 0
    grid_spec = pltpu.PrefetchScalarGridSpec(
        num_scalar_prefetch=1,
        grid=(N // T,),
        in_specs=[pl.BlockSpec((2, T), lambda i, lb: (0, i), memory_space=pltpu.SMEM),
                  pl.BlockSpec((T * TOK_ROWS, LANES), lambda i, lb: (i, 0))],
        out_specs=pl.BlockSpec(memory_space=pl.ANY),
        scratch_shapes=[pltpu.VMEM((MOE_ROWS * TOK_ROWS, LANES), jnp.uint32),
                        pltpu.SemaphoreType.DMA(()), pltpu.SemaphoreType.DMA(())],
    )
    return pl.pallas_call(
        functools.partial(_dispatch_kernel, tile=T),
        grid_spec=grid_spec,
        out_shape=jax.ShapeDtypeStruct((n_rows * TOK_ROWS, LANES), jnp.uint32),
        compiler_params=pltpu.CompilerParams(dimension_semantics=("arbitrary",), has_side_effects=True),
        name="dispatch",
    )(last_blk, dest, xpk)


def _experts_kernel(blk_e_ref, nblk_ref, xs_ref, w1_ref, w3_ref, w2_ref, yt_ref,
                    w1b, w3b, w2b, stage, ids_v, ids_s, row_sem, ids_sem, *, n_tokens):
    i = pl.program_id(0)
    R = MOE_ROWS
    SR = R * PK_TILES
    n_used = nblk_ref[0]
    spare = 2 * n_tokens
    slot = i % 2
    prev = 1 - slot

    def ids_copy(s):
        return pltpu.make_async_copy(ids_v.at[pl.ds(s, 1)], ids_s.at[pl.ds(s, 1)], ids_sem)

    def row_copy(s, r, dst_row):
        src = pl.multiple_of((s * R + r) * PK_TILES, PK_TILES)
        dst = pl.multiple_of(dst_row * PK_TILES, PK_TILES)
        return pltpu.make_async_copy(stage.at[pl.ds(src, PK_TILES)], yt_ref.at[pl.ds(dst, PK_TILES)],
                                     row_sem.at[s]).start(priority=r % 2)

    def drain_rows(s):
        pltpu.make_async_copy(stage.at[pl.ds(0, SR)], yt_ref.at[pl.ds(0, SR)], row_sem.at[s]).wait()

    def send_rows(s):
        for r in range(R):
            row_copy(s, r, ids_s[s, r])

    @pl.when(i == 0)
    def _():
        stage[...] = jnp.zeros_like(stage)
        ids_v[...] = (spare + lax.broadcasted_iota(jnp.int32, (SUBLANES, R), 0) * R
                      + lax.broadcasted_iota(jnp.int32, (SUBLANES, R), 1))
        ids_copy(1).start()
        for r in range(R):
            row_copy(0, r, spare + r)

    @pl.when((i == 0) | (blk_e_ref[i] != blk_e_ref[jnp.maximum(i - 1, 0)]))
    def _():
        w1b[...] = w1_ref[0].astype(BF16)
        w3b[...] = w3_ref[0].astype(BF16)
        w2b[...] = w2_ref[0].astype(BF16)

    @pl.when(i < n_used)
    def _():
        ids_copy(prev).wait()
        send_rows(prev)

        meta = pltpu.bitcast(xs_ref[pl.ds(PK_TILES, R, stride=TOK_ROWS), :], jnp.int32).astype(F32).T
        first_expert = meta[META_EXPERT:META_EXPERT + 1, :]
        token = meta[META_TOKEN:META_TOKEN + 1, :]
        valid = meta[META_VALID:META_VALID + 1, :]
        second = jnp.where(first_expert != blk_e_ref[i].astype(F32), 1.0, 0.0)
        pad_row = (spare + slot * R + lax.broadcasted_iota(jnp.int32, (1, R), 1)).astype(F32)
        ids_v[pl.ds(slot, 1), :] = jnp.where(valid != 0.0, 2.0 * token + second, pad_row).astype(jnp.int32)
        ids_copy(slot).start()

        u = jnp.concatenate([xs_ref[pl.ds(c, R, stride=TOK_ROWS), :] for c in range(PK_TILES)], axis=1)
        xb = _unpack_bf16_pairs(u).astype(BF16)
        h1 = _dot(xb, w1b[...])
        h3 = _dot(xb, w3b[...])
        hact = (h1 * _sigmoid(h1)) * h3
        y = _dot(hact.astype(BF16), w2b[...])
        pk = _pack_bf16_pairs(y)
        drain_rows(slot)
        for c in range(PK_TILES):
            stage[pl.ds(slot * SR + c, R, stride=PK_TILES), :] = pk[:, c * LANES:(c + 1) * LANES]

    @pl.when(i == n_used)
    def _():
        ids_copy(prev).wait()
        send_rows(prev)
        drain_rows(slot)
        drain_rows(prev)


def _experts(blk_e, nblk, xs, w1, w3, w2, n_tokens):
    P = xs.shape[0] // TOK_ROWS
    nb = P // MOE_ROWS
    assert blk_e.shape[0] == nb + 1
    grid_spec = pltpu.PrefetchScalarGridSpec(
        num_scalar_prefetch=2,
        grid=(nb + 1,),
        in_specs=[pl.BlockSpec((MOE_ROWS * TOK_ROWS, LANES), lambda i, be, n: (jnp.minimum(i, n[0] - 1), 0)),
                  pl.BlockSpec((1, D_MODEL, EXPERT_FF), lambda i, be, n: (be[i], 0, 0)),
                  pl.BlockSpec((1, D_MODEL, EXPERT_FF), lambda i, be, n: (be[i], 0, 0)),
                  pl.BlockSpec((1, EXPERT_FF, D_MODEL), lambda i, be, n: (be[i], 0, 0))],
        out_specs=pl.BlockSpec(memory_space=pl.ANY),
        scratch_shapes=[pltpu.VMEM((D_MODEL, EXPERT_FF), BF16), pltpu.VMEM((D_MODEL, EXPERT_FF), BF16),
                        pltpu.VMEM((EXPERT_FF, D_MODEL), BF16),
                        pltpu.VMEM((2 * MOE_ROWS * PK_TILES, LANES), jnp.uint32),
                        pltpu.VMEM((SUBLANES, MOE_ROWS), jnp.int32),
                        pltpu.SMEM((2, MOE_ROWS), jnp.int32),
                        pltpu.SemaphoreType.DMA((2,)), pltpu.SemaphoreType.DMA(())],
    )
    return pl.pallas_call(
        functools.partial(_experts_kernel, n_tokens=n_tokens),
        grid_spec=grid_spec,
        out_shape=jax.ShapeDtypeStruct(((2 * n_tokens + 2 * MOE_ROWS) * PK_TILES, LANES), jnp.uint32),
        compiler_params=pltpu.CompilerParams(dimension_semantics=("arbitrary",), vmem_limit_bytes=VMEM_LIMIT,
                                             has_side_effects=True),
        name="experts",
    )(blk_e, nblk, xs, w1, w3, w2)


def _combine_kernel(x1_ref, wts_ref, yt_ref, g_ref, b_ref, out_ref, *, tile):
    T = tile
    wpad = jnp.concatenate([wts_ref[...], jnp.zeros((LANES - SUBLANES, T), F32)], axis=0)
    wt = wpad.T
    slot = lambda s: _unpack_bf16_pairs(jnp.concatenate(
        [yt_ref[pl.ds(s * PK_TILES + c, T, stride=2 * PK_TILES), :] for c in range(PK_TILES)], axis=1))
    y = wt[:, 0:1] * slot(0) + wt[:, 1:2] * slot(1)
    out_ref[...] = _layer_norm(DN_ALPHA * x1_ref[...] + y, g_ref[...], b_ref[...])


def _combine(x1f, wts, yt, g, b):
    N = x1f.shape[0]
    T = min(COMBINE_TILE, N)
    assert N % T == 0
    return pl.pallas_call(
        functools.partial(_combine_kernel, tile=T),
        grid=(N // T,),
        in_specs=[pl.BlockSpec((T, D_MODEL), lambda i: (i, 0)),
                  pl.BlockSpec((SUBLANES, T), lambda i: (0, i)),
                  pl.BlockSpec((T * 2 * PK_TILES, LANES), lambda i: (i, 0)),
                  _const_spec(g.shape), _const_spec(b.shape)],
        out_specs=pl.BlockSpec((T, D_MODEL), lambda i: (i, 0)),
        out_shape=jax.ShapeDtypeStruct((N, D_MODEL), F32),
        compiler_params=pltpu.CompilerParams(dimension_semantics=("arbitrary",)),
        name="combine",
    )(x1f, wts, yt, g, b)


def _odd_tiles(w):
    return jnp.pad(w, ((0, 0), (0, LANES))).astype(BF16)


def _pack_mixer_weights(w_in, b_in, rg_w_a, rg_w_x, rg_b_a, rg_b_x, gla_w_a2):
    wmain = _odd_tiles(w_in[:, :C_MAIN_END])
    wbg = _odd_tiles(w_in[:, C_GATES_START:])
    walr = jnp.pad(w_in[:, C_MAIN_END:C_GATES_START], ((0, 0), (0, LANES - GLA_RANK))).astype(BF16)
    bmain = b_in[None, :C_MAIN_END]
    bbg = b_in[None, C_GATES_START:]
    balr = jnp.pad(b_in[None, C_MAIN_END:C_GATES_START], ((0, 0), (0, LANES - GLA_RANK)))
    zero = jnp.zeros((RNN_BLOCK_W, RNN_BLOCK_W), w_in.dtype)
    tiles = []
    for p in range(RNN_BLOCKS // 2):
        top = jnp.concatenate([rg_w_a[2 * p], zero, rg_w_x[2 * p], zero], axis=1)
        bot = jnp.concatenate([zero, rg_w_a[2 * p + 1], zero, rg_w_x[2 * p + 1]], axis=1)
        tiles.append(jnp.concatenate([top, bot], axis=0))
    wgate = jnp.stack(tiles).astype(BF16)
    bgate = jnp.concatenate([rg_b_a, rg_b_x])[None, :]
    wa2 = jnp.concatenate([gla_w_a2, jnp.zeros((LANES - GLA_RANK, GLA_DK), gla_w_a2.dtype)], axis=0).astype(BF16)
    return (wmain, bmain, wbg, bbg, walr, balr), wgate, bgate, wa2


def _layer(x, w_in, b_in, conv_w, conv_b, rg_w_a, rg_b_a, rg_w_x, rg_b_x, rg_lambda, gla_w_a2, gla_b_a,
           gla_norm_g, w_proj_rnn, w_proj_gla, w_o, b_o, ln1_g, ln1_b, router_w_group, router_b_group,
           router_w_expert, router_b_expert, exp_w1, exp_w3, exp_w2, ln2_g, ln2_b):
    B, S, _ = x.shape
    N = B * S
    row = lambda p: p[None, :]

    w_slices, wgate, bgate, wa2 = _pack_mixer_weights(w_in, b_in, rg_w_a, rg_w_x, rg_b_a, rg_b_x, gla_w_a2)
    wr = jnp.concatenate([router_w_group.T, jnp.zeros((SUBLANES - N_GROUPS, D_MODEL), F32), router_w_expert.T],
                         axis=0).astype(BF16)
    br = jnp.concatenate([router_b_group, jnp.zeros((SUBLANES - N_GROUPS,), F32), router_b_expert])[:, None]
    x1f, xpk, info, wts, tcnt = _mixer(
        x, *w_slices, conv_w, row(conv_b), wgate, bgate, row(rg_lambda), wa2, row(gla_b_a), row(gla_norm_g),
        _odd_tiles(w_proj_rnn), _odd_tiles(w_proj_gla), _odd_tiles(w_o), row(b_o), row(ln1_g), row(ln1_b), wr, br)

    tcnt = tcnt[:, :, 0]
    nt = tcnt.shape[0]
    tot = jnp.sum(tcnt, axis=0)
    pcount = (tot + MOE_ROWS - 1) // MOE_ROWS * MOE_ROWS
    pend = jnp.cumsum(pcount)
    base = (pend - pcount)[None, :] + jnp.cumsum(tcnt, axis=0) - tcnt
    base_tok = jnp.repeat(base.T, N // nt, axis=1)
    experts_col = jnp.arange(N_EXPERTS, dtype=jnp.int32)[:, None, None]
    dest = jnp.sum(jnp.where(info[None, 0:2] == experts_col, base_tok[:, None, :], 0), axis=0) + info[2:4]
    nb = -(-(2 * N) // MOE_ROWS) + N_EXPERTS
    P = nb * MOE_ROWS
    nblk = (pend[-1] // MOE_ROWS).astype(jnp.int32)
    blk_start = jnp.minimum(jnp.arange(nb + 1, dtype=jnp.int32), nblk - 1) * MOE_ROWS
    blk_e = jnp.sum((blk_start[:, None] >= pend[None, :]).astype(jnp.int32), axis=1)
    blk_e = jnp.minimum(blk_e, N_EXPERTS - 1)

    last_blk = jnp.where(tot > 0, pend - MOE_ROWS, -1)
    tail_blk = nblk + jnp.arange(N_EXPERTS, dtype=jnp.int32)
    tail_blk = jnp.where(tail_blk < nb, tail_blk * MOE_ROWS, -1)
    xs = _dispatch(jnp.concatenate([last_blk, tail_blk]).astype(jnp.int32), dest, xpk, P)
    yt = _experts(blk_e, nblk[None], xs, exp_w1, exp_w3, exp_w2, N)
    out = _combine(x1f, wts, yt, row(ln2_g), row(ln2_b))
    return out.reshape(B, S, D_MODEL)


def kernel(x, w_in, b_in, conv_w, conv_b, rg_w_a, rg_b_a, rg_w_x, rg_b_x, rg_lambda, gla_w_a2, gla_b_a, gla_norm_g, w_proj_rnn, w_proj_gla, w_o, b_o, ln1_g, ln1_b, router_w_group, router_b_group, router_w_expert, router_b_expert, exp_w1, exp_w3, exp_w2, ln2_g, ln2_b):
    h = x
    for l in range(w_in.shape[0]):
        h = _layer(h, w_in[l], b_in[l], conv_w[l], conv_b[l], rg_w_a[l], rg_b_a[l], rg_w_x[l], rg_b_x[l],
                   rg_lambda[l], gla_w_a2[l], gla_b_a[l], gla_norm_g[l], w_proj_rnn[l], w_proj_gla[l], w_o[l],
                   b_o[l], ln1_g[l], ln1_b[l], router_w_group[l], router_b_group[l], router_w_expert[l],
                   router_b_expert[l], exp_w1[l], exp_w3[l], exp_w2[l], ln2_g[l], ln2_b[l])
    return h
```

```python
import functools

import jax
import jax.numpy as jnp
from jax import lax
from jax.experimental import pallas as pl
from jax.experimental.pallas import tpu as pltpu

F32 = jnp.float32
BF16 = jnp.bfloat16

D_MODEL = 1024
RNN_WIDTH = 1024
RNN_BLOCKS = 8
RNN_BLOCK_W = RNN_WIDTH // RNN_BLOCKS
CONV_WIDTH = 4
LRU_C = 8.0
GLA_HEADS = 4
GLA_DK = D_MODEL // 2
GLA_DV = D_MODEL
GLA_HEAD_K = GLA_DK // GLA_HEADS
GLA_HEAD_V = GLA_DV // GLA_HEADS
GLA_RANK = 16
GLA_TAU = 16.0
GLA_CHUNK = 64
N_GROUPS = 4
EXPERTS_PER_GROUP = 8
N_EXPERTS = N_GROUPS * EXPERTS_PER_GROUP
EXPERT_FF = 512
DN_ALPHA = 2.0 ** 0.25
LN_EPS = 1e-5
RMS_EPS = 1e-6

LANES = 128
SUBLANES = 8
VMEM_LIMIT = 56 * 1024 * 1024

C_RX, C_RY, C_Q, C_K, C_V, C_G = 0, 1024, 2048, 2560, 3072, 4096
C_GA, C_GB = 0, 1024
C_MAIN_END = 5120
C_GATES_START = C_MAIN_END + GLA_RANK

MIX_TILE = 256
MOE_ROWS = 256
DISPATCH_TILE = 512
COMBINE_TILE = 256
ROUTE_ROWS = 8 + N_EXPERTS
PK_TILES = D_MODEL // 2 // LANES
TOK_ROWS = PK_TILES + 1
META_EXPERT, META_TOKEN, META_VALID = 0, 1, 2


def _sigmoid(v):
    return 1.0 / (1.0 + jnp.exp(-v))


def _softplus(v):
    return jnp.maximum(v, 0.0) + jnp.log1p(jnp.exp(-jnp.abs(v)))


def _layer_norm(v, g, b):
    mu = jnp.mean(v, axis=-1, keepdims=True)
    c = v - mu
    var = jnp.mean(c * c, axis=-1, keepdims=True)
    return c * lax.rsqrt(var + LN_EPS) * g + b


def _dot(a, b):
    return jnp.dot(a, b, preferred_element_type=F32)


def _dot_nt(a, b):
    return lax.dot_general(a, b, (((1,), (1,)), ((), ())), preferred_element_type=F32)


def _dot_tn(a, b):
    return lax.dot_general(a, b, (((0,), (0,)), ((), ())), preferred_element_type=F32)


def _pack_bf16_pairs(v):
    half = v.shape[1] // 2
    bits = pltpu.bitcast(v.astype(BF16).astype(F32), jnp.uint32)
    return (bits[:, :half] >> 16) | bits[:, half:]


def _unpack_bf16_pairs(u):
    lo = pltpu.bitcast(u << 16, F32)
    hi = pltpu.bitcast(u & jnp.uint32(0xFFFF0000), F32)
    return jnp.concatenate([lo, hi], axis=1)


def _const_spec(shape):
    nd = len(shape)
    return pl.BlockSpec(shape, lambda *_: (0,) * nd, pipeline_mode=pl.Buffered(1))


def _mixer_kernel(x_ref, wmain_ref, bmain_ref, wbg_ref, bbg_ref, walr_ref, balr_ref,
                  convw_ref, convb_ref, wgate_ref, bgate_ref, lam_ref,
                  wa2_ref, ba2_ref, gnorm_ref, wprnn_ref, wpgla_ref, wo_ref, bo_ref, ln1g_ref, ln1b_ref,
                  wr_ref, br_ref,
                  x1_ref, xpk_ref, info_ref, wts_ref, cnt_ref, rxbuf, hcar, st_ref, hbuf, zbuf,
                  *, tile, steps_per_seq):
    T = tile
    step = pl.program_id(0)

    @pl.when(step == 0)
    def _():
        zbuf[...] = jnp.zeros_like(zbuf)

    @pl.when(step % steps_per_seq == 0)
    def _():
        rxbuf[0:SUBLANES, :] = jnp.zeros((SUBLANES, RNN_WIDTH), F32)
        hcar[...] = jnp.zeros_like(hcar)
        st_ref[...] = jnp.zeros_like(st_ref)

    x = x_ref[...]
    xb = x.astype(BF16)

    def proj(c0, c1, w_ref=wmain_ref, b_ref=bmain_ref):
        return _dot(xb, w_ref[:, c0:c1]) + b_ref[:, c0:c1]

    rx = proj(C_RX, C_RX + RNN_WIDTH)
    rxbuf[SUBLANES:SUBLANES + T, :] = rx
    u = convb_ref[...] + convw_ref[CONV_WIDTH - 1:CONV_WIDTH, :] * rx
    for j in range(1, CONV_WIDTH):
        u = u + convw_ref[CONV_WIDTH - 1 - j:CONV_WIDTH - j, :] * rxbuf[SUBLANES - j:SUBLANES - j + T, :]
    rxbuf[0:SUBLANES, :] = rxbuf[T:T + SUBLANES, :]

    qk = proj(C_Q, C_Q + 2 * GLA_DK)
    q = qk[:, :GLA_DK] * (GLA_HEAD_K ** -0.5)
    k = qk[:, GLA_DK:]
    alr = proj(0, LANES, walr_ref, balr_ref)

    x1 = _layer_norm(zbuf[...], ln1g_ref[...], ln1b_ref[...])
    x1_ref[...] = x1
    info, wts, total = _route_tile(x1.astype(BF16), wr_ref[...], br_ref[...])
    info_ref[...] = info
    wts_ref[...] = wts
    cnt_ref[0] = jnp.broadcast_to(total, (N_EXPERTS, LANES))
    pk = _pack_bf16_pairs(x1)
    for c in range(PK_TILES):
        xpk_ref[pl.ds(c, T, stride=TOK_ROWS), :] = pk[:, c * LANES:(c + 1) * LANES]
    first_expert = jnp.broadcast_to(info[0:1, :].astype(F32), (LANES, T)).T.astype(jnp.int32)
    token = lax.broadcasted_iota(jnp.int32, (T, LANES), 0) + jnp.maximum(step - 1, 0) * T
    lane = lax.broadcasted_iota(jnp.int32, (T, LANES), 1)
    meta = jnp.where(lane == META_EXPERT, first_expert,
                     jnp.where(lane == META_TOKEN, token, jnp.where(lane == META_VALID, 1, 0)))
    xpk_ref[pl.ds(PK_TILES, T, stride=TOK_ROWS), :] = pltpu.bitcast(meta, jnp.uint32)

    r_parts, i_parts = [], []
    for p in range(RNN_BLOCKS // 2):
        up = u[:, 256 * p:256 * (p + 1)].astype(BF16)
        gp = _dot(up, wgate_ref[p])
        r_parts.append(gp[:, :256])
        i_parts.append(gp[:, 256:])
    r = _sigmoid(jnp.concatenate(r_parts, axis=1) + bgate_ref[:, :RNN_WIDTH])
    ig = _sigmoid(jnp.concatenate(i_parts, axis=1) + bgate_ref[:, RNN_WIDTH:])
    v = proj(C_V, C_V + GLA_DV)
    ry = proj(C_RY, C_RY + RNN_WIDTH)

    z = _dot(alr.astype(BF16), wa2_ref[...]) + ba2_ref[...]
    la = -_softplus(-z) * (1.0 / GLA_TAU)
    ri = lax.broadcasted_iota(jnp.int32, (T, T), 0)
    ci = lax.broadcasted_iota(jnp.int32, (T, T), 1)
    chunk_start = (ri >> 6) << 6
    tri = jnp.where((ci <= ri) & (ci >= chunk_start), 1.0, 0.0).astype(BF16)
    la_hi = la.astype(BF16)
    la_lo = (la - la_hi.astype(F32)).astype(BF16)
    bcum = _dot(tri, la_hi) + _dot(tri, la_lo)

    g = proj(C_G, C_G + GLA_DV)

    log_a = (-LRU_C) * r * _softplus(-lam_ref[...])
    a = jnp.exp(log_a)
    m2 = -jnp.tanh(log_a) * (1.0 + a * a)
    bv = jnp.where(m2 > 0.0, m2 * lax.rsqrt(m2), 0.0) * (ig * u)

    cr = lax.broadcasted_iota(jnp.int32, (GLA_CHUNK, GLA_CHUNK), 0)
    cc = lax.broadcasted_iota(jnp.int32, (GLA_CHUNK, GLA_CHUNK), 1)
    causal = cr >= cc
    n_chunks = T // GLA_CHUNK
    heads = [(slice(hd * GLA_HEAD_K, (hd + 1) * GLA_HEAD_K), slice(hd * GLA_HEAD_V, (hd + 1) * GLA_HEAD_V))
             for hd in range(GLA_HEADS)]
    qd_c, ki_c, ke_c, vv_c, dec_c = [], [], [], [], []
    for c in range(n_chunks):
        r0 = c * GLA_CHUNK
        bc = bcum[r0:r0 + GLA_CHUNK, :]
        bl = bcum[r0 + GLA_CHUNK - 1:r0 + GLA_CHUNK, :]
        kc = k[r0:r0 + GLA_CHUNK, :]
        qd_c.append((q[r0:r0 + GLA_CHUNK, :] * jnp.exp(bc)).astype(BF16))
        ki_c.append((kc * jnp.exp(-bc)).astype(BF16))
        ke_c.append((kc * jnp.exp(bl - bc)).astype(BF16))
        vv_c.append(v[r0:r0 + GLA_CHUNK, :].astype(BF16))
        dec_c.append(jnp.exp(bl))
    scores = [[jnp.where(causal, _dot_nt(qd_c[c][:, ks], ki_c[c][:, ks]), 0.0).astype(BF16) for ks, _ in heads]
              for c in range(n_chunks)]
    intra = [[_dot(scores[c][hd], vv_c[c][:, vs]) for hd, (_, vs) in enumerate(heads)] for c in range(n_chunks)]
    incr = [[_dot_tn(vv_c[c][:, vs], ke_c[c][:, ks]) for ks, vs in heads] for c in range(n_chunks)]

    n_groups = T // SUBLANES
    sub = lax.broadcasted_iota(jnp.int32, (n_groups, SUBLANES, RNN_WIDTH), 1)
    sa = a.reshape(n_groups, SUBLANES, RNN_WIDTH)
    sb = bv.reshape(n_groups, SUBLANES, RNN_WIDTH)
    for s in (1, 2, 4):
        keep = sub >= s
        sb = sa * jnp.where(keep, pltpu.roll(sb, s, 1), 0.0) + sb
        sa = sa * jnp.where(keep, pltpu.roll(sa, s, 1), 1.0)
    carry = hcar[0:1, :]
    for gi in range(n_groups):
        hg = sb[gi] + sa[gi] * carry
        hbuf[gi * SUBLANES:(gi + 1) * SUBLANES, :] = hg
        carry = hg[SUBLANES - 1:SUBLANES, :]
    hcar[0:1, :] = carry
    h = hbuf[...]
    ga = proj(C_GA, C_GA + D_MODEL, wbg_ref, bbg_ref)

    out_a = _dot((h * jax.nn.gelu(ry)).astype(BF16), wprnn_ref[:, :D_MODEL])
    gb = proj(C_GB, C_GB + D_MODEL, wbg_ref, bbg_ref)

    states = [st_ref[hd] for hd in range(GLA_HEADS)]
    o_chunks = []
    for c in range(n_chunks):
        o_heads = []
        for hd, (ks, _) in enumerate(heads):
            o_heads.append(intra[c][hd] + _dot_nt(qd_c[c][:, ks], states[hd].astype(BF16)))
            states[hd] = states[hd] * dec_c[c][:, ks] + incr[c][hd]
        o_chunks.append(jnp.concatenate(o_heads, axis=1))
    for hd in range(GLA_HEADS):
        st_ref[hd] = states[hd]
    o_all = jnp.concatenate(o_chunks, axis=0)

    o_parts = []
    for hd in range(GLA_HEADS):
        vs = slice(hd * GLA_HEAD_V, (hd + 1) * GLA_HEAD_V)
        oh = o_all[:, vs]
        ms = jnp.mean(oh * oh, axis=-1, keepdims=True)
        o_parts.append(oh * lax.rsqrt(ms + RMS_EPS) * gnorm_ref[:, vs])
    on = jnp.concatenate(o_parts, axis=1) * (g * _sigmoid(g))
    out_b = _dot(on.astype(BF16), wpgla_ref[:, :D_MODEL])

    merged = _sigmoid(ga) * out_a + _sigmoid(gb) * out_b
    y = _dot(merged.astype(BF16), wo_ref[:, :D_MODEL]) + bo_ref[...]
    zbuf[...] = DN_ALPHA * x + y


def _mixer(x, wmain, bmain, wbg, bbg, walr, balr, conv_w, conv_b, wgate, bgate, lam, wa2, ba2, gnorm,
           wprnn, wpgla, wo, bo, ln1g, ln1b, wr, br):
    B, S, _ = x.shape
    T = min(MIX_TILE, S)
    assert S % T == 0 and T % GLA_CHUNK == 0
    N = B * S
    nt = N // T
    mix_tile = lambda i: jnp.minimum(i, nt - 1)
    tail_tile = lambda i: jnp.maximum(i - 1, 0)
    weights = (wmain, bmain, wbg, bbg, walr, balr, conv_w, conv_b, wgate, bgate, lam, wa2, ba2, gnorm,
               wprnn, wpgla, wo, bo, ln1g, ln1b, wr, br)
    return pl.pallas_call(
        functools.partial(_mixer_kernel, tile=T, steps_per_seq=S // T),
        grid=(nt + 1,),
        in_specs=[pl.BlockSpec((T, D_MODEL), lambda i: (mix_tile(i), 0))] + [_const_spec(w.shape) for w in weights],
        out_specs=[pl.BlockSpec((T, D_MODEL), lambda i: (tail_tile(i), 0)),
                   pl.BlockSpec((T * TOK_ROWS, LANES), lambda i: (tail_tile(i), 0)),
                   pl.BlockSpec((SUBLANES, T), lambda i: (0, tail_tile(i))),
                   pl.BlockSpec((SUBLANES, T), lambda i: (0, tail_tile(i))),
                   pl.BlockSpec((1, N_EXPERTS, LANES), lambda i: (tail_tile(i), 0, 0))],
        out_shape=[jax.ShapeDtypeStruct((N, D_MODEL), F32),
                   jax.ShapeDtypeStruct((N * TOK_ROWS, LANES), jnp.uint32),
                   jax.ShapeDtypeStruct((SUBLANES, N), jnp.int32),
                   jax.ShapeDtypeStruct((SUBLANES, N), F32),
                   jax.ShapeDtypeStruct((nt, N_EXPERTS, LANES), jnp.int32)],
        scratch_shapes=[pltpu.VMEM((T + SUBLANES, RNN_WIDTH), F32),
                        pltpu.VMEM((SUBLANES, RNN_WIDTH), F32),
                        pltpu.VMEM((GLA_HEADS, GLA_HEAD_V, GLA_HEAD_K), F32),
                        pltpu.VMEM((T, RNN_WIDTH), F32),
                        pltpu.VMEM((T, D_MODEL), F32)],
        compiler_params=pltpu.CompilerParams(dimension_semantics=("arbitrary",), vmem_limit_bytes=VMEM_LIMIT),
        name="mixer",
    )(x.reshape(N, D_MODEL), *weights)


def _route_tile(x1b, wr, br):
    T = x1b.shape[0]
    logits = _dot_nt(wr, x1b) + br
    row8 = lax.broadcasted_iota(jnp.int32, (SUBLANES, T), 0)
    row8f = row8.astype(F32)
    neg = jnp.float32(-jnp.inf)
    first = lambda hit: jnp.min(jnp.where(hit, row8f, float(SUBLANES)), axis=0, keepdims=True)

    gl = jnp.where(row8 < N_GROUPS, logits[0:SUBLANES, :], neg)
    gmax = jnp.max(gl, axis=0, keepdims=True)
    grp = first(gl == gmax)
    p_grp = 1.0 / jnp.sum(jnp.exp(gl - gmax), axis=0, keepdims=True)

    e_sel = jnp.zeros((EXPERTS_PER_GROUP, T), F32)
    for gi in range(N_GROUPS):
        lo = SUBLANES + gi * EXPERTS_PER_GROUP
        e_sel = jnp.where(grp == float(gi), logits[lo:lo + EXPERTS_PER_GROUP, :], e_sel)
    m1 = jnp.max(e_sel, axis=0, keepdims=True)
    i1 = first(e_sel == m1)
    e_rest = jnp.where(row8f == i1, neg, e_sel)
    m2 = jnp.max(e_rest, axis=0, keepdims=True)
    i2 = first(e_rest == m2)
    e21 = jnp.exp(m2 - m1)
    p1 = 1.0 / (1.0 + e21)
    w0 = p_grp * p1
    w1 = p_grp * (e21 * p1)
    eid0 = (grp * EXPERTS_PER_GROUP + i1).astype(jnp.int32)
    eid1 = (grp * EXPERTS_PER_GROUP + i2).astype(jnp.int32)

    erow = lax.broadcasted_iota(jnp.int32, (N_EXPERTS, T), 0)
    oh0 = jnp.where(erow == eid0, 1.0, 0.0)
    oh1 = jnp.where(erow == eid1, 1.0, 0.0)
    both = oh0 + oh1
    ti = lax.broadcasted_iota(jnp.int32, (T, T), 0)
    tj = lax.broadcasted_iota(jnp.int32, (T, T), 1)
    before = jnp.where(ti < tj, 1.0, 0.0).astype(BF16)
    prior = _dot(both.astype(BF16), before)
    rank0 = jnp.sum(prior * oh0, axis=0, keepdims=True).astype(jnp.int32)
    rank1 = jnp.sum(prior * oh1, axis=0, keepdims=True).astype(jnp.int32)
    total = (prior[:, T - 1:T] + both[:, T - 1:T]).astype(jnp.int32)

    info = jnp.where(row8 == 0, eid0, jnp.where(row8 == 1, eid1, jnp.where(row8 == 2, rank0,
                     jnp.where(row8 == 3, rank1, 0))))
    return info, jnp.where(row8 == 0, w0, jnp.where(row8 == 1, w1, 0.0)), total


def _dispatch_kernel(zero_blk_ref, dest_ref, xpk_ref, xs_ref, zbuf, sem, zsem, *, tile):
    T = tile

    @pl.when(pl.program_id(0) == 0)
    def _():
        zbuf[...] = jnp.zeros_like(zbuf)

        def zero_copy(j):
            row = pl.multiple_of(jnp.maximum(zero_blk_ref[j], 0) * TOK_ROWS, MOE_ROWS * TOK_ROWS)
            return pltpu.make_async_copy(zbuf, xs_ref.at[pl.ds(row, MOE_ROWS * TOK_ROWS)], zsem)

        for j in range(2 * N_EXPERTS):
            @pl.when(zero_blk_ref[j] >= 0)
            def _():
                zero_copy(j).start()
        for j in range(2 * N_EXPERTS):
            @pl.when(zero_blk_ref[j] >= 0)
            def _():
                zero_copy(j).wait()

    for t in range(T):
        for kk in range(2):
            row = dest_ref[kk, t] * TOK_ROWS
            pltpu.make_async_copy(xpk_ref.at[pl.ds(t * TOK_ROWS, TOK_ROWS)], xs_ref.at[pl.ds(row, TOK_ROWS)],
                                  sem).start(priority=kk)
    for kk in range(2):
        pltpu.make_async_copy(xpk_ref, xs_ref.at[pl.ds(0, T * TOK_ROWS)], sem).wait()


def _dispatch(last_blk, dest, xpk, n_rows):
    N = xpk.shape[0] // TOK_ROWS
    T = min(DISPATCH_TILE, N)
    assert N % T == 0
    grid_spec = pltpu.PrefetchScalarGridSpec(
        num_scalar_prefetch=1,
        grid=(N // T,),
        in_specs=[pl.BlockSpec((2, T), lambda i, lb: (0, i), memory_space=pltpu.SMEM),
                  pl.BlockSpec((T * TOK_ROWS, LANES), lambda i, lb: (i, 0))],
        out_specs=pl.BlockSpec(memory_space=pl.ANY),
        scratch_shapes=[pltpu.VMEM((MOE_ROWS * TOK_ROWS, LANES), jnp.uint32),
                        pltpu.SemaphoreType.DMA(()), pltpu.SemaphoreType.DMA(())],
    )
    return pl.pallas_call(
        functools.partial(_dispatch_kernel, tile=T),
        grid_spec=grid_spec,
        out_shape=jax.ShapeDtypeStruct((n_rows * TOK_ROWS, LANES), jnp.uint32),
        compiler_params=pltpu.CompilerParams(dimension_semantics=("arbitrary",), has_side_effects=True),
        name="dispatch",
    )(last_blk, dest, xpk)


def _experts_kernel(blk_e_ref, nblk_ref, run_first_ref, run_idx_ref, next_e_ref,
                    xs_ref, w1_hbm, w3_hbm, w2_hbm, yt_ref,
                    w1b, w3b, w2b, w1f, w3f, w2f, stage, ids_v, ids_s, row_sem, ids_sem, w_sem, *, n_tokens):
    i = pl.program_id(0)
    R = MOE_ROWS
    SR = R * PK_TILES
    n_used = nblk_ref[0]
    spare = 2 * n_tokens
    slot = i % 2
    prev = 1 - slot

    def ids_copy(s):
        return pltpu.make_async_copy(ids_v.at[pl.ds(s, 1)], ids_s.at[pl.ds(s, 1)], ids_sem)

    def row_copy(s, r, dst_row):
        src = pl.multiple_of((s * R + r) * PK_TILES, PK_TILES)
        dst = pl.multiple_of(dst_row * PK_TILES, PK_TILES)
        return pltpu.make_async_copy(stage.at[pl.ds(src, PK_TILES)], yt_ref.at[pl.ds(dst, PK_TILES)],
                                     row_sem.at[s]).start(priority=r % 2)

    def drain_rows(s):
        pltpu.make_async_copy(stage.at[pl.ds(0, SR)], yt_ref.at[pl.ds(0, SR)], row_sem.at[s]).wait()

    def send_rows(s):
        for r in range(R):
            row_copy(s, r, ids_s[s, r])

    @pl.when(i == 0)
    def _():
        stage[...] = jnp.zeros_like(stage)
        ids_v[...] = (spare + lax.broadcasted_iota(jnp.int32, (SUBLANES, R), 0) * R
                      + lax.broadcasted_iota(jnp.int32, (SUBLANES, R), 1))
        ids_copy(1).start()
        for r in range(R):
            row_copy(0, r, spare + r)

    def weight_copies(e, buf):
        return [pltpu.make_async_copy(src.at[e], dst.at[buf], w_sem.at[buf])
                for src, dst in ((w1_hbm, w1f), (w3_hbm, w3f), (w2_hbm, w2f))]

    @pl.when(i == 0)
    def _():
        for cp in weight_copies(blk_e_ref[0], 0):
            cp.start()

    @pl.when(run_first_ref[i] != 0)
    def _():
        buf = run_idx_ref[i] % 2
        for cp in weight_copies(blk_e_ref[i], buf):
            cp.wait()
        w1b[...] = w1f[buf].astype(BF16)
        w3b[...] = w3f[buf].astype(BF16)
        w2b[...] = w2f[buf].astype(BF16)

        @pl.when(next_e_ref[i] >= 0)
        def _():
            for cp in weight_copies(next_e_ref[i], 1 - buf):
                cp.start()

    @pl.when(i < n_used)
    def _():
        ids_copy(prev).wait()
        drain_rows(slot)
        send_rows(prev)

        meta = pltpu.bitcast(xs_ref[pl.ds(PK_TILES, R, stride=TOK_ROWS), :], jnp.int32).astype(F32).T
        first_expert = meta[META_EXPERT:META_EXPERT + 1, :]
        token = meta[META_TOKEN:META_TOKEN + 1, :]
        valid = meta[META_VALID:META_VALID + 1, :]
        second = jnp.where(first_expert != blk_e_ref[i].astype(F32), 1.0, 0.0)
        pad_row = (spare + slot * R + lax.broadcasted_iota(jnp.int32, (1, R), 1)).astype(F32)
        ids_v[pl.ds(slot, 1), :] = jnp.where(valid != 0.0, 2.0 * token + second, pad_row).astype(jnp.int32)
        ids_copy(slot).start()

        u = jnp.concatenate([xs_ref[pl.ds(c, R, stride=TOK_ROWS), :] for c in range(PK_TILES)], axis=1)
        xb = _unpack_bf16_pairs(u).astype(BF16)
        h1 = _dot(xb, w1b[...])
        h3 = _dot(xb, w3b[...])
        hact = (h1 * _sigmoid(h1)) * h3
        y = _dot(hact.astype(BF16), w2b[...])
        pk = _pack_bf16_pairs(y)
        for c in range(PK_TILES):
            stage[pl.ds(slot * SR + c, R, stride=PK_TILES), :] = pk[:, c * LANES:(c + 1) * LANES]

    @pl.when(i == n_used)
    def _():
        ids_copy(prev).wait()
        send_rows(prev)
        drain_rows(slot)
        drain_rows(prev)


def _experts(blk_e, nblk, run_first, run_idx, next_e, xs, w1, w3, w2, n_tokens):
    P = xs.shape[0] // TOK_ROWS
    nb = P // MOE_ROWS
    assert blk_e.shape[0] == nb + 1
    grid_spec = pltpu.PrefetchScalarGridSpec(
        num_scalar_prefetch=5,
        grid=(nb + 1,),
        in_specs=[pl.BlockSpec((MOE_ROWS * TOK_ROWS, LANES), lambda i, be, n, *_: (jnp.minimum(i, n[0] - 1), 0)),
                  pl.BlockSpec(memory_space=pl.ANY), pl.BlockSpec(memory_space=pl.ANY),
                  pl.BlockSpec(memory_space=pl.ANY)],
        out_specs=pl.BlockSpec(memory_space=pl.ANY),
        scratch_shapes=[pltpu.VMEM((D_MODEL, EXPERT_FF), BF16), pltpu.VMEM((D_MODEL, EXPERT_FF), BF16),
                        pltpu.VMEM((EXPERT_FF, D_MODEL), BF16),
                        pltpu.VMEM((2, D_MODEL, EXPERT_FF), F32), pltpu.VMEM((2, D_MODEL, EXPERT_FF), F32),
                        pltpu.VMEM((2, EXPERT_FF, D_MODEL), F32),
                        pltpu.VMEM((2 * MOE_ROWS * PK_TILES, LANES), jnp.uint32),
                        pltpu.VMEM((SUBLANES, MOE_ROWS), jnp.int32),
                        pltpu.SMEM((2, MOE_ROWS), jnp.int32),
                        pltpu.SemaphoreType.DMA((2,)), pltpu.SemaphoreType.DMA(()),
                        pltpu.SemaphoreType.DMA((2,))],
    )
    return pl.pallas_call(
        functools.partial(_experts_kernel, n_tokens=n_tokens),
        grid_spec=grid_spec,
        out_shape=jax.ShapeDtypeStruct(((2 * n_tokens + 2 * MOE_ROWS) * PK_TILES, LANES), jnp.uint32),
        compiler_params=pltpu.CompilerParams(dimension_semantics=("arbitrary",), vmem_limit_bytes=VMEM_LIMIT,
                                             has_side_effects=True),
        name="experts",
    )(blk_e, nblk, run_first, run_idx, next_e, xs, w1, w3, w2)


def _combine_kernel(x1_ref, wts_ref, yt_ref, g_ref, b_ref, out_ref, *, tile):
    T = tile
    wpad = jnp.concatenate([wts_ref[...], jnp.zeros((LANES - SUBLANES, T), F32)], axis=0)
    wt = wpad.T
    slot = lambda s: _unpack_bf16_pairs(jnp.concatenate(
        [yt_ref[pl.ds(s * PK_TILES + c, T, stride=2 * PK_TILES), :] for c in range(PK_TILES)], axis=1))
    y = wt[:, 0:1] * slot(0) + wt[:, 1:2] * slot(1)
    out_ref[...] = _layer_norm(DN_ALPHA * x1_ref[...] + y, g_ref[...], b_ref[...])


def _combine(x1f, wts, yt, g, b):
    N = x1f.shape[0]
    T = min(COMBINE_TILE, N)
    assert N % T == 0
    return pl.pallas_call(
        functools.partial(_combine_kernel, tile=T),
        grid=(N // T,),
        in_specs=[pl.BlockSpec((T, D_MODEL), lambda i: (i, 0)),
                  pl.BlockSpec((SUBLANES, T), lambda i: (0, i)),
                  pl.BlockSpec((T * 2 * PK_TILES, LANES), lambda i: (i, 0)),
                  _const_spec(g.shape), _const_spec(b.shape)],
        out_specs=pl.BlockSpec((T, D_MODEL), lambda i: (i, 0)),
        out_shape=jax.ShapeDtypeStruct((N, D_MODEL), F32),
        compiler_params=pltpu.CompilerParams(dimension_semantics=("arbitrary",)),
        name="combine",
    )(x1f, wts, yt, g, b)


def _odd_tiles(w):
    return jnp.pad(w, ((0, 0), (0, LANES))).astype(BF16)


def _pack_mixer_weights(w_in, b_in, rg_w_a, rg_w_x, rg_b_a, rg_b_x, gla_w_a2):
    wmain = _odd_tiles(w_in[:, :C_MAIN_END])
    wbg = _odd_tiles(w_in[:, C_GATES_START:])
    walr = jnp.pad(w_in[:, C_MAIN_END:C_GATES_START], ((0, 0), (0, LANES - GLA_RANK))).astype(BF16)
    bmain = b_in[None, :C_MAIN_END]
    bbg = b_in[None, C_GATES_START:]
    balr = jnp.pad(b_in[None, C_MAIN_END:C_GATES_START], ((0, 0), (0, LANES - GLA_RANK)))
    zero = jnp.zeros((RNN_BLOCK_W, RNN_BLOCK_W), w_in.dtype)
    tiles = []
    for p in range(RNN_BLOCKS // 2):
        top = jnp.concatenate([rg_w_a[2 * p], zero, rg_w_x[2 * p], zero], axis=1)
        bot = jnp.concatenate([zero, rg_w_a[2 * p + 1], zero, rg_w_x[2 * p + 1]], axis=1)
        tiles.append(jnp.concatenate([top, bot], axis=0))
    wgate = jnp.stack(tiles).astype(BF16)
    bgate = jnp.concatenate([rg_b_a, rg_b_x])[None, :]
    wa2 = jnp.concatenate([gla_w_a2, jnp.zeros((LANES - GLA_RANK, GLA_DK), gla_w_a2.dtype)], axis=0).astype(BF16)
    return (wmain, bmain, wbg, bbg, walr, balr), wgate, bgate, wa2


def _layer(x, w_in, b_in, conv_w, conv_b, rg_w_a, rg_b_a, rg_w_x, rg_b_x, rg_lambda, gla_w_a2, gla_b_a,
           gla_norm_g, w_proj_rnn, w_proj_gla, w_o, b_o, ln1_g, ln1_b, router_w_group, router_b_group,
           router_w_expert, router_b_expert, exp_w1, exp_w3, exp_w2, ln2_g, ln2_b):
    B, S, _ = x.shape
    N = B * S
    row = lambda p: p[None, :]

    w_slices, wgate, bgate, wa2 = _pack_mixer_weights(w_in, b_in, rg_w_a, rg_w_x, rg_b_a, rg_b_x, gla_w_a2)
    wr = jnp.concatenate([router_w_group.T, jnp.zeros((SUBLANES - N_GROUPS, D_MODEL), F32), router_w_expert.T],
                         axis=0).astype(BF16)
    br = jnp.concatenate([router_b_group, jnp.zeros((SUBLANES - N_GROUPS,), F32), router_b_expert])[:, None]
    x1f, xpk, info, wts, tcnt = _mixer(
        x, *w_slices, conv_w, row(conv_b), wgate, bgate, row(rg_lambda), wa2, row(gla_b_a), row(gla_norm_g),
        _odd_tiles(w_proj_rnn), _odd_tiles(w_proj_gla), _odd_tiles(w_o), row(b_o), row(ln1_g), row(ln1_b), wr, br)

    tcnt = tcnt[:, :, 0]
    nt = tcnt.shape[0]
    tot = jnp.sum(tcnt, axis=0)
    pcount = (tot + MOE_ROWS - 1) // MOE_ROWS * MOE_ROWS
    pend = jnp.cumsum(pcount)
    base = (pend - pcount)[None, :] + jnp.cumsum(tcnt, axis=0) - tcnt
    base_tok = jnp.repeat(base.T, N // nt, axis=1)
    experts_col = jnp.arange(N_EXPERTS, dtype=jnp.int32)[:, None, None]
    dest = jnp.sum(jnp.where(info[None, 0:2] == experts_col, base_tok[:, None, :], 0), axis=0) + info[2:4]
    nb = -(-(2 * N) // MOE_ROWS) + N_EXPERTS
    P = nb * MOE_ROWS
    nblk = (pend[-1] // MOE_ROWS).astype(jnp.int32)
    blk_start = jnp.minimum(jnp.arange(nb + 1, dtype=jnp.int32), nblk - 1) * MOE_ROWS
    blk_e = jnp.sum((blk_start[:, None] >= pend[None, :]).astype(jnp.int32), axis=1)
    blk_e = jnp.minimum(blk_e, N_EXPERTS - 1)

    last_blk = jnp.where(tot > 0, pend - MOE_ROWS, -1)
    tail_blk = nblk + jnp.arange(N_EXPERTS, dtype=jnp.int32)
    tail_blk = jnp.where(tail_blk < nb, tail_blk * MOE_ROWS, -1)
    xs = _dispatch(jnp.concatenate([last_blk, tail_blk]).astype(jnp.int32), dest, xpk, P)
    blk_i = jnp.arange(nb + 1, dtype=jnp.int32)
    run_first = ((blk_i < nblk) & ((blk_i == 0) | (blk_e != jnp.roll(blk_e, 1)))).astype(jnp.int32)
    run_idx = jnp.cumsum(run_first) - 1
    later_nonempty = jnp.where(tot > 0, jnp.arange(N_EXPERTS, dtype=jnp.int32), N_EXPERTS)
    next_nonempty = lax.cummin(jnp.concatenate([later_nonempty[1:], jnp.full((1,), N_EXPERTS, jnp.int32)]),
                               reverse=True)
    next_e = jnp.where(next_nonempty < N_EXPERTS, next_nonempty, -1)[blk_e]
    yt = _experts(blk_e, nblk[None], run_first, run_idx.astype(jnp.int32), next_e.astype(jnp.int32),
                  xs, exp_w1, exp_w3, exp_w2, N)
    out = _combine(x1f, wts, yt, row(ln2_g), row(ln2_b))
    return out.reshape(B, S, D_MODEL)


def kernel(x, w_in, b_in, conv_w, conv_b, rg_w_a, rg_b_a, rg_w_x, rg_b_x, rg_lambda, gla_w_a2, gla_b_a, gla_norm_g, w_proj_rnn, w_proj_gla, w_o, b_o, ln1_g, ln1_b, router_w_group, router_b_group, router_w_expert, router_b_expert, exp_w1, exp_w3, exp_w2, ln2_g, ln2_b):
    h = x
    for l in range(w_in.shape[0]):
        h = _layer(h, w_in[l], b_in[l], conv_w[l], conv_b[l], rg_w_a[l], rg_b_a[l], rg_w_x[l], rg_b_x[l],
                   rg_lambda[l], gla_w_a2[l], gla_b_a[l], gla_norm_g[l], w_proj_rnn[l], w_proj_gla[l], w_o[l],
                   b_o[l], ln1_g[l], ln1_b[l], router_w_group[l], router_b_group[l], router_w_expert[l],
                   router_b_expert[l], exp_w1[l], exp_w3[l], exp_w2[l], ln2_g[l], ln2_b[l])
    return h
```

```python
import functools

import jax
import jax.numpy as jnp
from jax import lax
from jax.experimental import pallas as pl
from jax.experimental.pallas import tpu as pltpu

F32 = jnp.float32
BF16 = jnp.bfloat16

D_MODEL = 1024
RNN_WIDTH = 1024
RNN_BLOCKS = 8
RNN_BLOCK_W = RNN_WIDTH // RNN_BLOCKS
CONV_WIDTH = 4
LRU_C = 8.0
GLA_HEADS = 4
GLA_DK = D_MODEL // 2
GLA_DV = D_MODEL
GLA_HEAD_K = GLA_DK // GLA_HEADS
GLA_HEAD_V = GLA_DV // GLA_HEADS
GLA_RANK = 16
GLA_TAU = 16.0
GLA_CHUNK = 64
N_GROUPS = 4
EXPERTS_PER_GROUP = 8
N_EXPERTS = N_GROUPS * EXPERTS_PER_GROUP
EXPERT_FF = 512
DN_ALPHA = 2.0 ** 0.25
LN_EPS = 1e-5
RMS_EPS = 1e-6

LANES = 128
SUBLANES = 8
VMEM_LIMIT = 56 * 1024 * 1024

C_RX, C_RY, C_Q, C_K, C_V, C_G = 0, 1024, 2048, 2560, 3072, 4096
C_GA, C_GB = 0, 1024
C_MAIN_END = 5120
C_GATES_START = C_MAIN_END + GLA_RANK

MIX_TILE = 256
MOE_ROWS = 256
DISPATCH_TILE = 512
COMBINE_TILE = 256
N_STAGE = 3
ROUTE_ROWS = 8 + N_EXPERTS
PK_TILES = D_MODEL // 2 // LANES
TOK_ROWS = PK_TILES + 1
META_EXPERT, META_TOKEN, META_VALID = 0, 1, 2


def _sigmoid(v):
    return 1.0 / (1.0 + jnp.exp(-v))


def _softplus(v):
    return jnp.maximum(v, 0.0) + jnp.log1p(jnp.exp(-jnp.abs(v)))


def _layer_norm(v, g, b):
    mu = jnp.mean(v, axis=-1, keepdims=True)
    c = v - mu
    var = jnp.mean(c * c, axis=-1, keepdims=True)
    return c * lax.rsqrt(var + LN_EPS) * g + b


def _dot(a, b):
    return jnp.dot(a, b, preferred_element_type=F32)


def _dot_nt(a, b):
    return lax.dot_general(a, b, (((1,), (1,)), ((), ())), preferred_element_type=F32)


def _dot_tn(a, b):
    return lax.dot_general(a, b, (((0,), (0,)), ((), ())), preferred_element_type=F32)


def _pack_bf16_pairs(v):
    half = v.shape[1] // 2
    bits = pltpu.bitcast(v.astype(BF16).astype(F32), jnp.uint32)
    return (bits[:, :half] >> 16) | bits[:, half:]


def _unpack_bf16_pairs(u):
    lo = pltpu.bitcast(u << 16, F32)
    hi = pltpu.bitcast(u & jnp.uint32(0xFFFF0000), F32)
    return jnp.concatenate([lo, hi], axis=1)


def _const_spec(shape):
    nd = len(shape)
    return pl.BlockSpec(shape, lambda *_: (0,) * nd, pipeline_mode=pl.Buffered(1))


def _mixer_kernel(x_ref, wmain_ref, bmain_ref, wbg_ref, bbg_ref, walr_ref, balr_ref,
                  convw_ref, convb_ref, wgate_ref, bgate_ref, lam_ref,
                  wa2_ref, ba2_ref, gnorm_ref, wprnn_ref, wpgla_ref, wo_ref, bo_ref, ln1g_ref, ln1b_ref,
                  wr_ref, br_ref,
                  x1_ref, xpk_ref, info_ref, wts_ref, cnt_ref, rxbuf, hcar, st_ref, hbuf, zbuf,
                  *, tile, steps_per_seq):
    T = tile
    step = pl.program_id(0)

    @pl.when(step == 0)
    def _():
        zbuf[...] = jnp.zeros_like(zbuf)

    @pl.when(step % steps_per_seq == 0)
    def _():
        rxbuf[0:SUBLANES, :] = jnp.zeros((SUBLANES, RNN_WIDTH), F32)
        hcar[...] = jnp.zeros_like(hcar)
        st_ref[...] = jnp.zeros_like(st_ref)

    x = x_ref[...]
    xb = x.astype(BF16)

    def proj(c0, c1, w_ref=wmain_ref, b_ref=bmain_ref):
        return _dot(xb, w_ref[:, c0:c1]) + b_ref[:, c0:c1]

    rx = proj(C_RX, C_RX + RNN_WIDTH)
    rxbuf[SUBLANES:SUBLANES + T, :] = rx
    u = convb_ref[...] + convw_ref[CONV_WIDTH - 1:CONV_WIDTH, :] * rx
    for j in range(1, CONV_WIDTH):
        u = u + convw_ref[CONV_WIDTH - 1 - j:CONV_WIDTH - j, :] * rxbuf[SUBLANES - j:SUBLANES - j + T, :]
    rxbuf[0:SUBLANES, :] = rxbuf[T:T + SUBLANES, :]

    qk = proj(C_Q, C_Q + 2 * GLA_DK)
    q = qk[:, :GLA_DK] * (GLA_HEAD_K ** -0.5)
    k = qk[:, GLA_DK:]
    alr = proj(0, LANES, walr_ref, balr_ref)

    x1 = _layer_norm(zbuf[...], ln1g_ref[...], ln1b_ref[...])
    x1_ref[...] = x1
    info, wts, total = _route_tile(x1.astype(BF16), wr_ref[...], br_ref[...])
    info_ref[...] = info
    wts_ref[...] = wts
    cnt_ref[0] = jnp.broadcast_to(total, (N_EXPERTS, LANES))
    pk = _pack_bf16_pairs(x1)
    for c in range(PK_TILES):
        xpk_ref[pl.ds(c, T, stride=TOK_ROWS), :] = pk[:, c * LANES:(c + 1) * LANES]
    first_expert = jnp.broadcast_to(info[0:1, :].astype(F32), (LANES, T)).T.astype(jnp.int32)
    token = lax.broadcasted_iota(jnp.int32, (T, LANES), 0) + jnp.maximum(step - 1, 0) * T
    lane = lax.broadcasted_iota(jnp.int32, (T, LANES), 1)
    meta = jnp.where(lane == META_EXPERT, first_expert,
                     jnp.where(lane == META_TOKEN, token, jnp.where(lane == META_VALID, 1, 0)))
    xpk_ref[pl.ds(PK_TILES, T, stride=TOK_ROWS), :] = pltpu.bitcast(meta, jnp.uint32)

    r_parts, i_parts = [], []
    for p in range(RNN_BLOCKS // 2):
        up = u[:, 256 * p:256 * (p + 1)].astype(BF16)
        gp = _dot(up, wgate_ref[p])
        r_parts.append(gp[:, :256])
        i_parts.append(gp[:, 256:])
    r = _sigmoid(jnp.concatenate(r_parts, axis=1) + bgate_ref[:, :RNN_WIDTH])
    ig = _sigmoid(jnp.concatenate(i_parts, axis=1) + bgate_ref[:, RNN_WIDTH:])
    v = proj(C_V, C_V + GLA_DV)
    ry = proj(C_RY, C_RY + RNN_WIDTH)

    z = _dot(alr.astype(BF16), wa2_ref[...]) + ba2_ref[...]
    la = -_softplus(-z) * (1.0 / GLA_TAU)
    ri = lax.broadcasted_iota(jnp.int32, (T, T), 0)
    ci = lax.broadcasted_iota(jnp.int32, (T, T), 1)
    chunk_start = (ri >> 6) << 6
    tri = jnp.where((ci <= ri) & (ci >= chunk_start), 1.0, 0.0).astype(BF16)
    la_hi = la.astype(BF16)
    la_lo = (la - la_hi.astype(F32)).astype(BF16)
    bcum = _dot(tri, la_hi) + _dot(tri, la_lo)

    g = proj(C_G, C_G + GLA_DV)

    log_a = (-LRU_C) * r * _softplus(-lam_ref[...])
    a = jnp.exp(log_a)
    m2 = -jnp.tanh(log_a) * (1.0 + a * a)
    bv = jnp.where(m2 > 0.0, m2 * lax.rsqrt(m2), 0.0) * (ig * u)

    cr = lax.broadcasted_iota(jnp.int32, (GLA_CHUNK, GLA_CHUNK), 0)
    cc = lax.broadcasted_iota(jnp.int32, (GLA_CHUNK, GLA_CHUNK), 1)
    causal = cr >= cc
    n_chunks = T // GLA_CHUNK
    heads = [(slice(hd * GLA_HEAD_K, (hd + 1) * GLA_HEAD_K), slice(hd * GLA_HEAD_V, (hd + 1) * GLA_HEAD_V))
             for hd in range(GLA_HEADS)]
    qd_c, ki_c, ke_c, vv_c, dec_c = [], [], [], [], []
    for c in range(n_chunks):
        r0 = c * GLA_CHUNK
        bc = bcum[r0:r0 + GLA_CHUNK, :]
        bl = bcum[r0 + GLA_CHUNK - 1:r0 + GLA_CHUNK, :]
        kc = k[r0:r0 + GLA_CHUNK, :]
        qd_c.append((q[r0:r0 + GLA_CHUNK, :] * jnp.exp(bc)).astype(BF16))
        ki_c.append((kc * jnp.exp(-bc)).astype(BF16))
        ke_c.append((kc * jnp.exp(bl - bc)).astype(BF16))
        vv_c.append(v[r0:r0 + GLA_CHUNK, :].astype(BF16))
        dec_c.append(jnp.exp(bl))
    scores = [[jnp.where(causal, _dot_nt(qd_c[c][:, ks], ki_c[c][:, ks]), 0.0).astype(BF16) for ks, _ in heads]
              for c in range(n_chunks)]
    intra = [[_dot(scores[c][hd], vv_c[c][:, vs]) for hd, (_, vs) in enumerate(heads)] for c in range(n_chunks)]
    incr = [[_dot_tn(vv_c[c][:, vs], ke_c[c][:, ks]) for ks, vs in heads] for c in range(n_chunks)]

    n_groups = T // SUBLANES
    sub = lax.broadcasted_iota(jnp.int32, (n_groups, SUBLANES, RNN_WIDTH), 1)
    sa = a.reshape(n_groups, SUBLANES, RNN_WIDTH)
    sb = bv.reshape(n_groups, SUBLANES, RNN_WIDTH)
    for s in (1, 2, 4):
        keep = sub >= s
        sb = sa * jnp.where(keep, pltpu.roll(sb, s, 1), 0.0) + sb
        sa = sa * jnp.where(keep, pltpu.roll(sa, s, 1), 1.0)
    carry = hcar[0:1, :]
    for gi in range(n_groups):
        hg = sb[gi] + sa[gi] * carry
        hbuf[gi * SUBLANES:(gi + 1) * SUBLANES, :] = hg
        carry = hg[SUBLANES - 1:SUBLANES, :]
    hcar[0:1, :] = carry
    h = hbuf[...]
    ga = proj(C_GA, C_GA + D_MODEL, wbg_ref, bbg_ref)

    out_a = _dot((h * jax.nn.gelu(ry)).astype(BF16), wprnn_ref[:, :D_MODEL])
    gb = proj(C_GB, C_GB + D_MODEL, wbg_ref, bbg_ref)

    states = [st_ref[hd] for hd in range(GLA_HEADS)]
    o_chunks = []
    for c in range(n_chunks):
        o_heads = []
        for hd, (ks, _) in enumerate(heads):
            o_heads.append(intra[c][hd] + _dot_nt(qd_c[c][:, ks], states[hd].astype(BF16)))
            states[hd] = states[hd] * dec_c[c][:, ks] + incr[c][hd]
        o_chunks.append(jnp.concatenate(o_heads, axis=1))
    for hd in range(GLA_HEADS):
        st_ref[hd] = states[hd]
    o_all = jnp.concatenate(o_chunks, axis=0)

    o_parts = []
    for hd in range(GLA_HEADS):
        vs = slice(hd * GLA_HEAD_V, (hd + 1) * GLA_HEAD_V)
        oh = o_all[:, vs]
        ms = jnp.mean(oh * oh, axis=-1, keepdims=True)
        o_parts.append(oh * lax.rsqrt(ms + RMS_EPS) * gnorm_ref[:, vs])
    on = jnp.concatenate(o_parts, axis=1) * (g * _sigmoid(g))
    out_b = _dot(on.astype(BF16), wpgla_ref[:, :D_MODEL])

    merged = _sigmoid(ga) * out_a + _sigmoid(gb) * out_b
    y = _dot(merged.astype(BF16), wo_ref[:, :D_MODEL]) + bo_ref[...]
    zbuf[...] = DN_ALPHA * x + y


def _mixer(x, wmain, bmain, wbg, bbg, walr, balr, conv_w, conv_b, wgate, bgate, lam, wa2, ba2, gnorm,
           wprnn, wpgla, wo, bo, ln1g, ln1b, wr, br):
    B, S, _ = x.shape
    T = min(MIX_TILE, S)
    assert S % T == 0 and T % GLA_CHUNK == 0
    N = B * S
    nt = N // T
    mix_tile = lambda i: jnp.minimum(i, nt - 1)
    tail_tile = lambda i: jnp.maximum(i - 1, 0)
    weights = (wmain, bmain, wbg, bbg, walr, balr, conv_w, conv_b, wgate, bgate, lam, wa2, ba2, gnorm,
               wprnn, wpgla, wo, bo, ln1g, ln1b, wr, br)
    return pl.pallas_call(
        functools.partial(_mixer_kernel, tile=T, steps_per_seq=S // T),
        grid=(nt + 1,),
        in_specs=[pl.BlockSpec((T, D_MODEL), lambda i: (mix_tile(i), 0))] + [_const_spec(w.shape) for w in weights],
        out_specs=[pl.BlockSpec((T, D_MODEL), lambda i: (tail_tile(i), 0)),
                   pl.BlockSpec((T * TOK_ROWS, LANES), lambda i: (tail_tile(i), 0)),
                   pl.BlockSpec((SUBLANES, T), lambda i: (0, tail_tile(i))),
                   pl.BlockSpec((SUBLANES, T), lambda i: (0, tail_tile(i))),
                   pl.BlockSpec((1, N_EXPERTS, LANES), lambda i: (tail_tile(i), 0, 0))],
        out_shape=[jax.ShapeDtypeStruct((N, D_MODEL), F32),
                   jax.ShapeDtypeStruct((N * TOK_ROWS, LANES), jnp.uint32),
                   jax.ShapeDtypeStruct((SUBLANES, N), jnp.int32),
                   jax.ShapeDtypeStruct((SUBLANES, N), F32),
                   jax.ShapeDtypeStruct((nt, N_EXPERTS, LANES), jnp.int32)],
        scratch_shapes=[pltpu.VMEM((T + SUBLANES, RNN_WIDTH), F32),
                        pltpu.VMEM((SUBLANES, RNN_WIDTH), F32),
                        pltpu.VMEM((GLA_HEADS, GLA_HEAD_V, GLA_HEAD_K), F32),
                        pltpu.VMEM((T, RNN_WIDTH), F32),
                        pltpu.VMEM((T, D_MODEL), F32)],
        compiler_params=pltpu.CompilerParams(dimension_semantics=("arbitrary",), vmem_limit_bytes=VMEM_LIMIT),
        name="mixer",
    )(x.reshape(N, D_MODEL), *weights)


def _route_tile(x1b, wr, br):
    T = x1b.shape[0]
    logits = _dot_nt(wr, x1b) + br
    row8 = lax.broadcasted_iota(jnp.int32, (SUBLANES, T), 0)
    row8f = row8.astype(F32)
    neg = jnp.float32(-jnp.inf)
    first = lambda hit: jnp.min(jnp.where(hit, row8f, float(SUBLANES)), axis=0, keepdims=True)

    gl = jnp.where(row8 < N_GROUPS, logits[0:SUBLANES, :], neg)
    gmax = jnp.max(gl, axis=0, keepdims=True)
    grp = first(gl == gmax)
    p_grp = 1.0 / jnp.sum(jnp.exp(gl - gmax), axis=0, keepdims=True)

    e_sel = jnp.zeros((EXPERTS_PER_GROUP, T), F32)
    for gi in range(N_GROUPS):
        lo = SUBLANES + gi * EXPERTS_PER_GROUP
        e_sel = jnp.where(grp == float(gi), logits[lo:lo + EXPERTS_PER_GROUP, :], e_sel)
    m1 = jnp.max(e_sel, axis=0, keepdims=True)
    i1 = first(e_sel == m1)
    e_rest = jnp.where(row8f == i1, neg, e_sel)
    m2 = jnp.max(e_rest, axis=0, keepdims=True)
    i2 = first(e_rest == m2)
    e21 = jnp.exp(m2 - m1)
    p1 = 1.0 / (1.0 + e21)
    w0 = p_grp * p1
    w1 = p_grp * (e21 * p1)
    eid0 = (grp * EXPERTS_PER_GROUP + i1).astype(jnp.int32)
    eid1 = (grp * EXPERTS_PER_GROUP + i2).astype(jnp.int32)

    erow = lax.broadcasted_iota(jnp.int32, (N_EXPERTS, T), 0)
    oh0 = jnp.where(erow == eid0, 1.0, 0.0)
    oh1 = jnp.where(erow == eid1, 1.0, 0.0)
    both = oh0 + oh1
    ti = lax.broadcasted_iota(jnp.int32, (T, T), 0)
    tj = lax.broadcasted_iota(jnp.int32, (T, T), 1)
    before = jnp.where(ti < tj, 1.0, 0.0).astype(BF16)
    prior = _dot(both.astype(BF16), before)
    rank0 = jnp.sum(prior * oh0, axis=0, keepdims=True).astype(jnp.int32)
    rank1 = jnp.sum(prior * oh1, axis=0, keepdims=True).astype(jnp.int32)
    total = (prior[:, T - 1:T] + both[:, T - 1:T]).astype(jnp.int32)

    info = jnp.where(row8 == 0, eid0, jnp.where(row8 == 1, eid1, jnp.where(row8 == 2, rank0,
                     jnp.where(row8 == 3, rank1, 0))))
    return info, jnp.where(row8 == 0, w0, jnp.where(row8 == 1, w1, 0.0)), total


def _dispatch_kernel(zero_blk_ref, dest_ref, xpk_ref, xs_ref, zbuf, sem, zsem, *, tile):
    T = tile

    @pl.when(pl.program_id(0) == 0)
    def _():
        zbuf[...] = jnp.zeros_like(zbuf)

        def zero_copy(j):
            row = pl.multiple_of(jnp.maximum(zero_blk_ref[j], 0) * TOK_ROWS, MOE_ROWS * TOK_ROWS)
            return pltpu.make_async_copy(zbuf, xs_ref.at[pl.ds(row, MOE_ROWS * TOK_ROWS)], zsem)

        for j in range(2 * N_EXPERTS):
            @pl.when(zero_blk_ref[j] >= 0)
            def _():
                zero_copy(j).start()
        for j in range(2 * N_EXPERTS):
            @pl.when(zero_blk_ref[j] >= 0)
            def _():
                zero_copy(j).wait()

    for t in range(T):
        for kk in range(2):
            row = dest_ref[kk, t] * TOK_ROWS
            pltpu.make_async_copy(xpk_ref.at[pl.ds(t * TOK_ROWS, TOK_ROWS)], xs_ref.at[pl.ds(row, TOK_ROWS)],
                                  sem).start(priority=kk)
    for kk in range(2):
        pltpu.make_async_copy(xpk_ref, xs_ref.at[pl.ds(0, T * TOK_ROWS)], sem).wait()


def _dispatch(last_blk, dest, xpk, n_rows):
    N = xpk.shape[0] // TOK_ROWS
    T = min(DISPATCH_TILE, N)
    assert N % T == 0
    grid_spec = pltpu.PrefetchScalarGridSpec(
        num_scalar_prefetch=1,
        grid=(N // T,),
        in_specs=[pl.BlockSpec((2, T), lambda i, lb: (0, i), memory_space=pltpu.SMEM),
                  pl.BlockSpec((T * TOK_ROWS, LANES), lambda i, lb: (i, 0))],
        out_specs=pl.BlockSpec(memory_space=pl.ANY),
        scratch_shapes=[pltpu.VMEM((MOE_ROWS * TOK_ROWS, LANES), jnp.uint32),
                        pltpu.SemaphoreType.DMA(()), pltpu.SemaphoreType.DMA(())],
    )
    return pl.pallas_call(
        functools.partial(_dispatch_kernel, tile=T),
        grid_spec=grid_spec,
        out_shape=jax.ShapeDtypeStruct((n_rows * TOK_ROWS, LANES), jnp.uint32),
        compiler_params=pltpu.CompilerParams(dimension_semantics=("arbitrary",), has_side_effects=True),
        name="dispatch",
    )(last_blk, dest, xpk)


def _experts_kernel(blk_e_ref, nblk_ref, run_first_ref, run_idx_ref, next_e_ref,
                    xs_ref, w1_hbm, w3_hbm, w2_hbm, yt_ref,
                    w1b, w3b, w2b, w1f, w3f, w2f, stage, ids_v, ids_s, row_sem, ids_sem, w_sem, *, n_tokens):
    i = pl.program_id(0)
    R = MOE_ROWS
    SR = R * PK_TILES
    n_used = nblk_ref[0]
    spare = 2 * n_tokens
    slot = i % N_STAGE
    prev = (i + N_STAGE - 1) % N_STAGE
    prev2 = (i + N_STAGE - 2) % N_STAGE

    def ids_copy(s):
        return pltpu.make_async_copy(ids_v.at[pl.ds(s, 1)], ids_s.at[pl.ds(s, 1)], ids_sem)

    def row_copy(s, r, dst_row):
        src = pl.multiple_of((s * R + r) * PK_TILES, PK_TILES)
        dst = pl.multiple_of(dst_row * PK_TILES, PK_TILES)
        return pltpu.make_async_copy(stage.at[pl.ds(src, PK_TILES)], yt_ref.at[pl.ds(dst, PK_TILES)],
                                     row_sem.at[s]).start(priority=r % 2)

    def drain_rows(s):
        pltpu.make_async_copy(stage.at[pl.ds(0, SR)], yt_ref.at[pl.ds(0, SR)], row_sem.at[s]).wait()

    def send_rows(s):
        for r in range(R):
            row_copy(s, r, ids_s[s, r])

    @pl.when(i == 0)
    def _():
        stage[...] = jnp.zeros_like(stage)
        ids_v[...] = (spare + lax.broadcasted_iota(jnp.int32, (SUBLANES, R), 0) * R
                      + lax.broadcasted_iota(jnp.int32, (SUBLANES, R), 1))
        ids_copy(N_STAGE - 1).start()
        for s in range(N_STAGE - 1):
            for r in range(R):
                row_copy(s, r, spare + s * R + r)

    def weight_copies(e, buf):
        return [pltpu.make_async_copy(src.at[e], dst.at[buf], w_sem.at[buf])
                for src, dst in ((w1_hbm, w1f), (w3_hbm, w3f), (w2_hbm, w2f))]

    @pl.when(i == 0)
    def _():
        for cp in weight_copies(blk_e_ref[0], 0):
            cp.start()

    @pl.when(run_first_ref[i] != 0)
    def _():
        buf = run_idx_ref[i] % 2
        for cp in weight_copies(blk_e_ref[i], buf):
            cp.wait()
        w1b[...] = w1f[buf].astype(BF16)
        w3b[...] = w3f[buf].astype(BF16)
        w2b[...] = w2f[buf].astype(BF16)

        @pl.when(next_e_ref[i] >= 0)
        def _():
            for cp in weight_copies(next_e_ref[i], 1 - buf):
                cp.start()

    @pl.when(i < n_used)
    def _():
        ids_copy(prev).wait()
        drain_rows(slot)
        send_rows(prev)

        meta = pltpu.bitcast(xs_ref[pl.ds(PK_TILES, R, stride=TOK_ROWS), :], jnp.int32).astype(F32).T
        first_expert = meta[META_EXPERT:META_EXPERT + 1, :]
        token = meta[META_TOKEN:META_TOKEN + 1, :]
        valid = meta[META_VALID:META_VALID + 1, :]
        second = jnp.where(first_expert != blk_e_ref[i].astype(F32), 1.0, 0.0)
        pad_row = (spare + slot * R + lax.broadcasted_iota(jnp.int32, (1, R), 1)).astype(F32)
        ids_v[pl.ds(slot, 1), :] = jnp.where(valid != 0.0, 2.0 * token + second, pad_row).astype(jnp.int32)
        ids_copy(slot).start()

        u = jnp.concatenate([xs_ref[pl.ds(c, R, stride=TOK_ROWS), :] for c in range(PK_TILES)], axis=1)
        xb = _unpack_bf16_pairs(u).astype(BF16)
        h1 = _dot(xb, w1b[...])
        h3 = _dot(xb, w3b[...])
        hact = (h1 * _sigmoid(h1)) * h3
        y = _dot(hact.astype(BF16), w2b[...])
        pk = _pack_bf16_pairs(y)
        for c in range(PK_TILES):
            stage[pl.ds(slot * SR + c, R, stride=PK_TILES), :] = pk[:, c * LANES:(c + 1) * LANES]

    @pl.when(i == n_used)
    def _():
        ids_copy(prev).wait()
        drain_rows(slot)
        send_rows(prev)
        drain_rows(prev2)
        drain_rows(prev)


def _experts(blk_e, nblk, run_first, run_idx, next_e, xs, w1, w3, w2, n_tokens):
    P = xs.shape[0] // TOK_ROWS
    nb = P // MOE_ROWS
    assert blk_e.shape[0] == nb + 1
    grid_spec = pltpu.PrefetchScalarGridSpec(
        num_scalar_prefetch=5,
        grid=(nb + 1,),
        in_specs=[pl.BlockSpec((MOE_ROWS * TOK_ROWS, LANES), lambda i, be, n, *_: (jnp.minimum(i, n[0] - 1), 0)),
                  pl.BlockSpec(memory_space=pl.ANY), pl.BlockSpec(memory_space=pl.ANY),
                  pl.BlockSpec(memory_space=pl.ANY)],
        out_specs=pl.BlockSpec(memory_space=pl.ANY),
        scratch_shapes=[pltpu.VMEM((D_MODEL, EXPERT_FF), BF16), pltpu.VMEM((D_MODEL, EXPERT_FF), BF16),
                        pltpu.VMEM((EXPERT_FF, D_MODEL), BF16),
                        pltpu.VMEM((2, D_MODEL, EXPERT_FF), F32), pltpu.VMEM((2, D_MODEL, EXPERT_FF), F32),
                        pltpu.VMEM((2, EXPERT_FF, D_MODEL), F32),
                        pltpu.VMEM((N_STAGE * MOE_ROWS * PK_TILES, LANES), jnp.uint32),
                        pltpu.VMEM((SUBLANES, MOE_ROWS), jnp.int32),
                        pltpu.SMEM((N_STAGE, MOE_ROWS), jnp.int32),
                        pltpu.SemaphoreType.DMA((N_STAGE,)), pltpu.SemaphoreType.DMA(()),
                        pltpu.SemaphoreType.DMA((2,))],
    )
    return pl.pallas_call(
        functools.partial(_experts_kernel, n_tokens=n_tokens),
        grid_spec=grid_spec,
        out_shape=jax.ShapeDtypeStruct(((2 * n_tokens + N_STAGE * MOE_ROWS) * PK_TILES, LANES), jnp.uint32),
        compiler_params=pltpu.CompilerParams(dimension_semantics=("arbitrary",), vmem_limit_bytes=VMEM_LIMIT,
                                             has_side_effects=True),
        name="experts",
    )(blk_e, nblk, run_first, run_idx, next_e, xs, w1, w3, w2)


def _combine_kernel(x1_ref, wts_ref, yt_ref, g_ref, b_ref, out_ref, *, tile):
    T = tile
    wpad = jnp.concatenate([wts_ref[...], jnp.zeros((LANES - SUBLANES, T), F32)], axis=0)
    wt = wpad.T
    slot = lambda s: _unpack_bf16_pairs(jnp.concatenate(
        [yt_ref[pl.ds(s * PK_TILES + c, T, stride=2 * PK_TILES), :] for c in range(PK_TILES)], axis=1))
    y = wt[:, 0:1] * slot(0) + wt[:, 1:2] * slot(1)
    out_ref[...] = _layer_norm(DN_ALPHA * x1_ref[...] + y, g_ref[...], b_ref[...])


def _combine(x1f, wts, yt, g, b):
    N = x1f.shape[0]
    T = min(COMBINE_TILE, N)
    assert N % T == 0
    return pl.pallas_call(
        functools.partial(_combine_kernel, tile=T),
        grid=(N // T,),
        in_specs=[pl.BlockSpec((T, D_MODEL), lambda i: (i, 0)),
                  pl.BlockSpec((SUBLANES, T), lambda i: (0, i)),
                  pl.BlockSpec((T * 2 * PK_TILES, LANES), lambda i: (i, 0)),
                  _const_spec(g.shape), _const_spec(b.shape)],
        out_specs=pl.BlockSpec((T, D_MODEL), lambda i: (i, 0)),
        out_shape=jax.ShapeDtypeStruct((N, D_MODEL), F32),
        compiler_params=pltpu.CompilerParams(dimension_semantics=("arbitrary",)),
        name="combine",
    )(x1f, wts, yt, g, b)


def _odd_tiles(w):
    return jnp.pad(w, ((0, 0), (0, LANES))).astype(BF16)


def _pack_mixer_weights(w_in, b_in, rg_w_a, rg_w_x, rg_b_a, rg_b_x, gla_w_a2):
    wmain = _odd_tiles(w_in[:, :C_MAIN_END])
    wbg = _odd_tiles(w_in[:, C_GATES_START:])
    walr = jnp.pad(w_in[:, C_MAIN_END:C_GATES_START], ((0, 0), (0, LANES - GLA_RANK))).astype(BF16)
    bmain = b_in[None, :C_MAIN_END]
    bbg = b_in[None, C_GATES_START:]
    balr = jnp.pad(b_in[None, C_MAIN_END:C_GATES_START], ((0, 0), (0, LANES - GLA_RANK)))
    zero = jnp.zeros((RNN_BLOCK_W, RNN_BLOCK_W), w_in.dtype)
    tiles = []
    for p in range(RNN_BLOCKS // 2):
        top = jnp.concatenate([rg_w_a[2 * p], zero, rg_w_x[2 * p], zero], axis=1)
        bot = jnp.concatenate([zero, rg_w_a[2 * p + 1], zero, rg_w_x[2 * p + 1]], axis=1)
        tiles.append(jnp.concatenate([top, bot], axis=0))
    wgate = jnp.stack(tiles).astype(BF16)
    bgate = jnp.concatenate([rg_b_a, rg_b_x])[None, :]
    wa2 = jnp.concatenate([gla_w_a2, jnp.zeros((LANES - GLA_RANK, GLA_DK), gla_w_a2.dtype)], axis=0).astype(BF16)
    return (wmain, bmain, wbg, bbg, walr, balr), wgate, bgate, wa2


def _layer(x, w_in, b_in, conv_w, conv_b, rg_w_a, rg_b_a, rg_w_x, rg_b_x, rg_lambda, gla_w_a2, gla_b_a,
           gla_norm_g, w_proj_rnn, w_proj_gla, w_o, b_o, ln1_g, ln1_b, router_w_group, router_b_group,
           router_w_expert, router_b_expert, exp_w1, exp_w3, exp_w2, ln2_g, ln2_b):
    B, S, _ = x.shape
    N = B * S
    row = lambda p: p[None, :]

    w_slices, wgate, bgate, wa2 = _pack_mixer_weights(w_in, b_in, rg_w_a, rg_w_x, rg_b_a, rg_b_x, gla_w_a2)
    wr = jnp.concatenate([router_w_group.T, jnp.zeros((SUBLANES - N_GROUPS, D_MODEL), F32), router_w_expert.T],
                         axis=0).astype(BF16)
    br = jnp.concatenate([router_b_group, jnp.zeros((SUBLANES - N_GROUPS,), F32), router_b_expert])[:, None]
    x1f, xpk, info, wts, tcnt = _mixer(
        x, *w_slices, conv_w, row(conv_b), wgate, bgate, row(rg_lambda), wa2, row(gla_b_a), row(gla_norm_g),
        _odd_tiles(w_proj_rnn), _odd_tiles(w_proj_gla), _odd_tiles(w_o), row(b_o), row(ln1_g), row(ln1_b), wr, br)

    tcnt = tcnt[:, :, 0]
    nt = tcnt.shape[0]
    tot = jnp.sum(tcnt, axis=0)
    pcount = (tot + MOE_ROWS - 1) // MOE_ROWS * MOE_ROWS
    pend = jnp.cumsum(pcount)
    base = (pend - pcount)[None, :] + jnp.cumsum(tcnt, axis=0) - tcnt
    base_tok = jnp.repeat(base.T, N // nt, axis=1)
    experts_col = jnp.arange(N_EXPERTS, dtype=jnp.int32)[:, None, None]
    dest = jnp.sum(jnp.where(info[None, 0:2] == experts_col, base_tok[:, None, :], 0), axis=0) + info[2:4]
    nb = -(-(2 * N) // MOE_ROWS) + N_EXPERTS
    P = nb * MOE_ROWS
    nblk = (pend[-1] // MOE_ROWS).astype(jnp.int32)
    blk_start = jnp.minimum(jnp.arange(nb + 1, dtype=jnp.int32), nblk - 1) * MOE_ROWS
    blk_e = jnp.sum((blk_start[:, None] >= pend[None, :]).astype(jnp.int32), axis=1)
    blk_e = jnp.minimum(blk_e, N_EXPERTS - 1)

    last_blk = jnp.where(tot > 0, pend - MOE_ROWS, -1)
    tail_blk = nblk + jnp.arange(N_EXPERTS, dtype=jnp.int32)
    tail_blk = jnp.where(tail_blk < nb, tail_blk * MOE_ROWS, -1)
    xs = _dispatch(jnp.concatenate([last_blk, tail_blk]).astype(jnp.int32), dest, xpk, P)
    blk_i = jnp.arange(nb + 1, dtype=jnp.int32)
    experts_row = jnp.arange(N_EXPERTS, dtype=jnp.int32)
    run_first = ((blk_i < nblk) & ((blk_i == 0) | (blk_e != jnp.roll(blk_e, 1)))).astype(jnp.int32)
    run_idx = jnp.sum(jnp.where(blk_i[None, :] <= blk_i[:, None], run_first[None, :], 0), axis=1) - 1
    later = (experts_row[None, :] > experts_row[:, None]) & (tot[None, :] > 0)
    next_nonempty = jnp.min(jnp.where(later, experts_row[None, :], N_EXPERTS), axis=1)
    next_nonempty = jnp.where(next_nonempty < N_EXPERTS, next_nonempty, -1)
    next_e = jnp.sum(jnp.where(blk_e[:, None] == experts_row[None, :], next_nonempty[None, :], 0), axis=1)
    yt = _experts(blk_e, nblk[None], run_first, run_idx.astype(jnp.int32), next_e.astype(jnp.int32),
                  xs, exp_w1, exp_w3, exp_w2, N)
    out = _combine(x1f, wts, yt, row(ln2_g), row(ln2_b))
    return out.reshape(B, S, D_MODEL)


def kernel(x, w_in, b_in, conv_w, conv_b, rg_w_a, rg_b_a, rg_w_x, rg_b_x, rg_lambda, gla_w_a2, gla_b_a, gla_norm_g, w_proj_rnn, w_proj_gla, w_o, b_o, ln1_g, ln1_b, router_w_group, router_b_group, router_w_expert, router_b_expert, exp_w1, exp_w3, exp_w2, ln2_g, ln2_b):
    h = x
    for l in range(w_in.shape[0]):
        h = _layer(h, w_in[l], b_in[l], conv_w[l], conv_b[l], rg_w_a[l], rg_b_a[l], rg_w_x[l], rg_b_x[l],
                   rg_lambda[l], gla_w_a2[l], gla_b_a[l], gla_norm_g[l], w_proj_rnn[l], w_proj_gla[l], w_o[l],
                   b_o[l], ln1_g[l], ln1_b[l], router_w_group[l], router_b_group[l], router_w_expert[l],
                   router_b_expert[l], exp_w1[l], exp_w3[l], exp_w2[l], ln2_g[l], ln2_b[l])
    return h
```

```python
import functools

import jax
import jax.numpy as jnp
from jax import lax
from jax.experimental import pallas as pl
from jax.experimental.pallas import tpu as pltpu

F32 = jnp.float32
BF16 = jnp.bfloat16

D_MODEL = 1024
RNN_WIDTH = 1024
RNN_BLOCKS = 8
RNN_BLOCK_W = RNN_WIDTH // RNN_BLOCKS
CONV_WIDTH = 4
LRU_C = 8.0
GLA_HEADS = 4
GLA_DK = D_MODEL // 2
GLA_DV = D_MODEL
GLA_HEAD_K = GLA_DK // GLA_HEADS
GLA_HEAD_V = GLA_DV // GLA_HEADS
GLA_RANK = 16
GLA_TAU = 16.0
GLA_CHUNK = 64
N_GROUPS = 4
EXPERTS_PER_GROUP = 8
N_EXPERTS = N_GROUPS * EXPERTS_PER_GROUP
EXPERT_FF = 512
DN_ALPHA = 2.0 ** 0.25
LN_EPS = 1e-5
RMS_EPS = 1e-6

LANES = 128
SUBLANES = 8
VMEM_LIMIT = 56 * 1024 * 1024

C_RX, C_RY, C_Q, C_K, C_V, C_G = 0, 1024, 2048, 2560, 3072, 4096
C_GA, C_GB = 0, 1024
C_MAIN_END = 5120
C_GATES_START = C_MAIN_END + GLA_RANK

MIX_TILE = 256
MOE_ROWS = 512
DISPATCH_TILE = 512
COMBINE_TILE = 512
N_STAGE = 3
ROUTE_ROWS = 8 + N_EXPERTS
PK_TILES = D_MODEL // 2 // LANES
TOK_ROWS = PK_TILES + 1
META_EXPERT, META_TOKEN, META_VALID = 0, 1, 2


def _sigmoid(v):
    return 1.0 / (1.0 + jnp.exp(-v))


def _softplus(v):
    return jnp.maximum(v, 0.0) + jnp.log1p(jnp.exp(-jnp.abs(v)))


def _layer_norm(v, g, b):
    mu = jnp.mean(v, axis=-1, keepdims=True)
    c = v - mu
    var = jnp.mean(c * c, axis=-1, keepdims=True)
    return c * lax.rsqrt(var + LN_EPS) * g + b


def _dot(a, b):
    return jnp.dot(a, b, preferred_element_type=F32)


def _dot_nt(a, b):
    return lax.dot_general(a, b, (((1,), (1,)), ((), ())), preferred_element_type=F32)


def _dot_tn(a, b):
    return lax.dot_general(a, b, (((0,), (0,)), ((), ())), preferred_element_type=F32)


def _pack_bf16_pairs(v):
    half = v.shape[1] // 2
    bits = pltpu.bitcast(v.astype(BF16).astype(F32), jnp.uint32)
    return (bits[:, :half] >> 16) | bits[:, half:]


def _unpack_bf16_pairs(u):
    lo = pltpu.bitcast(u << 16, F32)
    hi = pltpu.bitcast(u & jnp.uint32(0xFFFF0000), F32)
    return jnp.concatenate([lo, hi], axis=1)


def _const_spec(shape):
    nd = len(shape)
    return pl.BlockSpec(shape, lambda *_: (0,) * nd, pipeline_mode=pl.Buffered(1))


def _mixer_kernel(x_ref, wmain_ref, bmain_ref, wbg_ref, bbg_ref, walr_ref, balr_ref,
                  convw_ref, convb_ref, wgate_ref, bgate_ref, lam_ref,
                  wa2_ref, ba2_ref, gnorm_ref, wprnn_ref, wpgla_ref, wo_ref, bo_ref, ln1g_ref, ln1b_ref,
                  wr_ref, br_ref,
                  x1_ref, xpk_ref, info_ref, wts_ref, cnt_ref, rxbuf, hcar, st_ref, hbuf, zbuf,
                  *, tile, steps_per_seq):
    T = tile
    step = pl.program_id(0)

    @pl.when(step == 0)
    def _():
        zbuf[...] = jnp.zeros_like(zbuf)

    @pl.when(step % steps_per_seq == 0)
    def _():
        rxbuf[0:SUBLANES, :] = jnp.zeros((SUBLANES, RNN_WIDTH), F32)
        hcar[...] = jnp.zeros_like(hcar)
        st_ref[...] = jnp.zeros_like(st_ref)

    x = x_ref[...]
    xb = x.astype(BF16)

    def proj(c0, c1, w_ref=wmain_ref, b_ref=bmain_ref):
        return _dot(xb, w_ref[:, c0:c1]) + b_ref[:, c0:c1]

    rx = proj(C_RX, C_RX + RNN_WIDTH)
    rxbuf[SUBLANES:SUBLANES + T, :] = rx
    u = convb_ref[...] + convw_ref[CONV_WIDTH - 1:CONV_WIDTH, :] * rx
    for j in range(1, CONV_WIDTH):
        u = u + convw_ref[CONV_WIDTH - 1 - j:CONV_WIDTH - j, :] * rxbuf[SUBLANES - j:SUBLANES - j + T, :]
    rxbuf[0:SUBLANES, :] = rxbuf[T:T + SUBLANES, :]

    qk = proj(C_Q, C_Q + 2 * GLA_DK)
    q = qk[:, :GLA_DK] * (GLA_HEAD_K ** -0.5)
    k = qk[:, GLA_DK:]
    alr = proj(0, LANES, walr_ref, balr_ref)

    x1 = _layer_norm(zbuf[...], ln1g_ref[...], ln1b_ref[...])
    x1_ref[...] = x1
    info, wts, total = _route_tile(x1.astype(BF16), wr_ref[...], br_ref[...])
    info_ref[...] = info
    wts_ref[...] = wts
    cnt_ref[0] = jnp.broadcast_to(total, (N_EXPERTS, LANES))
    pk = _pack_bf16_pairs(x1)
    for c in range(PK_TILES):
        xpk_ref[pl.ds(c, T, stride=TOK_ROWS), :] = pk[:, c * LANES:(c + 1) * LANES]
    first_expert = jnp.broadcast_to(info[0:1, :].astype(F32), (LANES, T)).T.astype(jnp.int32)
    token = lax.broadcasted_iota(jnp.int32, (T, LANES), 0) + jnp.maximum(step - 1, 0) * T
    lane = lax.broadcasted_iota(jnp.int32, (T, LANES), 1)
    meta = jnp.where(lane == META_EXPERT, first_expert,
                     jnp.where(lane == META_TOKEN, token, jnp.where(lane == META_VALID, 1, 0)))
    xpk_ref[pl.ds(PK_TILES, T, stride=TOK_ROWS), :] = pltpu.bitcast(meta, jnp.uint32)

    r_parts, i_parts = [], []
    for p in range(RNN_BLOCKS // 2):
        up = u[:, 256 * p:256 * (p + 1)].astype(BF16)
        gp = _dot(up, wgate_ref[p])
        r_parts.append(gp[:, :256])
        i_parts.append(gp[:, 256:])
    r = _sigmoid(jnp.concatenate(r_parts, axis=1) + bgate_ref[:, :RNN_WIDTH])
    ig = _sigmoid(jnp.concatenate(i_parts, axis=1) + bgate_ref[:, RNN_WIDTH:])
    v = proj(C_V, C_V + GLA_DV)
    ry = proj(C_RY, C_RY + RNN_WIDTH)

    z = _dot(alr.astype(BF16), wa2_ref[...]) + ba2_ref[...]
    la = -_softplus(-z) * (1.0 / GLA_TAU)
    ri = lax.broadcasted_iota(jnp.int32, (T, T), 0)
    ci = lax.broadcasted_iota(jnp.int32, (T, T), 1)
    chunk_start = (ri >> 6) << 6
    tri = jnp.where((ci <= ri) & (ci >= chunk_start), 1.0, 0.0).astype(BF16)
    la_hi = la.astype(BF16)
    la_lo = (la - la_hi.astype(F32)).astype(BF16)
    bcum = _dot(tri, la_hi) + _dot(tri, la_lo)

    g = proj(C_G, C_G + GLA_DV)

    log_a = (-LRU_C) * r * _softplus(-lam_ref[...])
    a = jnp.exp(log_a)
    m2 = -jnp.tanh(log_a) * (1.0 + a * a)
    bv = jnp.where(m2 > 0.0, m2 * lax.rsqrt(m2), 0.0) * (ig * u)

    cr = lax.broadcasted_iota(jnp.int32, (GLA_CHUNK, GLA_CHUNK), 0)
    cc = lax.broadcasted_iota(jnp.int32, (GLA_CHUNK, GLA_CHUNK), 1)
    causal = cr >= cc
    n_chunks = T // GLA_CHUNK
    heads = [(slice(hd * GLA_HEAD_K, (hd + 1) * GLA_HEAD_K), slice(hd * GLA_HEAD_V, (hd + 1) * GLA_HEAD_V))
             for hd in range(GLA_HEADS)]
    qd_c, ki_c, ke_c, vv_c, dec_c = [], [], [], [], []
    for c in range(n_chunks):
        r0 = c * GLA_CHUNK
        bc = bcum[r0:r0 + GLA_CHUNK, :]
        bl = bcum[r0 + GLA_CHUNK - 1:r0 + GLA_CHUNK, :]
        kc = k[r0:r0 + GLA_CHUNK, :]
        qd_c.append((q[r0:r0 + GLA_CHUNK, :] * jnp.exp(bc)).astype(BF16))
        ki_c.append((kc * jnp.exp(-bc)).astype(BF16))
        ke_c.append((kc * jnp.exp(bl - bc)).astype(BF16))
        vv_c.append(v[r0:r0 + GLA_CHUNK, :].astype(BF16))
        dec_c.append(jnp.exp(bl))
    scores = [[jnp.where(causal, _dot_nt(qd_c[c][:, ks], ki_c[c][:, ks]), 0.0).astype(BF16) for ks, _ in heads]
              for c in range(n_chunks)]
    intra = [[_dot(scores[c][hd], vv_c[c][:, vs]) for hd, (_, vs) in enumerate(heads)] for c in range(n_chunks)]
    incr = [[_dot_tn(vv_c[c][:, vs], ke_c[c][:, ks]) for ks, vs in heads] for c in range(n_chunks)]

    n_groups = T // SUBLANES
    sub = lax.broadcasted_iota(jnp.int32, (n_groups, SUBLANES, RNN_WIDTH), 1)
    sa = a.reshape(n_groups, SUBLANES, RNN_WIDTH)
    sb = bv.reshape(n_groups, SUBLANES, RNN_WIDTH)
    for s in (1, 2, 4):
        keep = sub >= s
        sb = sa * jnp.where(keep, pltpu.roll(sb, s, 1), 0.0) + sb
        sa = sa * jnp.where(keep, pltpu.roll(sa, s, 1), 1.0)
    carry = hcar[0:1, :]
    for gi in range(n_groups):
        hg = sb[gi] + sa[gi] * carry
        hbuf[gi * SUBLANES:(gi + 1) * SUBLANES, :] = hg
        carry = hg[SUBLANES - 1:SUBLANES, :]
    hcar[0:1, :] = carry
    h = hbuf[...]
    ga = proj(C_GA, C_GA + D_MODEL, wbg_ref, bbg_ref)

    out_a = _dot((h * jax.nn.gelu(ry)).astype(BF16), wprnn_ref[:, :D_MODEL])
    gb = proj(C_GB, C_GB + D_MODEL, wbg_ref, bbg_ref)

    states = [st_ref[hd] for hd in range(GLA_HEADS)]
    o_chunks = []
    for c in range(n_chunks):
        o_heads = []
        for hd, (ks, _) in enumerate(heads):
            o_heads.append(intra[c][hd] + _dot_nt(qd_c[c][:, ks], states[hd].astype(BF16)))
            states[hd] = states[hd] * dec_c[c][:, ks] + incr[c][hd]
        o_chunks.append(jnp.concatenate(o_heads, axis=1))
    for hd in range(GLA_HEADS):
        st_ref[hd] = states[hd]
    o_all = jnp.concatenate(o_chunks, axis=0)

    o_parts = []
    for hd in range(GLA_HEADS):
        vs = slice(hd * GLA_HEAD_V, (hd + 1) * GLA_HEAD_V)
        oh = o_all[:, vs]
        ms = jnp.mean(oh * oh, axis=-1, keepdims=True)
        o_parts.append(oh * lax.rsqrt(ms + RMS_EPS) * gnorm_ref[:, vs])
    on = jnp.concatenate(o_parts, axis=1) * (g * _sigmoid(g))
    out_b = _dot(on.astype(BF16), wpgla_ref[:, :D_MODEL])

    merged = _sigmoid(ga) * out_a + _sigmoid(gb) * out_b
    y = _dot(merged.astype(BF16), wo_ref[:, :D_MODEL]) + bo_ref[...]
    zbuf[...] = DN_ALPHA * x + y


def _mixer(x, wmain, bmain, wbg, bbg, walr, balr, conv_w, conv_b, wgate, bgate, lam, wa2, ba2, gnorm,
           wprnn, wpgla, wo, bo, ln1g, ln1b, wr, br):
    B, S, _ = x.shape
    T = min(MIX_TILE, S)
    assert S % T == 0 and T % GLA_CHUNK == 0
    N = B * S
    nt = N // T
    mix_tile = lambda i: jnp.minimum(i, nt - 1)
    tail_tile = lambda i: jnp.maximum(i - 1, 0)
    weights = (wmain, bmain, wbg, bbg, walr, balr, conv_w, conv_b, wgate, bgate, lam, wa2, ba2, gnorm,
               wprnn, wpgla, wo, bo, ln1g, ln1b, wr, br)
    return pl.pallas_call(
        functools.partial(_mixer_kernel, tile=T, steps_per_seq=S // T),
        grid=(nt + 1,),
        in_specs=[pl.BlockSpec((T, D_MODEL), lambda i: (mix_tile(i), 0))] + [_const_spec(w.shape) for w in weights],
        out_specs=[pl.BlockSpec((T, D_MODEL), lambda i: (tail_tile(i), 0)),
                   pl.BlockSpec((T * TOK_ROWS, LANES), lambda i: (tail_tile(i), 0)),
                   pl.BlockSpec((SUBLANES, T), lambda i: (0, tail_tile(i))),
                   pl.BlockSpec((SUBLANES, T), lambda i: (0, tail_tile(i))),
                   pl.BlockSpec((1, N_EXPERTS, LANES), lambda i: (tail_tile(i), 0, 0))],
        out_shape=[jax.ShapeDtypeStruct((N, D_MODEL), F32),
                   jax.ShapeDtypeStruct((N * TOK_ROWS, LANES), jnp.uint32),
                   jax.ShapeDtypeStruct((SUBLANES, N), jnp.int32),
                   jax.ShapeDtypeStruct((SUBLANES, N), F32),
                   jax.ShapeDtypeStruct((nt, N_EXPERTS, LANES), jnp.int32)],
        scratch_shapes=[pltpu.VMEM((T + SUBLANES, RNN_WIDTH), F32),
                        pltpu.VMEM((SUBLANES, RNN_WIDTH), F32),
                        pltpu.VMEM((GLA_HEADS, GLA_HEAD_V, GLA_HEAD_K), F32),
                        pltpu.VMEM((T, RNN_WIDTH), F32),
                        pltpu.VMEM((T, D_MODEL), F32)],
        compiler_params=pltpu.CompilerParams(dimension_semantics=("arbitrary",), vmem_limit_bytes=VMEM_LIMIT),
        name="mixer",
    )(x.reshape(N, D_MODEL), *weights)


def _route_tile(x1b, wr, br):
    T = x1b.shape[0]
    logits = _dot_nt(wr, x1b) + br
    row8 = lax.broadcasted_iota(jnp.int32, (SUBLANES, T), 0)
    row8f = row8.astype(F32)
    neg = jnp.float32(-jnp.inf)
    first = lambda hit: jnp.min(jnp.where(hit, row8f, float(SUBLANES)), axis=0, keepdims=True)

    gl = jnp.where(row8 < N_GROUPS, logits[0:SUBLANES, :], neg)
    gmax = jnp.max(gl, axis=0, keepdims=True)
    grp = first(gl == gmax)
    p_grp = 1.0 / jnp.sum(jnp.exp(gl - gmax), axis=0, keepdims=True)

    e_sel = jnp.zeros((EXPERTS_PER_GROUP, T), F32)
    for gi in range(N_GROUPS):
        lo = SUBLANES + gi * EXPERTS_PER_GROUP
        e_sel = jnp.where(grp == float(gi), logits[lo:lo + EXPERTS_PER_GROUP, :], e_sel)
    m1 = jnp.max(e_sel, axis=0, keepdims=True)
    i1 = first(e_sel == m1)
    e_rest = jnp.where(row8f == i1, neg, e_sel)
    m2 = jnp.max(e_rest, axis=0, keepdims=True)
    i2 = first(e_rest == m2)
    e21 = jnp.exp(m2 - m1)
    p1 = 1.0 / (1.0 + e21)
    w0 = p_grp * p1
    w1 = p_grp * (e21 * p1)
    eid0 = (grp * EXPERTS_PER_GROUP + i1).astype(jnp.int32)
    eid1 = (grp * EXPERTS_PER_GROUP + i2).astype(jnp.int32)

    erow = lax.broadcasted_iota(jnp.int32, (N_EXPERTS, T), 0)
    oh0 = jnp.where(erow == eid0, 1.0, 0.0)
    oh1 = jnp.where(erow == eid1, 1.0, 0.0)
    both = oh0 + oh1
    ti = lax.broadcasted_iota(jnp.int32, (T, T), 0)
    tj = lax.broadcasted_iota(jnp.int32, (T, T), 1)
    before = jnp.where(ti < tj, 1.0, 0.0).astype(BF16)
    prior = _dot(both.astype(BF16), before)
    rank0 = jnp.sum(prior * oh0, axis=0, keepdims=True).astype(jnp.int32)
    rank1 = jnp.sum(prior * oh1, axis=0, keepdims=True).astype(jnp.int32)
    total = (prior[:, T - 1:T] + both[:, T - 1:T]).astype(jnp.int32)

    info = jnp.where(row8 == 0, eid0, jnp.where(row8 == 1, eid1, jnp.where(row8 == 2, rank0,
                     jnp.where(row8 == 3, rank1, 0))))
    return info, jnp.where(row8 == 0, w0, jnp.where(row8 == 1, w1, 0.0)), total


def _dispatch_kernel(zero_blk_ref, dest_ref, xpk_ref, xs_ref, zbuf, sem, zsem, *, tile):
    T = tile

    @pl.when(pl.program_id(0) == 0)
    def _():
        zbuf[...] = jnp.zeros_like(zbuf)

        def zero_copy(j):
            row = pl.multiple_of(jnp.maximum(zero_blk_ref[j], 0) * TOK_ROWS, MOE_ROWS * TOK_ROWS)
            return pltpu.make_async_copy(zbuf, xs_ref.at[pl.ds(row, MOE_ROWS * TOK_ROWS)], zsem)

        for j in range(2 * N_EXPERTS):
            @pl.when(zero_blk_ref[j] >= 0)
            def _():
                zero_copy(j).start()
        for j in range(2 * N_EXPERTS):
            @pl.when(zero_blk_ref[j] >= 0)
            def _():
                zero_copy(j).wait()

    for t in range(T):
        for kk in range(2):
            row = dest_ref[kk, t] * TOK_ROWS
            pltpu.make_async_copy(xpk_ref.at[pl.ds(t * TOK_ROWS, TOK_ROWS)], xs_ref.at[pl.ds(row, TOK_ROWS)],
                                  sem).start(priority=kk)
    for kk in range(2):
        pltpu.make_async_copy(xpk_ref, xs_ref.at[pl.ds(0, T * TOK_ROWS)], sem).wait()


def _dispatch(last_blk, dest, xpk, n_rows):
    N = xpk.shape[0] // TOK_ROWS
    T = min(DISPATCH_TILE, N)
    assert N % T == 0
    grid_spec = pltpu.PrefetchScalarGridSpec(
        num_scalar_prefetch=1,
        grid=(N // T,),
        in_specs=[pl.BlockSpec((2, T), lambda i, lb: (0, i), memory_space=pltpu.SMEM),
                  pl.BlockSpec((T * TOK_ROWS, LANES), lambda i, lb: (i, 0))],
        out_specs=pl.BlockSpec(memory_space=pl.ANY),
        scratch_shapes=[pltpu.VMEM((MOE_ROWS * TOK_ROWS, LANES), jnp.uint32),
                        pltpu.SemaphoreType.DMA(()), pltpu.SemaphoreType.DMA(())],
    )
    return pl.pallas_call(
        functools.partial(_dispatch_kernel, tile=T),
        grid_spec=grid_spec,
        out_shape=jax.ShapeDtypeStruct((n_rows * TOK_ROWS, LANES), jnp.uint32),
        compiler_params=pltpu.CompilerParams(dimension_semantics=("arbitrary",), has_side_effects=True),
        name="dispatch",
    )(last_blk, dest, xpk)


def _experts_kernel(blk_e_ref, nblk_ref, run_first_ref, run_idx_ref, next_e_ref,
                    xs_ref, w1_hbm, w3_hbm, w2_hbm, yt_ref,
                    w1b, w3b, w2b, w1f, w3f, w2f, stage, ids_v, ids_s, row_sem, ids_sem, w_sem, *, n_tokens):
    i = pl.program_id(0)
    R = MOE_ROWS
    SR = R * PK_TILES
    n_used = nblk_ref[0]
    spare = 2 * n_tokens
    slot = i % N_STAGE
    prev = (i + N_STAGE - 1) % N_STAGE
    prev2 = (i + N_STAGE - 2) % N_STAGE

    def ids_copy(s):
        return pltpu.make_async_copy(ids_v.at[pl.ds(s, 1)], ids_s.at[pl.ds(s, 1)], ids_sem)

    def row_copy(s, r, dst_row):
        src = pl.multiple_of((s * R + r) * PK_TILES, PK_TILES)
        dst = pl.multiple_of(dst_row * PK_TILES, PK_TILES)
        return pltpu.make_async_copy(stage.at[pl.ds(src, PK_TILES)], yt_ref.at[pl.ds(dst, PK_TILES)],
                                     row_sem.at[s]).start(priority=r % 2)

    def drain_rows(s):
        pltpu.make_async_copy(stage.at[pl.ds(0, SR)], yt_ref.at[pl.ds(0, SR)], row_sem.at[s]).wait()

    def send_rows(s):
        for r in range(R):
            row_copy(s, r, ids_s[s, r])

    @pl.when(i == 0)
    def _():
        stage[...] = jnp.zeros_like(stage)
        ids_v[...] = (spare + lax.broadcasted_iota(jnp.int32, (SUBLANES, R), 0) * R
                      + lax.broadcasted_iota(jnp.int32, (SUBLANES, R), 1))
        ids_copy(N_STAGE - 1).start()
        for s in range(N_STAGE - 1):
            for r in range(R):
                row_copy(s, r, spare + s * R + r)

    def weight_copies(e, buf):
        return [pltpu.make_async_copy(src.at[e], dst.at[buf], w_sem.at[buf])
                for src, dst in ((w1_hbm, w1f), (w3_hbm, w3f), (w2_hbm, w2f))]

    @pl.when(i == 0)
    def _():
        for cp in weight_copies(blk_e_ref[0], 0):
            cp.start()

    @pl.when(run_first_ref[i] != 0)
    def _():
        buf = run_idx_ref[i] % 2
        for cp in weight_copies(blk_e_ref[i], buf):
            cp.wait()
        w1b[...] = w1f[buf].astype(BF16)
        w3b[...] = w3f[buf].astype(BF16)
        w2b[...] = w2f[buf].astype(BF16)

        @pl.when(next_e_ref[i] >= 0)
        def _():
            for cp in weight_copies(next_e_ref[i], 1 - buf):
                cp.start()

    @pl.when(i < n_used)
    def _():
        ids_copy(prev).wait()
        drain_rows(slot)
        send_rows(prev)

        meta = pltpu.bitcast(xs_ref[pl.ds(PK_TILES, R, stride=TOK_ROWS), :], jnp.int32).astype(F32).T
        first_expert = meta[META_EXPERT:META_EXPERT + 1, :]
        token = meta[META_TOKEN:META_TOKEN + 1, :]
        valid = meta[META_VALID:META_VALID + 1, :]
        second = jnp.where(first_expert != blk_e_ref[i].astype(F32), 1.0, 0.0)
        pad_row = (spare + slot * R + lax.broadcasted_iota(jnp.int32, (1, R), 1)).astype(F32)
        ids_v[pl.ds(slot, 1), :] = jnp.where(valid != 0.0, 2.0 * token + second, pad_row).astype(jnp.int32)
        ids_copy(slot).start()

        u = jnp.concatenate([xs_ref[pl.ds(c, R, stride=TOK_ROWS), :] for c in range(PK_TILES)], axis=1)
        xb = _unpack_bf16_pairs(u).astype(BF16)
        h1 = _dot(xb, w1b[...])
        h3 = _dot(xb, w3b[...])
        hact = (h1 * _sigmoid(h1)) * h3
        y = _dot(hact.astype(BF16), w2b[...])
        pk = _pack_bf16_pairs(y)
        for c in range(PK_TILES):
            stage[pl.ds(slot * SR + c, R, stride=PK_TILES), :] = pk[:, c * LANES:(c + 1) * LANES]

    @pl.when(i == n_used)
    def _():
        ids_copy(prev).wait()
        drain_rows(slot)
        send_rows(prev)
        drain_rows(prev2)
        drain_rows(prev)


def _experts(blk_e, nblk, run_first, run_idx, next_e, xs, w1, w3, w2, n_tokens):
    P = xs.shape[0] // TOK_ROWS
    nb = P // MOE_ROWS
    assert blk_e.shape[0] == nb + 1
    grid_spec = pltpu.PrefetchScalarGridSpec(
        num_scalar_prefetch=5,
        grid=(nb + 1,),
        in_specs=[pl.BlockSpec((MOE_ROWS * TOK_ROWS, LANES), lambda i, be, n, *_: (jnp.minimum(i, n[0] - 1), 0)),
                  pl.BlockSpec(memory_space=pl.ANY), pl.BlockSpec(memory_space=pl.ANY),
                  pl.BlockSpec(memory_space=pl.ANY)],
        out_specs=pl.BlockSpec(memory_space=pl.ANY),
        scratch_shapes=[pltpu.VMEM((D_MODEL, EXPERT_FF), BF16), pltpu.VMEM((D_MODEL, EXPERT_FF), BF16),
                        pltpu.VMEM((EXPERT_FF, D_MODEL), BF16),
                        pltpu.VMEM((2, D_MODEL, EXPERT_FF), F32), pltpu.VMEM((2, D_MODEL, EXPERT_FF), F32),
                        pltpu.VMEM((2, EXPERT_FF, D_MODEL), F32),
                        pltpu.VMEM((N_STAGE * MOE_ROWS * PK_TILES, LANES), jnp.uint32),
                        pltpu.VMEM((SUBLANES, MOE_ROWS), jnp.int32),
                        pltpu.SMEM((N_STAGE, MOE_ROWS), jnp.int32),
                        pltpu.SemaphoreType.DMA((N_STAGE,)), pltpu.SemaphoreType.DMA(()),
                        pltpu.SemaphoreType.DMA((2,))],
    )
    return pl.pallas_call(
        functools.partial(_experts_kernel, n_tokens=n_tokens),
        grid_spec=grid_spec,
        out_shape=jax.ShapeDtypeStruct(((2 * n_tokens + N_STAGE * MOE_ROWS) * PK_TILES, LANES), jnp.uint32),
        compiler_params=pltpu.CompilerParams(dimension_semantics=("arbitrary",), vmem_limit_bytes=VMEM_LIMIT,
                                             has_side_effects=True),
        name="experts",
    )(blk_e, nblk, run_first, run_idx, next_e, xs, w1, w3, w2)


def _combine_kernel(x1_ref, wts_ref, yt_ref, g_ref, b_ref, out_ref, *, tile):
    T = tile
    wpad = jnp.concatenate([wts_ref[...], jnp.zeros((LANES - SUBLANES, T), F32)], axis=0)
    wt = wpad.T
    slot = lambda s: _unpack_bf16_pairs(jnp.concatenate(
        [yt_ref[pl.ds(s * PK_TILES + c, T, stride=2 * PK_TILES), :] for c in range(PK_TILES)], axis=1))
    y = wt[:, 0:1] * slot(0) + wt[:, 1:2] * slot(1)
    out_ref[...] = _layer_norm(DN_ALPHA * x1_ref[...] + y, g_ref[...], b_ref[...])


def _combine(x1f, wts, yt, g, b):
    N = x1f.shape[0]
    T = min(COMBINE_TILE, N)
    assert N % T == 0
    return pl.pallas_call(
        functools.partial(_combine_kernel, tile=T),
        grid=(N // T,),
        in_specs=[pl.BlockSpec((T, D_MODEL), lambda i: (i, 0)),
                  pl.BlockSpec((SUBLANES, T), lambda i: (0, i)),
                  pl.BlockSpec((T * 2 * PK_TILES, LANES), lambda i: (i, 0)),
                  _const_spec(g.shape), _const_spec(b.shape)],
        out_specs=pl.BlockSpec((T, D_MODEL), lambda i: (i, 0)),
        out_shape=jax.ShapeDtypeStruct((N, D_MODEL), F32),
        compiler_params=pltpu.CompilerParams(dimension_semantics=("arbitrary",)),
        name="combine",
    )(x1f, wts, yt, g, b)


def _odd_tiles(w):
    return jnp.pad(w, ((0, 0), (0, LANES))).astype(BF16)


def _pack_mixer_weights(w_in, b_in, rg_w_a, rg_w_x, rg_b_a, rg_b_x, gla_w_a2):
    wmain = _odd_tiles(w_in[:, :C_MAIN_END])
    wbg = _odd_tiles(w_in[:, C_GATES_START:])
    walr = jnp.pad(w_in[:, C_MAIN_END:C_GATES_START], ((0, 0), (0, LANES - GLA_RANK))).astype(BF16)
    bmain = b_in[None, :C_MAIN_END]
    bbg = b_in[None, C_GATES_START:]
    balr = jnp.pad(b_in[None, C_MAIN_END:C_GATES_START], ((0, 0), (0, LANES - GLA_RANK)))
    zero = jnp.zeros((RNN_BLOCK_W, RNN_BLOCK_W), w_in.dtype)
    tiles = []
    for p in range(RNN_BLOCKS // 2):
        top = jnp.concatenate([rg_w_a[2 * p], zero, rg_w_x[2 * p], zero], axis=1)
        bot = jnp.concatenate([zero, rg_w_a[2 * p + 1], zero, rg_w_x[2 * p + 1]], axis=1)
        tiles.append(jnp.concatenate([top, bot], axis=0))
    wgate = jnp.stack(tiles).astype(BF16)
    bgate = jnp.concatenate([rg_b_a, rg_b_x])[None, :]
    wa2 = jnp.concatenate([gla_w_a2, jnp.zeros((LANES - GLA_RANK, GLA_DK), gla_w_a2.dtype)], axis=0).astype(BF16)
    return (wmain, bmain, wbg, bbg, walr, balr), wgate, bgate, wa2


def _layer(x, w_in, b_in, conv_w, conv_b, rg_w_a, rg_b_a, rg_w_x, rg_b_x, rg_lambda, gla_w_a2, gla_b_a,
           gla_norm_g, w_proj_rnn, w_proj_gla, w_o, b_o, ln1_g, ln1_b, router_w_group, router_b_group,
           router_w_expert, router_b_expert, exp_w1, exp_w3, exp_w2, ln2_g, ln2_b):
    B, S, _ = x.shape
    N = B * S
    row = lambda p: p[None, :]

    w_slices, wgate, bgate, wa2 = _pack_mixer_weights(w_in, b_in, rg_w_a, rg_w_x, rg_b_a, rg_b_x, gla_w_a2)
    wr = jnp.concatenate([router_w_group.T, jnp.zeros((SUBLANES - N_GROUPS, D_MODEL), F32), router_w_expert.T],
                         axis=0).astype(BF16)
    br = jnp.concatenate([router_b_group, jnp.zeros((SUBLANES - N_GROUPS,), F32), router_b_expert])[:, None]
    x1f, xpk, info, wts, tcnt = _mixer(
        x, *w_slices, conv_w, row(conv_b), wgate, bgate, row(rg_lambda), wa2, row(gla_b_a), row(gla_norm_g),
        _odd_tiles(w_proj_rnn), _odd_tiles(w_proj_gla), _odd_tiles(w_o), row(b_o), row(ln1_g), row(ln1_b), wr, br)

    tcnt = tcnt[:, :, 0]
    nt = tcnt.shape[0]
    tot = jnp.sum(tcnt, axis=0)
    pcount = (tot + MOE_ROWS - 1) // MOE_ROWS * MOE_ROWS
    pend = jnp.cumsum(pcount)
    base = (pend - pcount)[None, :] + jnp.cumsum(tcnt, axis=0) - tcnt
    base_tok = jnp.repeat(base.T, N // nt, axis=1)
    experts_col = jnp.arange(N_EXPERTS, dtype=jnp.int32)[:, None, None]
    dest = jnp.sum(jnp.where(info[None, 0:2] == experts_col, base_tok[:, None, :], 0), axis=0) + info[2:4]
    nb = -(-(2 * N) // MOE_ROWS) + N_EXPERTS
    P = nb * MOE_ROWS
    nblk = (pend[-1] // MOE_ROWS).astype(jnp.int32)
    blk_start = jnp.minimum(jnp.arange(nb + 1, dtype=jnp.int32), nblk - 1) * MOE_ROWS
    blk_e = jnp.sum((blk_start[:, None] >= pend[None, :]).astype(jnp.int32), axis=1)
    blk_e = jnp.minimum(blk_e, N_EXPERTS - 1)

    last_blk = jnp.where(tot > 0, pend - MOE_ROWS, -1)
    tail_blk = nblk + jnp.arange(N_EXPERTS, dtype=jnp.int32)
    tail_blk = jnp.where(tail_blk < nb, tail_blk * MOE_ROWS, -1)
    xs = _dispatch(jnp.concatenate([last_blk, tail_blk]).astype(jnp.int32), dest, xpk, P)
    blk_i = jnp.arange(nb + 1, dtype=jnp.int32)
    experts_row = jnp.arange(N_EXPERTS, dtype=jnp.int32)
    run_first = ((blk_i < nblk) & ((blk_i == 0) | (blk_e != jnp.roll(blk_e, 1)))).astype(jnp.int32)
    run_idx = jnp.sum(jnp.where(blk_i[None, :] <= blk_i[:, None], run_first[None, :], 0), axis=1) - 1
    later = (experts_row[None, :] > experts_row[:, None]) & (tot[None, :] > 0)
    next_nonempty = jnp.min(jnp.where(later, experts_row[None, :], N_EXPERTS), axis=1)
    next_nonempty = jnp.where(next_nonempty < N_EXPERTS, next_nonempty, -1)
    next_e = jnp.sum(jnp.where(blk_e[:, None] == experts_row[None, :], next_nonempty[None, :], 0), axis=1)
    yt = _experts(blk_e, nblk[None], run_first, run_idx.astype(jnp.int32), next_e.astype(jnp.int32),
                  xs, exp_w1, exp_w3, exp_w2, N)
    out = _combine(x1f, wts, yt, row(ln2_g), row(ln2_b))
    return out.reshape(B, S, D_MODEL)


def kernel(x, w_in, b_in, conv_w, conv_b, rg_w_a, rg_b_a, rg_w_x, rg_b_x, rg_lambda, gla_w_a2, gla_b_a, gla_norm_g, w_proj_rnn, w_proj_gla, w_o, b_o, ln1_g, ln1_b, router_w_group, router_b_group, router_w_expert, router_b_expert, exp_w1, exp_w3, exp_w2, ln2_g, ln2_b):
    h = x
    for l in range(w_in.shape[0]):
        h = _layer(h, w_in[l], b_in[l], conv_w[l], conv_b[l], rg_w_a[l], rg_b_a[l], rg_w_x[l], rg_b_x[l],
                   rg_lambda[l], gla_w_a2[l], gla_b_a[l], gla_norm_g[l], w_proj_rnn[l], w_proj_gla[l], w_o[l],
                   b_o[l], ln1_g[l], ln1_b[l], router_w_group[l], router_b_group[l], router_w_expert[l],
                   router_b_expert[l], exp_w1[l], exp_w3[l], exp_w2[l], ln2_g[l], ln2_b[l])
    return h
```

```python
import functools

import jax
import jax.numpy as jnp
from jax import lax
from jax.experimental import pallas as pl
from jax.experimental.pallas import tpu as pltpu

F32 = jnp.float32
BF16 = jnp.bfloat16

D_MODEL = 1024
RNN_WIDTH = 1024
RNN_BLOCKS = 8
RNN_BLOCK_W = RNN_WIDTH // RNN_BLOCKS
CONV_WIDTH = 4
LRU_C = 8.0
GLA_HEADS = 4
GLA_DK = D_MODEL // 2
GLA_DV = D_MODEL
GLA_HEAD_K = GLA_DK // GLA_HEADS
GLA_HEAD_V = GLA_DV // GLA_HEADS
GLA_RANK = 16
GLA_TAU = 16.0
GLA_CHUNK = 64
N_GROUPS = 4
EXPERTS_PER_GROUP = 8
N_EXPERTS = N_GROUPS * EXPERTS_PER_GROUP
EXPERT_FF = 512
DN_ALPHA = 2.0 ** 0.25
LN_EPS = 1e-5
RMS_EPS = 1e-6

LANES = 128
SUBLANES = 8
VMEM_LIMIT = 56 * 1024 * 1024

C_RX, C_RY, C_Q, C_K, C_V, C_G = 0, 1024, 2048, 2560, 3072, 4096
C_GA, C_GB = 0, 1024
C_MAIN_END = 5120
C_GATES_START = C_MAIN_END + GLA_RANK

MIX_TILE = 256
MOE_ROWS = 512
DISPATCH_TILE = 1024
COMBINE_TILE = 1024
N_STAGE = 3
ROUTE_ROWS = 8 + N_EXPERTS
PK_TILES = D_MODEL // 2 // LANES
TOK_ROWS = PK_TILES + 1
META_EXPERT, META_TOKEN, META_VALID = 0, 1, 2


def _sigmoid(v):
    return 1.0 / (1.0 + jnp.exp(-v))


def _softplus(v):
    return jnp.maximum(v, 0.0) + jnp.log1p(jnp.exp(-jnp.abs(v)))


def _layer_norm(v, g, b):
    mu = jnp.mean(v, axis=-1, keepdims=True)
    c = v - mu
    var = jnp.mean(c * c, axis=-1, keepdims=True)
    return c * lax.rsqrt(var + LN_EPS) * g + b


def _dot(a, b):
    return jnp.dot(a, b, preferred_element_type=F32)


def _dot_nt(a, b):
    return lax.dot_general(a, b, (((1,), (1,)), ((), ())), preferred_element_type=F32)


def _dot_tn(a, b):
    return lax.dot_general(a, b, (((0,), (0,)), ((), ())), preferred_element_type=F32)


def _pack_bf16_pairs(v):
    half = v.shape[1] // 2
    bits = pltpu.bitcast(v.astype(BF16).astype(F32), jnp.uint32)
    return (bits[:, :half] >> 16) | bits[:, half:]


def _unpack_bf16_pairs(u):
    lo = pltpu.bitcast(u << 16, F32)
    hi = pltpu.bitcast(u & jnp.uint32(0xFFFF0000), F32)
    return jnp.concatenate([lo, hi], axis=1)


def _const_spec(shape):
    nd = len(shape)
    return pl.BlockSpec(shape, lambda *_: (0,) * nd, pipeline_mode=pl.Buffered(1))


def _mixer_kernel(x_ref, wmain_ref, bmain_ref, wbg_ref, bbg_ref, walr_ref, balr_ref,
                  convw_ref, convb_ref, wgate_ref, bgate_ref, lam_ref,
                  wa2_ref, ba2_ref, gnorm_ref, wprnn_ref, wpgla_ref, wo_ref, bo_ref, ln1g_ref, ln1b_ref,
                  wr_ref, br_ref,
                  x1_ref, xpk_ref, info_ref, wts_ref, cnt_ref, rxbuf, hcar, st_ref, hbuf, zbuf,
                  *, tile, steps_per_seq):
    T = tile
    step = pl.program_id(0)

    @pl.when(step == 0)
    def _():
        zbuf[...] = jnp.zeros_like(zbuf)

    @pl.when(step % steps_per_seq == 0)
    def _():
        rxbuf[0:SUBLANES, :] = jnp.zeros((SUBLANES, RNN_WIDTH), F32)
        hcar[...] = jnp.zeros_like(hcar)
        st_ref[...] = jnp.zeros_like(st_ref)

    x = x_ref[...]
    xb = x.astype(BF16)

    def proj(c0, c1, w_ref=wmain_ref, b_ref=bmain_ref):
        return _dot(xb, w_ref[:, c0:c1]) + b_ref[:, c0:c1]

    rx = proj(C_RX, C_RX + RNN_WIDTH)
    rxbuf[SUBLANES:SUBLANES + T, :] = rx
    u = convb_ref[...] + convw_ref[CONV_WIDTH - 1:CONV_WIDTH, :] * rx
    for j in range(1, CONV_WIDTH):
        u = u + convw_ref[CONV_WIDTH - 1 - j:CONV_WIDTH - j, :] * rxbuf[SUBLANES - j:SUBLANES - j + T, :]
    rxbuf[0:SUBLANES, :] = rxbuf[T:T + SUBLANES, :]

    qk = proj(C_Q, C_Q + 2 * GLA_DK)
    q = qk[:, :GLA_DK] * (GLA_HEAD_K ** -0.5)
    k = qk[:, GLA_DK:]
    alr = proj(0, LANES, walr_ref, balr_ref)

    x1 = _layer_norm(zbuf[...], ln1g_ref[...], ln1b_ref[...])
    x1_ref[...] = x1
    info, wts, total = _route_tile(x1.astype(BF16), wr_ref[...], br_ref[...])
    info_ref[...] = info
    wts_ref[...] = wts
    cnt_ref[0] = jnp.broadcast_to(total, (N_EXPERTS, LANES))
    pk = _pack_bf16_pairs(x1)
    for c in range(PK_TILES):
        xpk_ref[pl.ds(c, T, stride=TOK_ROWS), :] = pk[:, c * LANES:(c + 1) * LANES]
    first_expert = jnp.broadcast_to(info[0:1, :].astype(F32), (LANES, T)).T.astype(jnp.int32)
    token = lax.broadcasted_iota(jnp.int32, (T, LANES), 0) + jnp.maximum(step - 1, 0) * T
    lane = lax.broadcasted_iota(jnp.int32, (T, LANES), 1)
    meta = jnp.where(lane == META_EXPERT, first_expert,
                     jnp.where(lane == META_TOKEN, token, jnp.where(lane == META_VALID, 1, 0)))
    xpk_ref[pl.ds(PK_TILES, T, stride=TOK_ROWS), :] = pltpu.bitcast(meta, jnp.uint32)

    r_parts, i_parts = [], []
    for p in range(RNN_BLOCKS // 2):
        up = u[:, 256 * p:256 * (p + 1)].astype(BF16)
        gp = _dot(up, wgate_ref[p])
        r_parts.append(gp[:, :256])
        i_parts.append(gp[:, 256:])
    r = _sigmoid(jnp.concatenate(r_parts, axis=1) + bgate_ref[:, :RNN_WIDTH])
    ig = _sigmoid(jnp.concatenate(i_parts, axis=1) + bgate_ref[:, RNN_WIDTH:])
    v = proj(C_V, C_V + GLA_DV)
    ry = proj(C_RY, C_RY + RNN_WIDTH)

    z = _dot(alr.astype(BF16), wa2_ref[...]) + ba2_ref[...]
    la = -_softplus(-z) * (1.0 / GLA_TAU)
    ri = lax.broadcasted_iota(jnp.int32, (T, T), 0)
    ci = lax.broadcasted_iota(jnp.int32, (T, T), 1)
    chunk_start = (ri >> 6) << 6
    tri = jnp.where((ci <= ri) & (ci >= chunk_start), 1.0, 0.0).astype(BF16)
    la_hi = la.astype(BF16)
    la_lo = (la - la_hi.astype(F32)).astype(BF16)
    bcum = _dot(tri, la_hi) + _dot(tri, la_lo)

    g = proj(C_G, C_G + GLA_DV)

    log_a = (-LRU_C) * r * _softplus(-lam_ref[...])
    a = jnp.exp(log_a)
    m2 = -jnp.tanh(log_a) * (1.0 + a * a)
    bv = jnp.where(m2 > 0.0, m2 * lax.rsqrt(m2), 0.0) * (ig * u)

    cr = lax.broadcasted_iota(jnp.int32, (GLA_CHUNK, GLA_CHUNK), 0)
    cc = lax.broadcasted_iota(jnp.int32, (GLA_CHUNK, GLA_CHUNK), 1)
    causal = cr >= cc
    n_chunks = T // GLA_CHUNK
    heads = [(slice(hd * GLA_HEAD_K, (hd + 1) * GLA_HEAD_K), slice(hd * GLA_HEAD_V, (hd + 1) * GLA_HEAD_V))
             for hd in range(GLA_HEADS)]
    qd_c, ki_c, ke_c, vv_c, dec_c = [], [], [], [], []
    for c in range(n_chunks):
        r0 = c * GLA_CHUNK
        bc = bcum[r0:r0 + GLA_CHUNK, :]
        bl = bcum[r0 + GLA_CHUNK - 1:r0 + GLA_CHUNK, :]
        kc = k[r0:r0 + GLA_CHUNK, :]
        qd_c.append((q[r0:r0 + GLA_CHUNK, :] * jnp.exp(bc)).astype(BF16))
        ki_c.append((kc * jnp.exp(-bc)).astype(BF16))
        ke_c.append((kc * jnp.exp(bl - bc)).astype(BF16))
        vv_c.append(v[r0:r0 + GLA_CHUNK, :].astype(BF16))
        dec_c.append(jnp.exp(bl))
    scores = [[jnp.where(causal, _dot_nt(qd_c[c][:, ks], ki_c[c][:, ks]), 0.0).astype(BF16) for ks, _ in heads]
              for c in range(n_chunks)]
    intra = [[_dot(scores[c][hd], vv_c[c][:, vs]) for hd, (_, vs) in enumerate(heads)] for c in range(n_chunks)]
    incr = [[_dot_tn(vv_c[c][:, vs], ke_c[c][:, ks]) for ks, vs in heads] for c in range(n_chunks)]

    n_groups = T // SUBLANES
    sub = lax.broadcasted_iota(jnp.int32, (n_groups, SUBLANES, RNN_WIDTH), 1)
    sa = a.reshape(n_groups, SUBLANES, RNN_WIDTH)
    sb = bv.reshape(n_groups, SUBLANES, RNN_WIDTH)
    for s in (1, 2, 4):
        keep = sub >= s
        sb = sa * jnp.where(keep, pltpu.roll(sb, s, 1), 0.0) + sb
        sa = sa * jnp.where(keep, pltpu.roll(sa, s, 1), 1.0)
    carry = hcar[0:1, :]
    for gi in range(n_groups):
        hg = sb[gi] + sa[gi] * carry
        hbuf[gi * SUBLANES:(gi + 1) * SUBLANES, :] = hg
        carry = hg[SUBLANES - 1:SUBLANES, :]
    hcar[0:1, :] = carry
    h = hbuf[...]
    ga = proj(C_GA, C_GA + D_MODEL, wbg_ref, bbg_ref)

    out_a = _dot((h * jax.nn.gelu(ry)).astype(BF16), wprnn_ref[:, :D_MODEL])
    gb = proj(C_GB, C_GB + D_MODEL, wbg_ref, bbg_ref)

    states = [st_ref[hd] for hd in range(GLA_HEADS)]
    o_chunks = []
    for c in range(n_chunks):
        o_heads = []
        for hd, (ks, _) in enumerate(heads):
            o_heads.append(intra[c][hd] + _dot_nt(qd_c[c][:, ks], states[hd].astype(BF16)))
            states[hd] = states[hd] * dec_c[c][:, ks] + incr[c][hd]
        o_chunks.append(jnp.concatenate(o_heads, axis=1))
    for hd in range(GLA_HEADS):
        st_ref[hd] = states[hd]
    o_all = jnp.concatenate(o_chunks, axis=0)

    o_parts = []
    for hd in range(GLA_HEADS):
        vs = slice(hd * GLA_HEAD_V, (hd + 1) * GLA_HEAD_V)
        oh = o_all[:, vs]
        ms = jnp.mean(oh * oh, axis=-1, keepdims=True)
        o_parts.append(oh * lax.rsqrt(ms + RMS_EPS) * gnorm_ref[:, vs])
    on = jnp.concatenate(o_parts, axis=1) * (g * _sigmoid(g))
    out_b = _dot(on.astype(BF16), wpgla_ref[:, :D_MODEL])

    merged = _sigmoid(ga) * out_a + _sigmoid(gb) * out_b
    y = _dot(merged.astype(BF16), wo_ref[:, :D_MODEL]) + bo_ref[...]
    zbuf[...] = DN_ALPHA * x + y


def _mixer(x, wmain, bmain, wbg, bbg, walr, balr, conv_w, conv_b, wgate, bgate, lam, wa2, ba2, gnorm,
           wprnn, wpgla, wo, bo, ln1g, ln1b, wr, br):
    B, S, _ = x.shape
    T = min(MIX_TILE, S)
    assert S % T == 0 and T % GLA_CHUNK == 0
    N = B * S
    nt = N // T
    mix_tile = lambda i: jnp.minimum(i, nt - 1)
    tail_tile = lambda i: jnp.maximum(i - 1, 0)
    weights = (wmain, bmain, wbg, bbg, walr, balr, conv_w, conv_b, wgate, bgate, lam, wa2, ba2, gnorm,
               wprnn, wpgla, wo, bo, ln1g, ln1b, wr, br)
    return pl.pallas_call(
        functools.partial(_mixer_kernel, tile=T, steps_per_seq=S // T),
        grid=(nt + 1,),
        in_specs=[pl.BlockSpec((T, D_MODEL), lambda i: (mix_tile(i), 0))] + [_const_spec(w.shape) for w in weights],
        out_specs=[pl.BlockSpec((T, D_MODEL), lambda i: (tail_tile(i), 0)),
                   pl.BlockSpec((T * TOK_ROWS, LANES), lambda i: (tail_tile(i), 0)),
                   pl.BlockSpec((SUBLANES, T), lambda i: (0, tail_tile(i))),
                   pl.BlockSpec((SUBLANES, T), lambda i: (0, tail_tile(i))),
                   pl.BlockSpec((1, N_EXPERTS, LANES), lambda i: (tail_tile(i), 0, 0))],
        out_shape=[jax.ShapeDtypeStruct((N, D_MODEL), F32),
                   jax.ShapeDtypeStruct((N * TOK_ROWS, LANES), jnp.uint32),
                   jax.ShapeDtypeStruct((SUBLANES, N), jnp.int32),
                   jax.ShapeDtypeStruct((SUBLANES, N), F32),
                   jax.ShapeDtypeStruct((nt, N_EXPERTS, LANES), jnp.int32)],
        scratch_shapes=[pltpu.VMEM((T + SUBLANES, RNN_WIDTH), F32),
                        pltpu.VMEM((SUBLANES, RNN_WIDTH), F32),
                        pltpu.VMEM((GLA_HEADS, GLA_HEAD_V, GLA_HEAD_K), F32),
                        pltpu.VMEM((T, RNN_WIDTH), F32),
                        pltpu.VMEM((T, D_MODEL), F32)],
        compiler_params=pltpu.CompilerParams(dimension_semantics=("arbitrary",), vmem_limit_bytes=VMEM_LIMIT),
        name="mixer",
    )(x.reshape(N, D_MODEL), *weights)


def _route_tile(x1b, wr, br):
    T = x1b.shape[0]
    logits = _dot_nt(wr, x1b) + br
    row8 = lax.broadcasted_iota(jnp.int32, (SUBLANES, T), 0)
    row8f = row8.astype(F32)
    neg = jnp.float32(-jnp.inf)
    first = lambda hit: jnp.min(jnp.where(hit, row8f, float(SUBLANES)), axis=0, keepdims=True)

    gl = jnp.where(row8 < N_GROUPS, logits[0:SUBLANES, :], neg)
    gmax = jnp.max(gl, axis=0, keepdims=True)
    grp = first(gl == gmax)
    p_grp = 1.0 / jnp.sum(jnp.exp(gl - gmax), axis=0, keepdims=True)

    e_sel = jnp.zeros((EXPERTS_PER_GROUP, T), F32)
    for gi in range(N_GROUPS):
        lo = SUBLANES + gi * EXPERTS_PER_GROUP
        e_sel = jnp.where(grp == float(gi), logits[lo:lo + EXPERTS_PER_GROUP, :], e_sel)
    m1 = jnp.max(e_sel, axis=0, keepdims=True)
    i1 = first(e_sel == m1)
    e_rest = jnp.where(row8f == i1, neg, e_sel)
    m2 = jnp.max(e_rest, axis=0, keepdims=True)
    i2 = first(e_rest == m2)
    e21 = jnp.exp(m2 - m1)
    p1 = 1.0 / (1.0 + e21)
    w0 = p_grp * p1
    w1 = p_grp * (e21 * p1)
    eid0 = (grp * EXPERTS_PER_GROUP + i1).astype(jnp.int32)
    eid1 = (grp * EXPERTS_PER_GROUP + i2).astype(jnp.int32)

    erow = lax.broadcasted_iota(jnp.int32, (N_EXPERTS, T), 0)
    oh0 = jnp.where(erow == eid0, 1.0, 0.0)
    oh1 = jnp.where(erow == eid1, 1.0, 0.0)
    both = oh0 + oh1
    ti = lax.broadcasted_iota(jnp.int32, (T, T), 0)
    tj = lax.broadcasted_iota(jnp.int32, (T, T), 1)
    before = jnp.where(ti < tj, 1.0, 0.0).astype(BF16)
    prior = _dot(both.astype(BF16), before)
    rank0 = jnp.sum(prior * oh0, axis=0, keepdims=True).astype(jnp.int32)
    rank1 = jnp.sum(prior * oh1, axis=0, keepdims=True).astype(jnp.int32)
    total = (prior[:, T - 1:T] + both[:, T - 1:T]).astype(jnp.int32)

    info = jnp.where(row8 == 0, eid0, jnp.where(row8 == 1, eid1, jnp.where(row8 == 2, rank0,
                     jnp.where(row8 == 3, rank1, 0))))
    return info, jnp.where(row8 == 0, w0, jnp.where(row8 == 1, w1, 0.0)), total


def _dispatch_kernel(zero_blk_ref, dest_ref, xpk_ref, xs_ref, zbuf, sem, zsem, *, tile):
    T = tile

    @pl.when(pl.program_id(0) == 0)
    def _():
        zbuf[...] = jnp.zeros_like(zbuf)

        def zero_copy(j):
            row = pl.multiple_of(jnp.maximum(zero_blk_ref[j], 0) * TOK_ROWS, MOE_ROWS * TOK_ROWS)
            return pltpu.make_async_copy(zbuf, xs_ref.at[pl.ds(row, MOE_ROWS * TOK_ROWS)], zsem)

        for j in range(2 * N_EXPERTS):
            @pl.when(zero_blk_ref[j] >= 0)
            def _():
                zero_copy(j).start()
        for j in range(2 * N_EXPERTS):
            @pl.when(zero_blk_ref[j] >= 0)
            def _():
                zero_copy(j).wait()

    for t in range(T):
        for kk in range(2):
            row = dest_ref[kk, t] * TOK_ROWS
            pltpu.make_async_copy(xpk_ref.at[pl.ds(t * TOK_ROWS, TOK_ROWS)], xs_ref.at[pl.ds(row, TOK_ROWS)],
                                  sem).start(priority=kk)
    for kk in range(2):
        pltpu.make_async_copy(xpk_ref, xs_ref.at[pl.ds(0, T * TOK_ROWS)], sem).wait()


def _dispatch(last_blk, dest, xpk, n_rows):
    N = xpk.shape[0] // TOK_ROWS
    T = min(DISPATCH_TILE, N)
    assert N % T == 0
    grid_spec = pltpu.PrefetchScalarGridSpec(
        num_scalar_prefetch=1,
        grid=(N // T,),
        in_specs=[pl.BlockSpec((2, T), lambda i, lb: (0, i), memory_space=pltpu.SMEM),
                  pl.BlockSpec((T * TOK_ROWS, LANES), lambda i, lb: (i, 0))],
        out_specs=pl.BlockSpec(memory_space=pl.ANY),
        scratch_shapes=[pltpu.VMEM((MOE_ROWS * TOK_ROWS, LANES), jnp.uint32),
                        pltpu.SemaphoreType.DMA(()), pltpu.SemaphoreType.DMA(())],
    )
    return pl.pallas_call(
        functools.partial(_dispatch_kernel, tile=T),
        grid_spec=grid_spec,
        out_shape=jax.ShapeDtypeStruct((n_rows * TOK_ROWS, LANES), jnp.uint32),
        compiler_params=pltpu.CompilerParams(dimension_semantics=("arbitrary",), has_side_effects=True),
        name="dispatch",
    )(last_blk, dest, xpk)


def _experts_kernel(blk_e_ref, nblk_ref, run_first_ref, run_idx_ref, next_e_ref,
                    xs_ref, w1_hbm, w3_hbm, w2_hbm, yt_ref,
                    w1b, w3b, w2b, w1f, w3f, w2f, stage, ids_v, ids_s, row_sem, ids_sem, w_sem, *, n_tokens):
    i = pl.program_id(0)
    R = MOE_ROWS
    SR = R * PK_TILES
    n_used = nblk_ref[0]
    spare = 2 * n_tokens
    slot = i % N_STAGE
    prev = (i + N_STAGE - 1) % N_STAGE
    prev2 = (i + N_STAGE - 2) % N_STAGE

    def ids_copy(s):
        return pltpu.make_async_copy(ids_v.at[pl.ds(s, 1)], ids_s.at[pl.ds(s, 1)], ids_sem)

    def row_copy(s, r, dst_row):
        src = pl.multiple_of((s * R + r) * PK_TILES, PK_TILES)
        dst = pl.multiple_of(dst_row * PK_TILES, PK_TILES)
        return pltpu.make_async_copy(stage.at[pl.ds(src, PK_TILES)], yt_ref.at[pl.ds(dst, PK_TILES)],
                                     row_sem.at[s]).start(priority=1)

    def drain_rows(s):
        pltpu.make_async_copy(stage.at[pl.ds(0, SR)], yt_ref.at[pl.ds(0, SR)], row_sem.at[s]).wait()

    def send_rows(s):
        for r in range(R):
            row_copy(s, r, ids_s[s, r])

    @pl.when(i == 0)
    def _():
        stage[...] = jnp.zeros_like(stage)
        ids_v[...] = (spare + lax.broadcasted_iota(jnp.int32, (SUBLANES, R), 0) * R
                      + lax.broadcasted_iota(jnp.int32, (SUBLANES, R), 1))
        ids_copy(N_STAGE - 1).start()
        for s in range(N_STAGE - 1):
            for r in range(R):
                row_copy(s, r, spare + s * R + r)

    def weight_copies(e, buf):
        return [pltpu.make_async_copy(src.at[e], dst.at[buf], w_sem.at[buf])
                for src, dst in ((w1_hbm, w1f), (w3_hbm, w3f), (w2_hbm, w2f))]

    @pl.when(i == 0)
    def _():
        for cp in weight_copies(blk_e_ref[0], 0):
            cp.start()

    @pl.when(run_first_ref[i] != 0)
    def _():
        buf = run_idx_ref[i] % 2
        for cp in weight_copies(blk_e_ref[i], buf):
            cp.wait()
        w1b[...] = w1f[buf].astype(BF16)
        w3b[...] = w3f[buf].astype(BF16)
        w2b[...] = w2f[buf].astype(BF16)

        @pl.when(next_e_ref[i] >= 0)
        def _():
            for cp in weight_copies(next_e_ref[i], 1 - buf):
                cp.start()

    @pl.when(i < n_used)
    def _():
        ids_copy(prev).wait()
        drain_rows(slot)
        send_rows(prev)

        meta = pltpu.bitcast(xs_ref[pl.ds(PK_TILES, R, stride=TOK_ROWS), :], jnp.int32).astype(F32).T
        first_expert = meta[META_EXPERT:META_EXPERT + 1, :]
        token = meta[META_TOKEN:META_TOKEN + 1, :]
        valid = meta[META_VALID:META_VALID + 1, :]
        second = jnp.where(first_expert != blk_e_ref[i].astype(F32), 1.0, 0.0)
        pad_row = (spare + slot * R + lax.broadcasted_iota(jnp.int32, (1, R), 1)).astype(F32)
        ids_v[pl.ds(slot, 1), :] = jnp.where(valid != 0.0, 2.0 * token + second, pad_row).astype(jnp.int32)
        ids_copy(slot).start()

        u = jnp.concatenate([xs_ref[pl.ds(c, R, stride=TOK_ROWS), :] for c in range(PK_TILES)], axis=1)
        xb = _unpack_bf16_pairs(u).astype(BF16)
        h1 = _dot(xb, w1b[...])
        h3 = _dot(xb, w3b[...])
        hact = (h1 * _sigmoid(h1)) * h3
        y = _dot(hact.astype(BF16), w2b[...])
        pk = _pack_bf16_pairs(y)
        for c in range(PK_TILES):
            stage[pl.ds(slot * SR + c, R, stride=PK_TILES), :] = pk[:, c * LANES:(c + 1) * LANES]

    @pl.when(i == n_used)
    def _():
        ids_copy(prev).wait()
        drain_rows(slot)
        send_rows(prev)
        drain_rows(prev2)
        drain_rows(prev)


def _experts(blk_e, nblk, run_first, run_idx, next_e, xs, w1, w3, w2, n_tokens):
    P = xs.shape[0] // TOK_ROWS
    nb = P // MOE_ROWS
    assert blk_e.shape[0] == nb + 1
    grid_spec = pltpu.PrefetchScalarGridSpec(
        num_scalar_prefetch=5,
        grid=(nb + 1,),
        in_specs=[pl.BlockSpec((MOE_ROWS * TOK_ROWS, LANES), lambda i, be, n, *_: (jnp.minimum(i, n[0] - 1), 0)),
                  pl.BlockSpec(memory_space=pl.ANY), pl.BlockSpec(memory_space=pl.ANY),
                  pl.BlockSpec(memory_space=pl.ANY)],
        out_specs=pl.BlockSpec(memory_space=pl.ANY),
        scratch_shapes=[pltpu.VMEM((D_MODEL, EXPERT_FF), BF16), pltpu.VMEM((D_MODEL, EXPERT_FF), BF16),
                        pltpu.VMEM((EXPERT_FF, D_MODEL), BF16),
                        pltpu.VMEM((2, D_MODEL, EXPERT_FF), F32), pltpu.VMEM((2, D_MODEL, EXPERT_FF), F32),
                        pltpu.VMEM((2, EXPERT_FF, D_MODEL), F32),
                        pltpu.VMEM((N_STAGE * MOE_ROWS * PK_TILES, LANES), jnp.uint32),
                        pltpu.VMEM((SUBLANES, MOE_ROWS), jnp.int32),
                        pltpu.SMEM((N_STAGE, MOE_ROWS), jnp.int32),
                        pltpu.SemaphoreType.DMA((N_STAGE,)), pltpu.SemaphoreType.DMA(()),
                        pltpu.SemaphoreType.DMA((2,))],
    )
    return pl.pallas_call(
        functools.partial(_experts_kernel, n_tokens=n_tokens),
        grid_spec=grid_spec,
        out_shape=jax.ShapeDtypeStruct(((2 * n_tokens + N_STAGE * MOE_ROWS) * PK_TILES, LANES), jnp.uint32),
        compiler_params=pltpu.CompilerParams(dimension_semantics=("arbitrary",), vmem_limit_bytes=VMEM_LIMIT,
                                             has_side_effects=True),
        name="experts",
    )(blk_e, nblk, run_first, run_idx, next_e, xs, w1, w3, w2)


def _combine_kernel(x1_ref, wts_ref, yt_ref, g_ref, b_ref, out_ref, *, tile):
    T = tile
    wpad = jnp.concatenate([wts_ref[...], jnp.zeros((LANES - SUBLANES, T), F32)], axis=0)
    wt = wpad.T
    slot = lambda s: _unpack_bf16_pairs(jnp.concatenate(
        [yt_ref[pl.ds(s * PK_TILES + c, T, stride=2 * PK_TILES), :] for c in range(PK_TILES)], axis=1))
    y = wt[:, 0:1] * slot(0) + wt[:, 1:2] * slot(1)
    out_ref[...] = _layer_norm(DN_ALPHA * x1_ref[...] + y, g_ref[...], b_ref[...])


def _combine(x1f, wts, yt, g, b):
    N = x1f.shape[0]
    T = min(COMBINE_TILE, N)
    assert N % T == 0
    return pl.pallas_call(
        functools.partial(_combine_kernel, tile=T),
        grid=(N // T,),
        in_specs=[pl.BlockSpec((T, D_MODEL), lambda i: (i, 0)),
                  pl.BlockSpec((SUBLANES, T), lambda i: (0, i)),
                  pl.BlockSpec((T * 2 * PK_TILES, LANES), lambda i: (i, 0)),
                  _const_spec(g.shape), _const_spec(b.shape)],
        out_specs=pl.BlockSpec((T, D_MODEL), lambda i: (i, 0)),
        out_shape=jax.ShapeDtypeStruct((N, D_MODEL), F32),
        compiler_params=pltpu.CompilerParams(dimension_semantics=("arbitrary",)),
        name="combine",
    )(x1f, wts, yt, g, b)


def _odd_tiles(w):
    return jnp.pad(w, ((0, 0), (0, LANES))).astype(BF16)


def _pack_mixer_weights(w_in, b_in, rg_w_a, rg_w_x, rg_b_a, rg_b_x, gla_w_a2):
    wmain = _odd_tiles(w_in[:, :C_MAIN_END])
    wbg = _odd_tiles(w_in[:, C_GATES_START:])
    walr = jnp.pad(w_in[:, C_MAIN_END:C_GATES_START], ((0, 0), (0, LANES - GLA_RANK))).astype(BF16)
    bmain = b_in[None, :C_MAIN_END]
    bbg = b_in[None, C_GATES_START:]
    balr = jnp.pad(b_in[None, C_MAIN_END:C_GATES_START], ((0, 0), (0, LANES - GLA_RANK)))
    zero = jnp.zeros((RNN_BLOCK_W, RNN_BLOCK_W), w_in.dtype)
    tiles = []
    for p in range(RNN_BLOCKS // 2):
        top = jnp.concatenate([rg_w_a[2 * p], zero, rg_w_x[2 * p], zero], axis=1)
        bot = jnp.concatenate([zero, rg_w_a[2 * p + 1], zero, rg_w_x[2 * p + 1]], axis=1)
        tiles.append(jnp.concatenate([top, bot], axis=0))
    wgate = jnp.stack(tiles).astype(BF16)
    bgate = jnp.concatenate([rg_b_a, rg_b_x])[None, :]
    wa2 = jnp.concatenate([gla_w_a2, jnp.zeros((LANES - GLA_RANK, GLA_DK), gla_w_a2.dtype)], axis=0).astype(BF16)
    return (wmain, bmain, wbg, bbg, walr, balr), wgate, bgate, wa2


def _layer(x, w_in, b_in, conv_w, conv_b, rg_w_a, rg_b_a, rg_w_x, rg_b_x, rg_lambda, gla_w_a2, gla_b_a,
           gla_norm_g, w_proj_rnn, w_proj_gla, w_o, b_o, ln1_g, ln1_b, router_w_group, router_b_group,
           router_w_expert, router_b_expert, exp_w1, exp_w3, exp_w2, ln2_g, ln2_b):
    B, S, _ = x.shape
    N = B * S
    row = lambda p: p[None, :]

    w_slices, wgate, bgate, wa2 = _pack_mixer_weights(w_in, b_in, rg_w_a, rg_w_x, rg_b_a, rg_b_x, gla_w_a2)
    wr = jnp.concatenate([router_w_group.T, jnp.zeros((SUBLANES - N_GROUPS, D_MODEL), F32), router_w_expert.T],
                         axis=0).astype(BF16)
    br = jnp.concatenate([router_b_group, jnp.zeros((SUBLANES - N_GROUPS,), F32), router_b_expert])[:, None]
    x1f, xpk, info, wts, tcnt = _mixer(
        x, *w_slices, conv_w, row(conv_b), wgate, bgate, row(rg_lambda), wa2, row(gla_b_a), row(gla_norm_g),
        _odd_tiles(w_proj_rnn), _odd_tiles(w_proj_gla), _odd_tiles(w_o), row(b_o), row(ln1_g), row(ln1_b), wr, br)

    tcnt = tcnt[:, :, 0]
    nt = tcnt.shape[0]
    tot = jnp.sum(tcnt, axis=0)
    pcount = (tot + MOE_ROWS - 1) // MOE_ROWS * MOE_ROWS
    pend = jnp.cumsum(pcount)
    base = (pend - pcount)[None, :] + jnp.cumsum(tcnt, axis=0) - tcnt
    base_tok = jnp.repeat(base.T, N // nt, axis=1)
    experts_col = jnp.arange(N_EXPERTS, dtype=jnp.int32)[:, None, None]
    dest = jnp.sum(jnp.where(info[None, 0:2] == experts_col, base_tok[:, None, :], 0), axis=0) + info[2:4]
    nb = -(-(2 * N) // MOE_ROWS) + N_EXPERTS
    P = nb * MOE_ROWS
    nblk = (pend[-1] // MOE_ROWS).astype(jnp.int32)
    blk_start = jnp.minimum(jnp.arange(nb + 1, dtype=jnp.int32), nblk - 1) * MOE_ROWS
    blk_e = jnp.sum((blk_start[:, None] >= pend[None, :]).astype(jnp.int32), axis=1)
    blk_e = jnp.minimum(blk_e, N_EXPERTS - 1)

    last_blk = jnp.where(tot > 0, pend - MOE_ROWS, -1)
    tail_blk = nblk + jnp.arange(N_EXPERTS, dtype=jnp.int32)
    tail_blk = jnp.where(tail_blk < nb, tail_blk * MOE_ROWS, -1)
    xs = _dispatch(jnp.concatenate([last_blk, tail_blk]).astype(jnp.int32), dest, xpk, P)
    blk_i = jnp.arange(nb + 1, dtype=jnp.int32)
    experts_row = jnp.arange(N_EXPERTS, dtype=jnp.int32)
    run_first = ((blk_i < nblk) & ((blk_i == 0) | (blk_e != jnp.roll(blk_e, 1)))).astype(jnp.int32)
    run_idx = jnp.sum(jnp.where(blk_i[None, :] <= blk_i[:, None], run_first[None, :], 0), axis=1) - 1
    later = (experts_row[None, :] > experts_row[:, None]) & (tot[None, :] > 0)
    next_nonempty = jnp.min(jnp.where(later, experts_row[None, :], N_EXPERTS), axis=1)
    next_nonempty = jnp.where(next_nonempty < N_EXPERTS, next_nonempty, -1)
    next_e = jnp.sum(jnp.where(blk_e[:, None] == experts_row[None, :], next_nonempty[None, :], 0), axis=1)
    yt = _experts(blk_e, nblk[None], run_first, run_idx.astype(jnp.int32), next_e.astype(jnp.int32),
                  xs, exp_w1, exp_w3, exp_w2, N)
    out = _combine(x1f, wts, yt, row(ln2_g), row(ln2_b))
    return out.reshape(B, S, D_MODEL)


def kernel(x, w_in, b_in, conv_w, conv_b, rg_w_a, rg_b_a, rg_w_x, rg_b_x, rg_lambda, gla_w_a2, gla_b_a, gla_norm_g, w_proj_rnn, w_proj_gla, w_o, b_o, ln1_g, ln1_b, router_w_group, router_b_group, router_w_expert, router_b_expert, exp_w1, exp_w3, exp_w2, ln2_g, ln2_b):
    h = x
    for l in range(w_in.shape[0]):
        h = _layer(h, w_in[l], b_in[l], conv_w[l], conv_b[l], rg_w_a[l], rg_b_a[l], rg_w_x[l], rg_b_x[l],
                   rg_lambda[l], gla_w_a2[l], gla_b_a[l], gla_norm_g[l], w_proj_rnn[l], w_proj_gla[l], w_o[l],
                   b_o[l], ln1_g[l], ln1_b[l], router_w_group[l], router_b_group[l], router_w_expert[l],
                   router_b_expert[l], exp_w1[l], exp_w3[l], exp_w2[l], ln2_g[l], ln2_b[l])
    return h
```

```python
import functools

import jax
import jax.numpy as jnp
from jax import lax
from jax.experimental import pallas as pl
from jax.experimental.pallas import tpu as pltpu

F32 = jnp.float32
BF16 = jnp.bfloat16

D_MODEL = 1024
RNN_WIDTH = 1024
RNN_BLOCKS = 8
RNN_BLOCK_W = RNN_WIDTH // RNN_BLOCKS
CONV_WIDTH = 4
LRU_C = 8.0
GLA_HEADS = 4
GLA_DK = D_MODEL // 2
GLA_DV = D_MODEL
GLA_HEAD_K = GLA_DK // GLA_HEADS
GLA_HEAD_V = GLA_DV // GLA_HEADS
GLA_RANK = 16
GLA_TAU = 16.0
GLA_CHUNK = 64
N_GROUPS = 4
EXPERTS_PER_GROUP = 8
N_EXPERTS = N_GROUPS * EXPERTS_PER_GROUP
EXPERT_FF = 512
DN_ALPHA = 2.0 ** 0.25
LN_EPS = 1e-5
RMS_EPS = 1e-6

LANES = 128
SUBLANES = 8
VMEM_LIMIT = 56 * 1024 * 1024

C_RX, C_RY, C_Q, C_K, C_V, C_G = 0, 1024, 2048, 2560, 3072, 4096
C_GA, C_GB = 0, 1024
C_MAIN_END = 5120
C_GATES_START = C_MAIN_END + GLA_RANK

MIX_TILE = 256
MOE_ROWS = 512
DISPATCH_TILE = 1024
COMBINE_TILE = 1024
N_STAGE = 3
GATE_PAIR_W = 2 * RNN_BLOCK_W
PK_TILES = D_MODEL // 2 // LANES
TOK_ROWS = PK_TILES + 1
META_EXPERT, META_TOKEN, META_VALID = 0, 1, 2


def _sigmoid(v):
    return 1.0 / (1.0 + jnp.exp(-v))


def _softplus(v):
    return jnp.maximum(v, 0.0) + jnp.log1p(jnp.exp(-jnp.abs(v)))


def _layer_norm(v, g, b):
    mu = jnp.mean(v, axis=-1, keepdims=True)
    c = v - mu
    var = jnp.mean(c * c, axis=-1, keepdims=True)
    return c * lax.rsqrt(var + LN_EPS) * g + b


def _dot(a, b):
    return jnp.dot(a, b, preferred_element_type=F32)


def _dot_nt(a, b):
    return lax.dot_general(a, b, (((1,), (1,)), ((), ())), preferred_element_type=F32)


def _dot_tn(a, b):
    return lax.dot_general(a, b, (((0,), (0,)), ((), ())), preferred_element_type=F32)


def _pack_bf16_pairs(v):
    half = v.shape[1] // 2
    bits = pltpu.bitcast(v.astype(BF16).astype(F32), jnp.uint32)
    return (bits[:, :half] >> 16) | bits[:, half:]


def _unpack_bf16_pairs(u):
    lo = pltpu.bitcast(u << 16, F32)
    hi = pltpu.bitcast(u & jnp.uint32(0xFFFF0000), F32)
    return jnp.concatenate([lo, hi], axis=1)


def _const_spec(shape):
    nd = len(shape)
    return pl.BlockSpec(shape, lambda *_: (0,) * nd, pipeline_mode=pl.Buffered(1))


def _mixer_kernel(x_ref, wmain_ref, bmain_ref, wbg_ref, bbg_ref, walr_ref, balr_ref,
                  convw_ref, convb_ref, wgate_ref, bgate_ref, lam_ref,
                  wa2_ref, ba2_ref, gnorm_ref, wprnn_ref, wpgla_ref, wo_ref, bo_ref, ln1g_ref, ln1b_ref,
                  wr_ref, br_ref,
                  x1_ref, xpk_ref, info_ref, wts_ref, cnt_ref, rxbuf, hcar, st_ref, hbuf, zbuf,
                  *, tile, steps_per_seq):
    T = tile
    step = pl.program_id(0)

    @pl.when(step == 0)
    def _():
        zbuf[...] = jnp.zeros_like(zbuf)

    @pl.when(step % steps_per_seq == 0)
    def _():
        rxbuf[0:SUBLANES, :] = jnp.zeros((SUBLANES, RNN_WIDTH), F32)
        hcar[...] = jnp.zeros_like(hcar)
        st_ref[...] = jnp.zeros_like(st_ref)

    x = x_ref[...]
    xb = x.astype(BF16)

    def proj(c0, c1, w_ref=wmain_ref, b_ref=bmain_ref):
        return _dot(xb, w_ref[:, c0:c1]) + b_ref[:, c0:c1]

    rx = proj(C_RX, C_RX + RNN_WIDTH)
    rxbuf[SUBLANES:SUBLANES + T, :] = rx
    u = convb_ref[...] + convw_ref[CONV_WIDTH - 1:CONV_WIDTH, :] * rx
    for j in range(1, CONV_WIDTH):
        u = u + convw_ref[CONV_WIDTH - 1 - j:CONV_WIDTH - j, :] * rxbuf[SUBLANES - j:SUBLANES - j + T, :]
    rxbuf[0:SUBLANES, :] = rxbuf[T:T + SUBLANES, :]

    qk = proj(C_Q, C_Q + 2 * GLA_DK)
    q = qk[:, :GLA_DK] * (GLA_HEAD_K ** -0.5)
    k = qk[:, GLA_DK:]
    alr = proj(0, LANES, walr_ref, balr_ref)

    x1 = _layer_norm(zbuf[...], ln1g_ref[...], ln1b_ref[...])
    x1_ref[...] = x1
    info, wts, total = _route_tile(x1.astype(BF16), wr_ref[...], br_ref[...])
    info_ref[...] = info
    wts_ref[...] = wts
    cnt_ref[0] = jnp.broadcast_to(total, (N_EXPERTS, LANES))
    pk = _pack_bf16_pairs(x1)
    for c in range(PK_TILES):
        xpk_ref[pl.ds(c, T, stride=TOK_ROWS), :] = pk[:, c * LANES:(c + 1) * LANES]
    first_expert = jnp.broadcast_to(info[0:1, :].astype(F32), (LANES, T)).T.astype(jnp.int32)
    token = lax.broadcasted_iota(jnp.int32, (T, LANES), 0) + jnp.maximum(step - 1, 0) * T
    lane = lax.broadcasted_iota(jnp.int32, (T, LANES), 1)
    meta = jnp.where(lane == META_EXPERT, first_expert,
                     jnp.where(lane == META_TOKEN, token, jnp.where(lane == META_VALID, 1, 0)))
    xpk_ref[pl.ds(PK_TILES, T, stride=TOK_ROWS), :] = pltpu.bitcast(meta, jnp.uint32)

    r_parts, i_parts = [], []
    for p in range(RNN_BLOCKS // 2):
        up = u[:, GATE_PAIR_W * p:GATE_PAIR_W * (p + 1)].astype(BF16)
        gp = _dot(up, wgate_ref[p])
        r_parts.append(gp[:, :GATE_PAIR_W])
        i_parts.append(gp[:, GATE_PAIR_W:])
    r = _sigmoid(jnp.concatenate(r_parts, axis=1) + bgate_ref[:, :RNN_WIDTH])
    ig = _sigmoid(jnp.concatenate(i_parts, axis=1) + bgate_ref[:, RNN_WIDTH:])
    v = proj(C_V, C_V + GLA_DV)
    ry = proj(C_RY, C_RY + RNN_WIDTH)

    z = _dot(alr.astype(BF16), wa2_ref[...]) + ba2_ref[...]
    la = -_softplus(-z) * (1.0 / GLA_TAU)
    ri = lax.broadcasted_iota(jnp.int32, (T, T), 0)
    ci = lax.broadcasted_iota(jnp.int32, (T, T), 1)
    chunk_start = ri - (ri & (GLA_CHUNK - 1))
    tri = jnp.where((ci <= ri) & (ci >= chunk_start), 1.0, 0.0).astype(BF16)
    la_hi = la.astype(BF16)
    la_lo = (la - la_hi.astype(F32)).astype(BF16)
    bcum = _dot(tri, la_hi) + _dot(tri, la_lo)

    g = proj(C_G, C_G + GLA_DV)

    log_a = (-LRU_C) * r * _softplus(-lam_ref[...])
    a = jnp.exp(log_a)
    m2 = -jnp.tanh(log_a) * (1.0 + a * a)
    bv = jnp.where(m2 > 0.0, m2 * lax.rsqrt(m2), 0.0) * (ig * u)

    cr = lax.broadcasted_iota(jnp.int32, (GLA_CHUNK, GLA_CHUNK), 0)
    cc = lax.broadcasted_iota(jnp.int32, (GLA_CHUNK, GLA_CHUNK), 1)
    causal = cr >= cc
    n_chunks = T // GLA_CHUNK
    heads = [(slice(hd * GLA_HEAD_K, (hd + 1) * GLA_HEAD_K), slice(hd * GLA_HEAD_V, (hd + 1) * GLA_HEAD_V))
             for hd in range(GLA_HEADS)]
    qd_c, ki_c, ke_c, vv_c, dec_c = [], [], [], [], []
    for c in range(n_chunks):
        r0 = c * GLA_CHUNK
        bc = bcum[r0:r0 + GLA_CHUNK, :]
        bl = bcum[r0 + GLA_CHUNK - 1:r0 + GLA_CHUNK, :]
        kc = k[r0:r0 + GLA_CHUNK, :]
        qd_c.append((q[r0:r0 + GLA_CHUNK, :] * jnp.exp(bc)).astype(BF16))
        ki_c.append((kc * jnp.exp(-bc)).astype(BF16))
        ke_c.append((kc * jnp.exp(bl - bc)).astype(BF16))
        vv_c.append(v[r0:r0 + GLA_CHUNK, :].astype(BF16))
        dec_c.append(jnp.exp(bl))
    scores = [[jnp.where(causal, _dot_nt(qd_c[c][:, ks], ki_c[c][:, ks]), 0.0).astype(BF16) for ks, _ in heads]
              for c in range(n_chunks)]
    intra = [[_dot(scores[c][hd], vv_c[c][:, vs]) for hd, (_, vs) in enumerate(heads)] for c in range(n_chunks)]
    incr = [[_dot_tn(vv_c[c][:, vs], ke_c[c][:, ks]) for ks, vs in heads] for c in range(n_chunks)]

    n_groups = T // SUBLANES
    sub = lax.broadcasted_iota(jnp.int32, (n_groups, SUBLANES, RNN_WIDTH), 1)
    sa = a.reshape(n_groups, SUBLANES, RNN_WIDTH)
    sb = bv.reshape(n_groups, SUBLANES, RNN_WIDTH)
    for s in (1, 2, 4):
        keep = sub >= s
        sb = sa * jnp.where(keep, pltpu.roll(sb, s, 1), 0.0) + sb
        sa = sa * jnp.where(keep, pltpu.roll(sa, s, 1), 1.0)
    carry = hcar[0:1, :]
    for gi in range(n_groups):
        hg = sb[gi] + sa[gi] * carry
        hbuf[gi * SUBLANES:(gi + 1) * SUBLANES, :] = hg
        carry = hg[SUBLANES - 1:SUBLANES, :]
    hcar[0:1, :] = carry
    h = hbuf[...]
    ga = proj(C_GA, C_GA + D_MODEL, wbg_ref, bbg_ref)

    out_a = _dot((h * jax.nn.gelu(ry)).astype(BF16), wprnn_ref[:, :D_MODEL])
    gb = proj(C_GB, C_GB + D_MODEL, wbg_ref, bbg_ref)

    states = [st_ref[hd] for hd in range(GLA_HEADS)]
    o_chunks = []
    for c in range(n_chunks):
        o_heads = []
        for hd, (ks, _) in enumerate(heads):
            o_heads.append(intra[c][hd] + _dot_nt(qd_c[c][:, ks], states[hd].astype(BF16)))
            states[hd] = states[hd] * dec_c[c][:, ks] + incr[c][hd]
        o_chunks.append(jnp.concatenate(o_heads, axis=1))
    for hd in range(GLA_HEADS):
        st_ref[hd] = states[hd]
    o_all = jnp.concatenate(o_chunks, axis=0)

    o_parts = []
    for hd in range(GLA_HEADS):
        vs = slice(hd * GLA_HEAD_V, (hd + 1) * GLA_HEAD_V)
        oh = o_all[:, vs]
        ms = jnp.mean(oh * oh, axis=-1, keepdims=True)
        o_parts.append(oh * lax.rsqrt(ms + RMS_EPS) * gnorm_ref[:, vs])
    on = jnp.concatenate(o_parts, axis=1) * (g * _sigmoid(g))
    out_b = _dot(on.astype(BF16), wpgla_ref[:, :D_MODEL])

    merged = _sigmoid(ga) * out_a + _sigmoid(gb) * out_b
    y = _dot(merged.astype(BF16), wo_ref[:, :D_MODEL]) + bo_ref[...]
    zbuf[...] = DN_ALPHA * x + y


def _mixer(x, wmain, bmain, wbg, bbg, walr, balr, conv_w, conv_b, wgate, bgate, lam, wa2, ba2, gnorm,
           wprnn, wpgla, wo, bo, ln1g, ln1b, wr, br):
    B, S, _ = x.shape
    T = min(MIX_TILE, S)
    assert S % T == 0 and T % GLA_CHUNK == 0
    N = B * S
    nt = N // T
    mix_tile = lambda i: jnp.minimum(i, nt - 1)
    tail_tile = lambda i: jnp.maximum(i - 1, 0)
    weights = (wmain, bmain, wbg, bbg, walr, balr, conv_w, conv_b, wgate, bgate, lam, wa2, ba2, gnorm,
               wprnn, wpgla, wo, bo, ln1g, ln1b, wr, br)
    return pl.pallas_call(
        functools.partial(_mixer_kernel, tile=T, steps_per_seq=S // T),
        grid=(nt + 1,),
        in_specs=[pl.BlockSpec((T, D_MODEL), lambda i: (mix_tile(i), 0))] + [_const_spec(w.shape) for w in weights],
        out_specs=[pl.BlockSpec((T, D_MODEL), lambda i: (tail_tile(i), 0)),
                   pl.BlockSpec((T * TOK_ROWS, LANES), lambda i: (tail_tile(i), 0)),
                   pl.BlockSpec((SUBLANES, T), lambda i: (0, tail_tile(i))),
                   pl.BlockSpec((SUBLANES, T), lambda i: (0, tail_tile(i))),
                   pl.BlockSpec((1, N_EXPERTS, LANES), lambda i: (tail_tile(i), 0, 0))],
        out_shape=[jax.ShapeDtypeStruct((N, D_MODEL), F32),
                   jax.ShapeDtypeStruct((N * TOK_ROWS, LANES), jnp.uint32),
                   jax.ShapeDtypeStruct((SUBLANES, N), jnp.int32),
                   jax.ShapeDtypeStruct((SUBLANES, N), F32),
                   jax.ShapeDtypeStruct((nt, N_EXPERTS, LANES), jnp.int32)],
        scratch_shapes=[pltpu.VMEM((T + SUBLANES, RNN_WIDTH), F32),
                        pltpu.VMEM((SUBLANES, RNN_WIDTH), F32),
                        pltpu.VMEM((GLA_HEADS, GLA_HEAD_V, GLA_HEAD_K), F32),
                        pltpu.VMEM((T, RNN_WIDTH), F32),
                        pltpu.VMEM((T, D_MODEL), F32)],
        compiler_params=pltpu.CompilerParams(dimension_semantics=("arbitrary",), vmem_limit_bytes=VMEM_LIMIT),
        name="mixer",
    )(x.reshape(N, D_MODEL), *weights)


def _route_tile(x1b, wr, br):
    T = x1b.shape[0]
    logits = _dot_nt(wr, x1b) + br
    row8 = lax.broadcasted_iota(jnp.int32, (SUBLANES, T), 0)
    row8f = row8.astype(F32)
    neg = jnp.float32(-jnp.inf)
    first = lambda hit: jnp.min(jnp.where(hit, row8f, float(SUBLANES)), axis=0, keepdims=True)

    gl = jnp.where(row8 < N_GROUPS, logits[0:SUBLANES, :], neg)
    gmax = jnp.max(gl, axis=0, keepdims=True)
    grp = first(gl == gmax)
    p_grp = 1.0 / jnp.sum(jnp.exp(gl - gmax), axis=0, keepdims=True)

    e_sel = jnp.zeros((EXPERTS_PER_GROUP, T), F32)
    for gi in range(N_GROUPS):
        lo = SUBLANES + gi * EXPERTS_PER_GROUP
        e_sel = jnp.where(grp == float(gi), logits[lo:lo + EXPERTS_PER_GROUP, :], e_sel)
    m1 = jnp.max(e_sel, axis=0, keepdims=True)
    i1 = first(e_sel == m1)
    e_rest = jnp.where(row8f == i1, neg, e_sel)
    m2 = jnp.max(e_rest, axis=0, keepdims=True)
    i2 = first(e_rest == m2)
    e21 = jnp.exp(m2 - m1)
    p1 = 1.0 / (1.0 + e21)
    w0 = p_grp * p1
    w1 = p_grp * (e21 * p1)
    eid0 = (grp * EXPERTS_PER_GROUP + i1).astype(jnp.int32)
    eid1 = (grp * EXPERTS_PER_GROUP + i2).astype(jnp.int32)

    erow = lax.broadcasted_iota(jnp.int32, (N_EXPERTS, T), 0)
    oh0 = jnp.where(erow == eid0, 1.0, 0.0)
    oh1 = jnp.where(erow == eid1, 1.0, 0.0)
    both = oh0 + oh1
    ti = lax.broadcasted_iota(jnp.int32, (T, T), 0)
    tj = lax.broadcasted_iota(jnp.int32, (T, T), 1)
    before = jnp.where(ti < tj, 1.0, 0.0).astype(BF16)
    prior = _dot(both.astype(BF16), before)
    rank0 = jnp.sum(prior * oh0, axis=0, keepdims=True).astype(jnp.int32)
    rank1 = jnp.sum(prior * oh1, axis=0, keepdims=True).astype(jnp.int32)
    total = (prior[:, T - 1:T] + both[:, T - 1:T]).astype(jnp.int32)

    info = jnp.where(row8 == 0, eid0, jnp.where(row8 == 1, eid1, jnp.where(row8 == 2, rank0,
                     jnp.where(row8 == 3, rank1, 0))))
    return info, jnp.where(row8 == 0, w0, jnp.where(row8 == 1, w1, 0.0)), total


def _dispatch_kernel(zero_blk_ref, dest_ref, xpk_ref, xs_ref, zbuf, sem, zsem, *, tile):
    T = tile

    @pl.when(pl.program_id(0) == 0)
    def _():
        zbuf[...] = jnp.zeros_like(zbuf)

        def zero_copy(j):
            row = pl.multiple_of(jnp.maximum(zero_blk_ref[j], 0) * TOK_ROWS, MOE_ROWS * TOK_ROWS)
            return pltpu.make_async_copy(zbuf, xs_ref.at[pl.ds(row, MOE_ROWS * TOK_ROWS)], zsem)

        for j in range(2 * N_EXPERTS):
            @pl.when(zero_blk_ref[j] >= 0)
            def _():
                zero_copy(j).start()
        for j in range(2 * N_EXPERTS):
            @pl.when(zero_blk_ref[j] >= 0)
            def _():
                zero_copy(j).wait()

    for t in range(T):
        for kk in range(2):
            row = dest_ref[kk, t] * TOK_ROWS
            pltpu.make_async_copy(xpk_ref.at[pl.ds(t * TOK_ROWS, TOK_ROWS)], xs_ref.at[pl.ds(row, TOK_ROWS)],
                                  sem).start(priority=kk)
    for kk in range(2):
        pltpu.make_async_copy(xpk_ref, xs_ref.at[pl.ds(0, T * TOK_ROWS)], sem).wait()


def _dispatch(last_blk, dest, xpk, n_rows):
    N = xpk.shape[0] // TOK_ROWS
    T = min(DISPATCH_TILE, N)
    assert N % T == 0
    grid_spec = pltpu.PrefetchScalarGridSpec(
        num_scalar_prefetch=1,
        grid=(N // T,),
        in_specs=[pl.BlockSpec((2, T), lambda i, lb: (0, i), memory_space=pltpu.SMEM),
                  pl.BlockSpec((T * TOK_ROWS, LANES), lambda i, lb: (i, 0))],
        out_specs=pl.BlockSpec(memory_space=pl.ANY),
        scratch_shapes=[pltpu.VMEM((MOE_ROWS * TOK_ROWS, LANES), jnp.uint32),
                        pltpu.SemaphoreType.DMA(()), pltpu.SemaphoreType.DMA(())],
    )
    return pl.pallas_call(
        functools.partial(_dispatch_kernel, tile=T),
        grid_spec=grid_spec,
        out_shape=jax.ShapeDtypeStruct((n_rows * TOK_ROWS, LANES), jnp.uint32),
        compiler_params=pltpu.CompilerParams(dimension_semantics=("arbitrary",), has_side_effects=True),
        name="dispatch",
    )(last_blk, dest, xpk)


def _experts_kernel(blk_e_ref, nblk_ref, run_first_ref, run_idx_ref, next_e_ref,
                    xs_ref, w1_hbm, w3_hbm, w2_hbm, yt_ref,
                    w1b, w3b, w2b, w1f, w3f, w2f, stage, ids_v, ids_s, row_sem, ids_sem, w_sem, *, n_tokens):
    i = pl.program_id(0)
    R = MOE_ROWS
    SR = R * PK_TILES
    n_used = nblk_ref[0]
    spare = 2 * n_tokens
    slot = i % N_STAGE
    prev = (i + N_STAGE - 1) % N_STAGE
    prev2 = (i + N_STAGE - 2) % N_STAGE

    def ids_copy(s):
        return pltpu.make_async_copy(ids_v.at[pl.ds(s, 1)], ids_s.at[pl.ds(s, 1)], ids_sem)

    def row_copy(s, r, dst_row):
        src = pl.multiple_of((s * R + r) * PK_TILES, PK_TILES)
        dst = pl.multiple_of(dst_row * PK_TILES, PK_TILES)
        return pltpu.make_async_copy(stage.at[pl.ds(src, PK_TILES)], yt_ref.at[pl.ds(dst, PK_TILES)],
                                     row_sem.at[s]).start(priority=1)

    def drain_rows(s):
        pltpu.make_async_copy(stage.at[pl.ds(0, SR)], yt_ref.at[pl.ds(0, SR)], row_sem.at[s]).wait()

    def send_rows(s):
        for r in range(R):
            row_copy(s, r, ids_s[s, r])

    @pl.when(i == 0)
    def _():
        stage[...] = jnp.zeros_like(stage)
        ids_v[...] = (spare + lax.broadcasted_iota(jnp.int32, (SUBLANES, R), 0) * R
                      + lax.broadcasted_iota(jnp.int32, (SUBLANES, R), 1))
        ids_copy(N_STAGE - 1).start()
        for s in range(N_STAGE - 1):
            for r in range(R):
                row_copy(s, r, spare + s * R + r)

    def weight_copies(e, buf):
        return [pltpu.make_async_copy(src.at[e], dst.at[buf], w_sem.at[buf])
                for src, dst in ((w1_hbm, w1f), (w3_hbm, w3f), (w2_hbm, w2f))]

    @pl.when(i == 0)
    def _():
        for cp in weight_copies(blk_e_ref[0], 0):
            cp.start()

    @pl.when(run_first_ref[i] != 0)
    def _():
        buf = run_idx_ref[i] % 2
        for cp in weight_copies(blk_e_ref[i], buf):
            cp.wait()
        w1b[...] = w1f[buf].astype(BF16)
        w3b[...] = w3f[buf].astype(BF16)
        w2b[...] = w2f[buf].astype(BF16)

        @pl.when(next_e_ref[i] >= 0)
        def _():
            for cp in weight_copies(next_e_ref[i], 1 - buf):
                cp.start()

    @pl.when(i < n_used)
    def _():
        ids_copy(prev).wait()
        drain_rows(slot)
        send_rows(prev)

        meta = pltpu.bitcast(xs_ref[pl.ds(PK_TILES, R, stride=TOK_ROWS), :], jnp.int32).astype(F32).T
        first_expert = meta[META_EXPERT:META_EXPERT + 1, :]
        token = meta[META_TOKEN:META_TOKEN + 1, :]
        valid = meta[META_VALID:META_VALID + 1, :]
        second = jnp.where(first_expert != blk_e_ref[i].astype(F32), 1.0, 0.0)
        pad_row = (spare + slot * R + lax.broadcasted_iota(jnp.int32, (1, R), 1)).astype(F32)
        ids_v[pl.ds(slot, 1), :] = jnp.where(valid != 0.0, 2.0 * token + second, pad_row).astype(jnp.int32)
        ids_copy(slot).start()

        u = jnp.concatenate([xs_ref[pl.ds(c, R, stride=TOK_ROWS), :] for c in range(PK_TILES)], axis=1)
        xb = _unpack_bf16_pairs(u).astype(BF16)
        h1 = _dot(xb, w1b[...])
        h3 = _dot(xb, w3b[...])
        hact = (h1 * _sigmoid(h1)) * h3
        y = _dot(hact.astype(BF16), w2b[...])
        pk = _pack_bf16_pairs(y)
        for c in range(PK_TILES):
            stage[pl.ds(slot * SR + c, R, stride=PK_TILES), :] = pk[:, c * LANES:(c + 1) * LANES]

    @pl.when(i == n_used)
    def _():
        ids_copy(prev).wait()
        drain_rows(slot)
        send_rows(prev)
        drain_rows(prev2)
        drain_rows(prev)


def _experts(blk_e, nblk, run_first, run_idx, next_e, xs, w1, w3, w2, n_tokens):
    P = xs.shape[0] // TOK_ROWS
    nb = P // MOE_ROWS
    assert blk_e.shape[0] == nb + 1
    grid_spec = pltpu.PrefetchScalarGridSpec(
        num_scalar_prefetch=5,
        grid=(nb + 1,),
        in_specs=[pl.BlockSpec((MOE_ROWS * TOK_ROWS, LANES), lambda i, be, n, *_: (jnp.minimum(i, n[0] - 1), 0)),
                  pl.BlockSpec(memory_space=pl.ANY), pl.BlockSpec(memory_space=pl.ANY),
                  pl.BlockSpec(memory_space=pl.ANY)],
        out_specs=pl.BlockSpec(memory_space=pl.ANY),
        scratch_shapes=[pltpu.VMEM((D_MODEL, EXPERT_FF), BF16), pltpu.VMEM((D_MODEL, EXPERT_FF), BF16),
                        pltpu.VMEM((EXPERT_FF, D_MODEL), BF16),
                        pltpu.VMEM((2, D_MODEL, EXPERT_FF), F32), pltpu.VMEM((2, D_MODEL, EXPERT_FF), F32),
                        pltpu.VMEM((2, EXPERT_FF, D_MODEL), F32),
                        pltpu.VMEM((N_STAGE * MOE_ROWS * PK_TILES, LANES), jnp.uint32),
                        pltpu.VMEM((SUBLANES, MOE_ROWS), jnp.int32),
                        pltpu.SMEM((N_STAGE, MOE_ROWS), jnp.int32),
                        pltpu.SemaphoreType.DMA((N_STAGE,)), pltpu.SemaphoreType.DMA(()),
                        pltpu.SemaphoreType.DMA((2,))],
    )
    return pl.pallas_call(
        functools.partial(_experts_kernel, n_tokens=n_tokens),
        grid_spec=grid_spec,
        out_shape=jax.ShapeDtypeStruct(((2 * n_tokens + N_STAGE * MOE_ROWS) * PK_TILES, LANES), jnp.uint32),
        compiler_params=pltpu.CompilerParams(dimension_semantics=("arbitrary",), vmem_limit_bytes=VMEM_LIMIT,
                                             has_side_effects=True),
        name="experts",
    )(blk_e, nblk, run_first, run_idx, next_e, xs, w1, w3, w2)


def _combine_kernel(x1_ref, wts_ref, yt_ref, g_ref, b_ref, out_ref, *, tile):
    T = tile
    wpad = jnp.concatenate([wts_ref[...], jnp.zeros((LANES - SUBLANES, T), F32)], axis=0)
    wt = wpad.T
    slot = lambda s: _unpack_bf16_pairs(jnp.concatenate(
        [yt_ref[pl.ds(s * PK_TILES + c, T, stride=2 * PK_TILES), :] for c in range(PK_TILES)], axis=1))
    y = wt[:, 0:1] * slot(0) + wt[:, 1:2] * slot(1)
    out_ref[...] = _layer_norm(DN_ALPHA * x1_ref[...] + y, g_ref[...], b_ref[...])


def _combine(x1f, wts, yt, g, b):
    N = x1f.shape[0]
    T = min(COMBINE_TILE, N)
    assert N % T == 0
    return pl.pallas_call(
        functools.partial(_combine_kernel, tile=T),
        grid=(N // T,),
        in_specs=[pl.BlockSpec((T, D_MODEL), lambda i: (i, 0)),
                  pl.BlockSpec((SUBLANES, T), lambda i: (0, i)),
                  pl.BlockSpec((T * 2 * PK_TILES, LANES), lambda i: (i, 0)),
                  _const_spec(g.shape), _const_spec(b.shape)],
        out_specs=pl.BlockSpec((T, D_MODEL), lambda i: (i, 0)),
        out_shape=jax.ShapeDtypeStruct((N, D_MODEL), F32),
        compiler_params=pltpu.CompilerParams(dimension_semantics=("arbitrary",)),
        name="combine",
    )(x1f, wts, yt, g, b)


def _odd_tiles(w):
    return jnp.pad(w, ((0, 0), (0, LANES))).astype(BF16)


def _pack_mixer_weights(w_in, b_in, rg_w_a, rg_w_x, rg_b_a, rg_b_x, gla_w_a2):
    wmain = _odd_tiles(w_in[:, :C_MAIN_END])
    wbg = _odd_tiles(w_in[:, C_GATES_START:])
    walr = jnp.pad(w_in[:, C_MAIN_END:C_GATES_START], ((0, 0), (0, LANES - GLA_RANK))).astype(BF16)
    bmain = b_in[None, :C_MAIN_END]
    bbg = b_in[None, C_GATES_START:]
    balr = jnp.pad(b_in[None, C_MAIN_END:C_GATES_START], ((0, 0), (0, LANES - GLA_RANK)))
    zero = jnp.zeros((RNN_BLOCK_W, RNN_BLOCK_W), w_in.dtype)
    tiles = []
    for p in range(RNN_BLOCKS // 2):
        top = jnp.concatenate([rg_w_a[2 * p], zero, rg_w_x[2 * p], zero], axis=1)
        bot = jnp.concatenate([zero, rg_w_a[2 * p + 1], zero, rg_w_x[2 * p + 1]], axis=1)
        tiles.append(jnp.concatenate([top, bot], axis=0))
    wgate = jnp.stack(tiles).astype(BF16)
    bgate = jnp.concatenate([rg_b_a, rg_b_x])[None, :]
    wa2 = jnp.concatenate([gla_w_a2, jnp.zeros((LANES - GLA_RANK, GLA_DK), gla_w_a2.dtype)], axis=0).astype(BF16)
    return (wmain, bmain, wbg, bbg, walr, balr), wgate, bgate, wa2


def _layer(x, w_in, b_in, conv_w, conv_b, rg_w_a, rg_b_a, rg_w_x, rg_b_x, rg_lambda, gla_w_a2, gla_b_a,
           gla_norm_g, w_proj_rnn, w_proj_gla, w_o, b_o, ln1_g, ln1_b, router_w_group, router_b_group,
           router_w_expert, router_b_expert, exp_w1, exp_w3, exp_w2, ln2_g, ln2_b):
    B, S, _ = x.shape
    N = B * S
    row = lambda p: p[None, :]

    w_slices, wgate, bgate, wa2 = _pack_mixer_weights(w_in, b_in, rg_w_a, rg_w_x, rg_b_a, rg_b_x, gla_w_a2)
    wr = jnp.concatenate([router_w_group.T, jnp.zeros((SUBLANES - N_GROUPS, D_MODEL), F32), router_w_expert.T],
                         axis=0).astype(BF16)
    br = jnp.concatenate([router_b_group, jnp.zeros((SUBLANES - N_GROUPS,), F32), router_b_expert])[:, None]
    x1f, xpk, info, wts, tcnt = _mixer(
        x, *w_slices, conv_w, row(conv_b), wgate, bgate, row(rg_lambda), wa2, row(gla_b_a), row(gla_norm_g),
        _odd_tiles(w_proj_rnn), _odd_tiles(w_proj_gla), _odd_tiles(w_o), row(b_o), row(ln1_g), row(ln1_b), wr, br)

    tcnt = tcnt[:, :, 0]
    nt = tcnt.shape[0]
    tot = jnp.sum(tcnt, axis=0)
    pcount = (tot + MOE_ROWS - 1) // MOE_ROWS * MOE_ROWS
    pend = jnp.cumsum(pcount)
    base = (pend - pcount)[None, :] + jnp.cumsum(tcnt, axis=0) - tcnt
    base_tok = jnp.repeat(base.T, N // nt, axis=1)
    experts_col = jnp.arange(N_EXPERTS, dtype=jnp.int32)[:, None, None]
    dest = jnp.sum(jnp.where(info[None, 0:2] == experts_col, base_tok[:, None, :], 0), axis=0) + info[2:4]
    nb = -(-(2 * N) // MOE_ROWS) + N_EXPERTS
    P = nb * MOE_ROWS
    nblk = (pend[-1] // MOE_ROWS).astype(jnp.int32)
    blk_start = jnp.minimum(jnp.arange(nb + 1, dtype=jnp.int32), nblk - 1) * MOE_ROWS
    blk_e = jnp.sum((blk_start[:, None] >= pend[None, :]).astype(jnp.int32), axis=1)
    blk_e = jnp.minimum(blk_e, N_EXPERTS - 1)

    last_blk = jnp.where(tot > 0, pend - MOE_ROWS, -1)
    tail_blk = nblk + jnp.arange(N_EXPERTS, dtype=jnp.int32)
    tail_blk = jnp.where(tail_blk < nb, tail_blk * MOE_ROWS, -1)
    xs = _dispatch(jnp.concatenate([last_blk, tail_blk]).astype(jnp.int32), dest, xpk, P)
    blk_i = jnp.arange(nb + 1, dtype=jnp.int32)
    experts_row = jnp.arange(N_EXPERTS, dtype=jnp.int32)
    run_first = ((blk_i < nblk) & ((blk_i == 0) | (blk_e != jnp.roll(blk_e, 1)))).astype(jnp.int32)
    run_idx = jnp.sum(jnp.where(blk_i[None, :] <= blk_i[:, None], run_first[None, :], 0), axis=1) - 1
    later = (experts_row[None, :] > experts_row[:, None]) & (tot[None, :] > 0)
    next_nonempty = jnp.min(jnp.where(later, experts_row[None, :], N_EXPERTS), axis=1)
    next_nonempty = jnp.where(next_nonempty < N_EXPERTS, next_nonempty, -1)
    next_e = jnp.sum(jnp.where(blk_e[:, None] == experts_row[None, :], next_nonempty[None, :], 0), axis=1)
    yt = _experts(blk_e, nblk[None], run_first, run_idx.astype(jnp.int32), next_e.astype(jnp.int32),
                  xs, exp_w1, exp_w3, exp_w2, N)
    out = _combine(x1f, wts, yt, row(ln2_g), row(ln2_b))
    return out.reshape(B, S, D_MODEL)


def kernel(x, w_in, b_in, conv_w, conv_b, rg_w_a, rg_b_a, rg_w_x, rg_b_x, rg_lambda, gla_w_a2, gla_b_a, gla_norm_g, w_proj_rnn, w_proj_gla, w_o, b_o, ln1_g, ln1_b, router_w_group, router_b_group, router_w_expert, router_b_expert, exp_w1, exp_w3, exp_w2, ln2_g, ln2_b):
    h = x
    for l in range(w_in.shape[0]):
        h = _layer(h, w_in[l], b_in[l], conv_w[l], conv_b[l], rg_w_a[l], rg_b_a[l], rg_w_x[l], rg_b_x[l],
                   rg_lambda[l], gla_w_a2[l], gla_b_a[l], gla_norm_g[l], w_proj_rnn[l], w_proj_gla[l], w_o[l],
                   b_o[l], ln1_g[l], ln1_b[l], router_w_group[l], router_b_group[l], router_w_expert[l],
                   router_b_expert[l], exp_w1[l], exp_w3[l], exp_w2[l], ln2_g[l], ln2_b[l])
    return h
```

```python
import functools

import jax
import jax.numpy as jnp
from jax import lax
from jax.experimental import pallas as pl
from jax.experimental.pallas import tpu as pltpu

F32 = jnp.float32
BF16 = jnp.bfloat16

D_MODEL = 1024
RNN_WIDTH = 1024
RNN_BLOCKS = 8
RNN_BLOCK_W = RNN_WIDTH // RNN_BLOCKS
CONV_WIDTH = 4
LRU_C = 8.0
GLA_HEADS = 4
GLA_DK = D_MODEL // 2
GLA_DV = D_MODEL
GLA_HEAD_K = GLA_DK // GLA_HEADS
GLA_HEAD_V = GLA_DV // GLA_HEADS
GLA_RANK = 16
GLA_TAU = 16.0
GLA_CHUNK = 64
N_GROUPS = 4
EXPERTS_PER_GROUP = 8
N_EXPERTS = N_GROUPS * EXPERTS_PER_GROUP
EXPERT_FF = 512
DN_ALPHA = 2.0 ** 0.25
LN_EPS = 1e-5
RMS_EPS = 1e-6

LANES = 128
SUBLANES = 8
VMEM_LIMIT = 56 * 1024 * 1024

C_RX, C_RY, C_Q, C_K, C_V, C_G = 0, 1024, 2048, 2560, 3072, 4096
C_GA, C_GB = 0, 1024
C_MAIN_END = 5120
C_GATES_START = C_MAIN_END + GLA_RANK

MIX_TILE = 256
MOE_ROWS = 512
DISPATCH_TILE = 2048
COMBINE_TILE = 1024
N_STAGE = 3
GATE_PAIR_W = 2 * RNN_BLOCK_W
PK_TILES = D_MODEL // 2 // LANES
TOK_ROWS = PK_TILES + 1
META_EXPERT, META_TOKEN, META_VALID = 0, 1, 2


def _sigmoid(v):
    return 0.5 * jnp.tanh(0.5 * v) + 0.5


def _softplus(v):
    return jnp.maximum(v, 0.0) + jnp.log1p(jnp.exp(-jnp.abs(v)))


def _layer_norm(v, g, b):
    mu = jnp.mean(v, axis=-1, keepdims=True)
    c = v - mu
    var = jnp.mean(c * c, axis=-1, keepdims=True)
    return c * lax.rsqrt(var + LN_EPS) * g + b


def _dot(a, b):
    return jnp.dot(a, b, preferred_element_type=F32)


def _dot_nt(a, b):
    return lax.dot_general(a, b, (((1,), (1,)), ((), ())), preferred_element_type=F32)


def _dot_tn(a, b):
    return lax.dot_general(a, b, (((0,), (0,)), ((), ())), preferred_element_type=F32)


def _pack_bf16_pairs(v):
    half = v.shape[1] // 2
    bits = pltpu.bitcast(v.astype(BF16).astype(F32), jnp.uint32)
    return (bits[:, :half] >> 16) | bits[:, half:]


def _unpack_bf16_pairs(u):
    lo = pltpu.bitcast(u << 16, F32)
    hi = pltpu.bitcast(u & jnp.uint32(0xFFFF0000), F32)
    return jnp.concatenate([lo, hi], axis=1)


def _const_spec(shape):
    nd = len(shape)
    return pl.BlockSpec(shape, lambda *_: (0,) * nd, pipeline_mode=pl.Buffered(1))


def _mixer_kernel(x_ref, wmain_ref, bmain_ref, wbg_ref, bbg_ref, walr_ref, balr_ref,
                  convw_ref, convb_ref, wgate_ref, bgate_ref, lam_ref,
                  wa2_ref, ba2_ref, gnorm_ref, wprnn_ref, wpgla_ref, wo_ref, bo_ref, ln1g_ref, ln1b_ref,
                  wr_ref, br_ref,
                  x1_ref, xpk_ref, info_ref, wts_ref, cnt_ref, rxbuf, hcar, st_ref, hbuf, zbuf,
                  *, tile, steps_per_seq):
    T = tile
    step = pl.program_id(0)

    @pl.when(step == 0)
    def _():
        zbuf[...] = jnp.zeros_like(zbuf)

    @pl.when(step % steps_per_seq == 0)
    def _():
        rxbuf[0:SUBLANES, :] = jnp.zeros((SUBLANES, RNN_WIDTH), F32)
        hcar[...] = jnp.zeros_like(hcar)
        st_ref[...] = jnp.zeros_like(st_ref)

    x = x_ref[...]
    xb = x.astype(BF16)

    def proj(c0, c1, w_ref=wmain_ref, b_ref=bmain_ref):
        return _dot(xb, w_ref[:, c0:c1]) + b_ref[:, c0:c1]

    rx = proj(C_RX, C_RX + RNN_WIDTH)
    rxbuf[SUBLANES:SUBLANES + T, :] = rx
    u = convb_ref[...] + convw_ref[CONV_WIDTH - 1:CONV_WIDTH, :] * rx
    for j in range(1, CONV_WIDTH):
        u = u + convw_ref[CONV_WIDTH - 1 - j:CONV_WIDTH - j, :] * rxbuf[SUBLANES - j:SUBLANES - j + T, :]
    rxbuf[0:SUBLANES, :] = rxbuf[T:T + SUBLANES, :]

    qk = proj(C_Q, C_Q + 2 * GLA_DK)
    q = qk[:, :GLA_DK] * (GLA_HEAD_K ** -0.5)
    k = qk[:, GLA_DK:]
    alr = proj(0, LANES, walr_ref, balr_ref)

    x1 = _layer_norm(zbuf[...], ln1g_ref[...], ln1b_ref[...])
    x1_ref[...] = x1
    info, wts, total = _route_tile(x1.astype(BF16), wr_ref[...], br_ref[...])
    info_ref[...] = info
    wts_ref[...] = wts
    cnt_ref[0] = jnp.broadcast_to(total, (N_EXPERTS, LANES))
    pk = _pack_bf16_pairs(x1)
    for c in range(PK_TILES):
        xpk_ref[pl.ds(c, T, stride=TOK_ROWS), :] = pk[:, c * LANES:(c + 1) * LANES]
    first_expert = jnp.broadcast_to(info[0:1, :].astype(F32), (LANES, T)).T.astype(jnp.int32)
    token = lax.broadcasted_iota(jnp.int32, (T, LANES), 0) + jnp.maximum(step - 1, 0) * T
    lane = lax.broadcasted_iota(jnp.int32, (T, LANES), 1)
    meta = jnp.where(lane == META_EXPERT, first_expert,
                     jnp.where(lane == META_TOKEN, token, jnp.where(lane == META_VALID, 1, 0)))
    xpk_ref[pl.ds(PK_TILES, T, stride=TOK_ROWS), :] = pltpu.bitcast(meta, jnp.uint32)

    r_parts, i_parts = [], []
    for p in range(RNN_BLOCKS // 2):
        up = u[:, GATE_PAIR_W * p:GATE_PAIR_W * (p + 1)].astype(BF16)
        gp = _dot(up, wgate_ref[p])
        r_parts.append(gp[:, :GATE_PAIR_W])
        i_parts.append(gp[:, GATE_PAIR_W:])
    r = _sigmoid(jnp.concatenate(r_parts, axis=1) + bgate_ref[:, :RNN_WIDTH])
    ig = _sigmoid(jnp.concatenate(i_parts, axis=1) + bgate_ref[:, RNN_WIDTH:])
    v = proj(C_V, C_V + GLA_DV)
    ry = proj(C_RY, C_RY + RNN_WIDTH)

    z = _dot(alr.astype(BF16), wa2_ref[...]) + ba2_ref[...]
    la = -_softplus(-z) * (1.0 / GLA_TAU)
    ri = lax.broadcasted_iota(jnp.int32, (T, T), 0)
    ci = lax.broadcasted_iota(jnp.int32, (T, T), 1)
    chunk_start = ri - (ri & (GLA_CHUNK - 1))
    tri = jnp.where((ci <= ri) & (ci >= chunk_start), 1.0, 0.0).astype(BF16)
    la_hi = la.astype(BF16)
    la_lo = (la - la_hi.astype(F32)).astype(BF16)
    bcum = _dot(tri, la_hi) + _dot(tri, la_lo)

    g = proj(C_G, C_G + GLA_DV)

    log_a = (-LRU_C) * r * _softplus(-lam_ref[...])
    a = jnp.exp(log_a)
    m2 = -jnp.tanh(log_a) * (1.0 + a * a)
    bv = jnp.where(m2 > 0.0, m2 * lax.rsqrt(m2), 0.0) * (ig * u)

    cr = lax.broadcasted_iota(jnp.int32, (GLA_CHUNK, GLA_CHUNK), 0)
    cc = lax.broadcasted_iota(jnp.int32, (GLA_CHUNK, GLA_CHUNK), 1)
    causal = cr >= cc
    n_chunks = T // GLA_CHUNK
    heads = [(slice(hd * GLA_HEAD_K, (hd + 1) * GLA_HEAD_K), slice(hd * GLA_HEAD_V, (hd + 1) * GLA_HEAD_V))
             for hd in range(GLA_HEADS)]
    qd_c, ki_c, ke_c, vv_c, dec_c = [], [], [], [], []
    for c in range(n_chunks):
        r0 = c * GLA_CHUNK
        bc = bcum[r0:r0 + GLA_CHUNK, :]
        bl = bcum[r0 + GLA_CHUNK - 1:r0 + GLA_CHUNK, :]
        kc = k[r0:r0 + GLA_CHUNK, :]
        qd_c.append((q[r0:r0 + GLA_CHUNK, :] * jnp.exp(bc)).astype(BF16))
        ki_c.append((kc * jnp.exp(-bc)).astype(BF16))
        ke_c.append((kc * jnp.exp(bl - bc)).astype(BF16))
        vv_c.append(v[r0:r0 + GLA_CHUNK, :].astype(BF16))
        dec_c.append(jnp.exp(bl))
    scores = [[jnp.where(causal, _dot_nt(qd_c[c][:, ks], ki_c[c][:, ks]), 0.0).astype(BF16) for ks, _ in heads]
              for c in range(n_chunks)]
    intra = [[_dot(scores[c][hd], vv_c[c][:, vs]) for hd, (_, vs) in enumerate(heads)] for c in range(n_chunks)]
    incr = [[_dot_tn(vv_c[c][:, vs], ke_c[c][:, ks]) for ks, vs in heads] for c in range(n_chunks)]

    n_groups = T // SUBLANES
    sub = lax.broadcasted_iota(jnp.int32, (n_groups, SUBLANES, RNN_WIDTH), 1)
    sa = a.reshape(n_groups, SUBLANES, RNN_WIDTH)
    sb = bv.reshape(n_groups, SUBLANES, RNN_WIDTH)
    for s in (1, 2, 4):
        keep = sub >= s
        sb = sa * jnp.where(keep, pltpu.roll(sb, s, 1), 0.0) + sb
        sa = sa * jnp.where(keep, pltpu.roll(sa, s, 1), 1.0)
    carry = hcar[0:1, :]
    for gi in range(n_groups):
        hg = sb[gi] + sa[gi] * carry
        hbuf[gi * SUBLANES:(gi + 1) * SUBLANES, :] = hg
        carry = hg[SUBLANES - 1:SUBLANES, :]
    hcar[0:1, :] = carry
    h = hbuf[...]
    ga = proj(C_GA, C_GA + D_MODEL, wbg_ref, bbg_ref)

    out_a = _dot((h * jax.nn.gelu(ry)).astype(BF16), wprnn_ref[:, :D_MODEL])
    gb = proj(C_GB, C_GB + D_MODEL, wbg_ref, bbg_ref)

    states = [st_ref[hd] for hd in range(GLA_HEADS)]
    o_chunks = []
    for c in range(n_chunks):
        o_heads = []
        for hd, (ks, _) in enumerate(heads):
            o_heads.append(intra[c][hd] + _dot_nt(qd_c[c][:, ks], states[hd].astype(BF16)))
            states[hd] = states[hd] * dec_c[c][:, ks] + incr[c][hd]
        o_chunks.append(jnp.concatenate(o_heads, axis=1))
    for hd in range(GLA_HEADS):
        st_ref[hd] = states[hd]
    o_all = jnp.concatenate(o_chunks, axis=0)

    o_parts = []
    for hd in range(GLA_HEADS):
        vs = slice(hd * GLA_HEAD_V, (hd + 1) * GLA_HEAD_V)
        oh = o_all[:, vs]
        ms = jnp.mean(oh * oh, axis=-1, keepdims=True)
        o_parts.append(oh * lax.rsqrt(ms + RMS_EPS) * gnorm_ref[:, vs])
    on = jnp.concatenate(o_parts, axis=1) * (g * _sigmoid(g))
    out_b = _dot(on.astype(BF16), wpgla_ref[:, :D_MODEL])

    merged = _sigmoid(ga) * out_a + _sigmoid(gb) * out_b
    y = _dot(merged.astype(BF16), wo_ref[:, :D_MODEL]) + bo_ref[...]
    zbuf[...] = DN_ALPHA * x + y


def _mixer(x, wmain, bmain, wbg, bbg, walr, balr, conv_w, conv_b, wgate, bgate, lam, wa2, ba2, gnorm,
           wprnn, wpgla, wo, bo, ln1g, ln1b, wr, br):
    B, S, _ = x.shape
    T = min(MIX_TILE, S)
    assert S % T == 0 and T % GLA_CHUNK == 0
    N = B * S
    nt = N // T
    mix_tile = lambda i: jnp.minimum(i, nt - 1)
    tail_tile = lambda i: jnp.maximum(i - 1, 0)
    weights = (wmain, bmain, wbg, bbg, walr, balr, conv_w, conv_b, wgate, bgate, lam, wa2, ba2, gnorm,
               wprnn, wpgla, wo, bo, ln1g, ln1b, wr, br)
    return pl.pallas_call(
        functools.partial(_mixer_kernel, tile=T, steps_per_seq=S // T),
        grid=(nt + 1,),
        in_specs=[pl.BlockSpec((T, D_MODEL), lambda i: (mix_tile(i), 0))] + [_const_spec(w.shape) for w in weights],
        out_specs=[pl.BlockSpec((T, D_MODEL), lambda i: (tail_tile(i), 0)),
                   pl.BlockSpec((T * TOK_ROWS, LANES), lambda i: (tail_tile(i), 0)),
                   pl.BlockSpec((SUBLANES, T), lambda i: (0, tail_tile(i))),
                   pl.BlockSpec((SUBLANES, T), lambda i: (0, tail_tile(i))),
                   pl.BlockSpec((1, N_EXPERTS, LANES), lambda i: (tail_tile(i), 0, 0))],
        out_shape=[jax.ShapeDtypeStruct((N, D_MODEL), F32),
                   jax.ShapeDtypeStruct((N * TOK_ROWS, LANES), jnp.uint32),
                   jax.ShapeDtypeStruct((SUBLANES, N), jnp.int32),
                   jax.ShapeDtypeStruct((SUBLANES, N), F32),
                   jax.ShapeDtypeStruct((nt, N_EXPERTS, LANES), jnp.int32)],
        scratch_shapes=[pltpu.VMEM((T + SUBLANES, RNN_WIDTH), F32),
                        pltpu.VMEM((SUBLANES, RNN_WIDTH), F32),
                        pltpu.VMEM((GLA_HEADS, GLA_HEAD_V, GLA_HEAD_K), F32),
                        pltpu.VMEM((T, RNN_WIDTH), F32),
                        pltpu.VMEM((T, D_MODEL), F32)],
        compiler_params=pltpu.CompilerParams(dimension_semantics=("arbitrary",), vmem_limit_bytes=VMEM_LIMIT),
        name="mixer",
    )(x.reshape(N, D_MODEL), *weights)


def _route_tile(x1b, wr, br):
    T = x1b.shape[0]
    logits = _dot_nt(wr, x1b) + br
    row8 = lax.broadcasted_iota(jnp.int32, (SUBLANES, T), 0)
    row8f = row8.astype(F32)
    neg = jnp.float32(-jnp.inf)
    first = lambda hit: jnp.min(jnp.where(hit, row8f, float(SUBLANES)), axis=0, keepdims=True)

    gl = jnp.where(row8 < N_GROUPS, logits[0:SUBLANES, :], neg)
    gmax = jnp.max(gl, axis=0, keepdims=True)
    grp = first(gl == gmax)
    p_grp = 1.0 / jnp.sum(jnp.exp(gl - gmax), axis=0, keepdims=True)

    e_sel = jnp.zeros((EXPERTS_PER_GROUP, T), F32)
    for gi in range(N_GROUPS):
        lo = SUBLANES + gi * EXPERTS_PER_GROUP
        e_sel = jnp.where(grp == float(gi), logits[lo:lo + EXPERTS_PER_GROUP, :], e_sel)
    m1 = jnp.max(e_sel, axis=0, keepdims=True)
    i1 = first(e_sel == m1)
    e_rest = jnp.where(row8f == i1, neg, e_sel)
    m2 = jnp.max(e_rest, axis=0, keepdims=True)
    i2 = first(e_rest == m2)
    e21 = jnp.exp(m2 - m1)
    p1 = 1.0 / (1.0 + e21)
    w0 = p_grp * p1
    w1 = p_grp * (e21 * p1)
    eid0 = (grp * EXPERTS_PER_GROUP + i1).astype(jnp.int32)
    eid1 = (grp * EXPERTS_PER_GROUP + i2).astype(jnp.int32)

    erow = lax.broadcasted_iota(jnp.int32, (N_EXPERTS, T), 0)
    oh0 = jnp.where(erow == eid0, 1.0, 0.0)
    oh1 = jnp.where(erow == eid1, 1.0, 0.0)
    both = oh0 + oh1
    ti = lax.broadcasted_iota(jnp.int32, (T, T), 0)
    tj = lax.broadcasted_iota(jnp.int32, (T, T), 1)
    before = jnp.where(ti < tj, 1.0, 0.0).astype(BF16)
    prior = _dot(both.astype(BF16), before)
    rank0 = jnp.sum(prior * oh0, axis=0, keepdims=True).astype(jnp.int32)
    rank1 = jnp.sum(prior * oh1, axis=0, keepdims=True).astype(jnp.int32)
    total = (prior[:, T - 1:T] + both[:, T - 1:T]).astype(jnp.int32)

    info = jnp.where(row8 == 0, eid0, jnp.where(row8 == 1, eid1, jnp.where(row8 == 2, rank0,
                     jnp.where(row8 == 3, rank1, 0))))
    return info, jnp.where(row8 == 0, w0, jnp.where(row8 == 1, w1, 0.0)), total


def _dispatch_kernel(zero_blk_ref, dest_ref, xpk_ref, xs_ref, zbuf, sem, zsem, *, tile):
    T = tile

    @pl.when(pl.program_id(0) == 0)
    def _():
        zbuf[...] = jnp.zeros_like(zbuf)

        def zero_copy(j):
            row = pl.multiple_of(jnp.maximum(zero_blk_ref[j], 0) * TOK_ROWS, MOE_ROWS * TOK_ROWS)
            return pltpu.make_async_copy(zbuf, xs_ref.at[pl.ds(row, MOE_ROWS * TOK_ROWS)], zsem)

        for j in range(2 * N_EXPERTS):
            @pl.when(zero_blk_ref[j] >= 0)
            def _():
                zero_copy(j).start()
        for j in range(2 * N_EXPERTS):
            @pl.when(zero_blk_ref[j] >= 0)
            def _():
                zero_copy(j).wait()

    for t in range(T):
        for kk in range(2):
            row = dest_ref[kk, t] * TOK_ROWS
            pltpu.make_async_copy(xpk_ref.at[pl.ds(t * TOK_ROWS, TOK_ROWS)], xs_ref.at[pl.ds(row, TOK_ROWS)],
                                  sem).start(priority=kk)
    for kk in range(2):
        pltpu.make_async_copy(xpk_ref, xs_ref.at[pl.ds(0, T * TOK_ROWS)], sem).wait()


def _dispatch(last_blk, dest, xpk, n_rows):
    N = xpk.shape[0] // TOK_ROWS
    T = min(DISPATCH_TILE, N)
    assert N % T == 0
    grid_spec = pltpu.PrefetchScalarGridSpec(
        num_scalar_prefetch=1,
        grid=(N // T,),
        in_specs=[pl.BlockSpec((2, T), lambda i, lb: (0, i), memory_space=pltpu.SMEM),
                  pl.BlockSpec((T * TOK_ROWS, LANES), lambda i, lb: (i, 0))],
        out_specs=pl.BlockSpec(memory_space=pl.ANY),
        scratch_shapes=[pltpu.VMEM((MOE_ROWS * TOK_ROWS, LANES), jnp.uint32),
                        pltpu.SemaphoreType.DMA(()), pltpu.SemaphoreType.DMA(())],
    )
    return pl.pallas_call(
        functools.partial(_dispatch_kernel, tile=T),
        grid_spec=grid_spec,
        out_shape=jax.ShapeDtypeStruct((n_rows * TOK_ROWS, LANES), jnp.uint32),
        compiler_params=pltpu.CompilerParams(dimension_semantics=("arbitrary",), has_side_effects=True),
        name="dispatch",
    )(last_blk, dest, xpk)


def _experts_kernel(blk_e_ref, nblk_ref, run_first_ref, run_idx_ref, next_e_ref,
                    xs_ref, w1_hbm, w3_hbm, w2_hbm, yt_ref,
                    w1b, w3b, w2b, w1f, w3f, w2f, stage, ids_v, ids_s, row_sem, ids_sem, w_sem, *, n_tokens):
    i = pl.program_id(0)
    R = MOE_ROWS
    SR = R * PK_TILES
    n_used = nblk_ref[0]
    spare = 2 * n_tokens
    slot = i % N_STAGE
    prev = (i + N_STAGE - 1) % N_STAGE
    prev2 = (i + N_STAGE - 2) % N_STAGE

    def ids_copy(s):
        return pltpu.make_async_copy(ids_v.at[pl.ds(s, 1)], ids_s.at[pl.ds(s, 1)], ids_sem)

    def row_copy(s, r, dst_row):
        src = pl.multiple_of((s * R + r) * PK_TILES, PK_TILES)
        dst = pl.multiple_of(dst_row * PK_TILES, PK_TILES)
        return pltpu.make_async_copy(stage.at[pl.ds(src, PK_TILES)], yt_ref.at[pl.ds(dst, PK_TILES)],
                                     row_sem.at[s]).start(priority=1)

    def drain_rows(s):
        pltpu.make_async_copy(stage.at[pl.ds(0, SR)], yt_ref.at[pl.ds(0, SR)], row_sem.at[s]).wait()

    def send_rows(s):
        for r in range(R):
            row_copy(s, r, ids_s[s, r])

    @pl.when(i == 0)
    def _():
        stage[...] = jnp.zeros_like(stage)
        ids_v[...] = (spare + lax.broadcasted_iota(jnp.int32, (SUBLANES, R), 0) * R
                      + lax.broadcasted_iota(jnp.int32, (SUBLANES, R), 1))
        ids_copy(N_STAGE - 1).start()
        for s in range(N_STAGE - 1):
            for r in range(R):
                row_copy(s, r, spare + s * R + r)

    def weight_copies(e, buf):
        return [pltpu.make_async_copy(src.at[e], dst.at[buf], w_sem.at[buf])
                for src, dst in ((w1_hbm, w1f), (w3_hbm, w3f), (w2_hbm, w2f))]

    @pl.when(i == 0)
    def _():
        for cp in weight_copies(blk_e_ref[0], 0):
            cp.start()

    @pl.when(run_first_ref[i] != 0)
    def _():
        buf = run_idx_ref[i] % 2
        for cp in weight_copies(blk_e_ref[i], buf):
            cp.wait()
        w1b[...] = w1f[buf].astype(BF16)
        w3b[...] = w3f[buf].astype(BF16)
        w2b[...] = w2f[buf].astype(BF16)

        @pl.when(next_e_ref[i] >= 0)
        def _():
            for cp in weight_copies(next_e_ref[i], 1 - buf):
                cp.start()

    @pl.when(i < n_used)
    def _():
        ids_copy(prev).wait()
        drain_rows(slot)
        send_rows(prev)

        meta = pltpu.bitcast(xs_ref[pl.ds(PK_TILES, R, stride=TOK_ROWS), :], jnp.int32).astype(F32).T
        first_expert = meta[META_EXPERT:META_EXPERT + 1, :]
        token = meta[META_TOKEN:META_TOKEN + 1, :]
        valid = meta[META_VALID:META_VALID + 1, :]
        second = jnp.where(first_expert != blk_e_ref[i].astype(F32), 1.0, 0.0)
        pad_row = (spare + slot * R + lax.broadcasted_iota(jnp.int32, (1, R), 1)).astype(F32)
        ids_v[pl.ds(slot, 1), :] = jnp.where(valid != 0.0, 2.0 * token + second, pad_row).astype(jnp.int32)
        ids_copy(slot).start()

        u = jnp.concatenate([xs_ref[pl.ds(c, R, stride=TOK_ROWS), :] for c in range(PK_TILES)], axis=1)
        xb = _unpack_bf16_pairs(u).astype(BF16)
        h1 = _dot(xb, w1b[...])
        h3 = _dot(xb, w3b[...])
        hact = (h1 * _sigmoid(h1)) * h3
        y = _dot(hact.astype(BF16), w2b[...])
        pk = _pack_bf16_pairs(y)
        for c in range(PK_TILES):
            stage[pl.ds(slot * SR + c, R, stride=PK_TILES), :] = pk[:, c * LANES:(c + 1) * LANES]

    @pl.when(i == n_used)
    def _():
        ids_copy(prev).wait()
        drain_rows(slot)
        send_rows(prev)
        drain_rows(prev2)
        drain_rows(prev)


def _experts(blk_e, nblk, run_first, run_idx, next_e, xs, w1, w3, w2, n_tokens):
    P = xs.shape[0] // TOK_ROWS
    nb = P // MOE_ROWS
    assert blk_e.shape[0] == nb + 1
    grid_spec = pltpu.PrefetchScalarGridSpec(
        num_scalar_prefetch=5,
        grid=(nb + 1,),
        in_specs=[pl.BlockSpec((MOE_ROWS * TOK_ROWS, LANES), lambda i, be, n, *_: (jnp.minimum(i, n[0] - 1), 0)),
                  pl.BlockSpec(memory_space=pl.ANY), pl.BlockSpec(memory_space=pl.ANY),
                  pl.BlockSpec(memory_space=pl.ANY)],
        out_specs=pl.BlockSpec(memory_space=pl.ANY),
        scratch_shapes=[pltpu.VMEM((D_MODEL, EXPERT_FF), BF16), pltpu.VMEM((D_MODEL, EXPERT_FF), BF16),
                        pltpu.VMEM((EXPERT_FF, D_MODEL), BF16),
                        pltpu.VMEM((2, D_MODEL, EXPERT_FF), F32), pltpu.VMEM((2, D_MODEL, EXPERT_FF), F32),
                        pltpu.VMEM((2, EXPERT_FF, D_MODEL), F32),
                        pltpu.VMEM((N_STAGE * MOE_ROWS * PK_TILES, LANES), jnp.uint32),
                        pltpu.VMEM((SUBLANES, MOE_ROWS), jnp.int32),
                        pltpu.SMEM((N_STAGE, MOE_ROWS), jnp.int32),
                        pltpu.SemaphoreType.DMA((N_STAGE,)), pltpu.SemaphoreType.DMA(()),
                        pltpu.SemaphoreType.DMA((2,))],
    )
    return pl.pallas_call(
        functools.partial(_experts_kernel, n_tokens=n_tokens),
        grid_spec=grid_spec,
        out_shape=jax.ShapeDtypeStruct(((2 * n_tokens + N_STAGE * MOE_ROWS) * PK_TILES, LANES), jnp.uint32),
        compiler_params=pltpu.CompilerParams(dimension_semantics=("arbitrary",), vmem_limit_bytes=VMEM_LIMIT,
                                             has_side_effects=True),
        name="experts",
    )(blk_e, nblk, run_first, run_idx, next_e, xs, w1, w3, w2)


def _combine_kernel(x1_ref, wts_ref, yt_ref, g_ref, b_ref, out_ref, *, tile):
    T = tile
    wpad = jnp.concatenate([wts_ref[...], jnp.zeros((LANES - SUBLANES, T), F32)], axis=0)
    wt = wpad.T
    slot = lambda s: _unpack_bf16_pairs(jnp.concatenate(
        [yt_ref[pl.ds(s * PK_TILES + c, T, stride=2 * PK_TILES), :] for c in range(PK_TILES)], axis=1))
    y = wt[:, 0:1] * slot(0) + wt[:, 1:2] * slot(1)
    out_ref[...] = _layer_norm(DN_ALPHA * x1_ref[...] + y, g_ref[...], b_ref[...])


def _combine(x1f, wts, yt, g, b):
    N = x1f.shape[0]
    T = min(COMBINE_TILE, N)
    assert N % T == 0
    return pl.pallas_call(
        functools.partial(_combine_kernel, tile=T),
        grid=(N // T,),
        in_specs=[pl.BlockSpec((T, D_MODEL), lambda i: (i, 0)),
                  pl.BlockSpec((SUBLANES, T), lambda i: (0, i)),
                  pl.BlockSpec((T * 2 * PK_TILES, LANES), lambda i: (i, 0)),
                  _const_spec(g.shape), _const_spec(b.shape)],
        out_specs=pl.BlockSpec((T, D_MODEL), lambda i: (i, 0)),
        out_shape=jax.ShapeDtypeStruct((N, D_MODEL), F32),
        compiler_params=pltpu.CompilerParams(dimension_semantics=("arbitrary",)),
        name="combine",
    )(x1f, wts, yt, g, b)


def _odd_tiles(w):
    return jnp.pad(w, ((0, 0), (0, LANES))).astype(BF16)


def _pack_mixer_weights(w_in, b_in, rg_w_a, rg_w_x, rg_b_a, rg_b_x, gla_w_a2):
    wmain = _odd_tiles(w_in[:, :C_MAIN_END])
    wbg = _odd_tiles(w_in[:, C_GATES_START:])
    walr = jnp.pad(w_in[:, C_MAIN_END:C_GATES_START], ((0, 0), (0, LANES - GLA_RANK))).astype(BF16)
    bmain = b_in[None, :C_MAIN_END]
    bbg = b_in[None, C_GATES_START:]
    balr = jnp.pad(b_in[None, C_MAIN_END:C_GATES_START], ((0, 0), (0, LANES - GLA_RANK)))
    zero = jnp.zeros((RNN_BLOCK_W, RNN_BLOCK_W), w_in.dtype)
    tiles = []
    for p in range(RNN_BLOCKS // 2):
        top = jnp.concatenate([rg_w_a[2 * p], zero, rg_w_x[2 * p], zero], axis=1)
        bot = jnp.concatenate([zero, rg_w_a[2 * p + 1], zero, rg_w_x[2 * p + 1]], axis=1)
        tiles.append(jnp.concatenate([top, bot], axis=0))
    wgate = jnp.stack(tiles).astype(BF16)
    bgate = jnp.concatenate([rg_b_a, rg_b_x])[None, :]
    wa2 = jnp.concatenate([gla_w_a2, jnp.zeros((LANES - GLA_RANK, GLA_DK), gla_w_a2.dtype)], axis=0).astype(BF16)
    return (wmain, bmain, wbg, bbg, walr, balr), wgate, bgate, wa2


def _layer(x, w_in, b_in, conv_w, conv_b, rg_w_a, rg_b_a, rg_w_x, rg_b_x, rg_lambda, gla_w_a2, gla_b_a,
           gla_norm_g, w_proj_rnn, w_proj_gla, w_o, b_o, ln1_g, ln1_b, router_w_group, router_b_group,
           router_w_expert, router_b_expert, exp_w1, exp_w3, exp_w2, ln2_g, ln2_b):
    B, S, _ = x.shape
    N = B * S
    row = lambda p: p[None, :]

    w_slices, wgate, bgate, wa2 = _pack_mixer_weights(w_in, b_in, rg_w_a, rg_w_x, rg_b_a, rg_b_x, gla_w_a2)
    wr = jnp.concatenate([router_w_group.T, jnp.zeros((SUBLANES - N_GROUPS, D_MODEL), F32), router_w_expert.T],
                         axis=0).astype(BF16)
    br = jnp.concatenate([router_b_group, jnp.zeros((SUBLANES - N_GROUPS,), F32), router_b_expert])[:, None]
    x1f, xpk, info, wts, tcnt = _mixer(
        x, *w_slices, conv_w, row(conv_b), wgate, bgate, row(rg_lambda), wa2, row(gla_b_a), row(gla_norm_g),
        _odd_tiles(w_proj_rnn), _odd_tiles(w_proj_gla), _odd_tiles(w_o), row(b_o), row(ln1_g), row(ln1_b), wr, br)

    tcnt = tcnt[:, :, 0]
    nt = tcnt.shape[0]
    tot = jnp.sum(tcnt, axis=0)
    pcount = (tot + MOE_ROWS - 1) // MOE_ROWS * MOE_ROWS
    pend = jnp.cumsum(pcount)
    base = (pend - pcount)[None, :] + jnp.cumsum(tcnt, axis=0) - tcnt
    base_tok = jnp.repeat(base.T, N // nt, axis=1)
    experts_col = jnp.arange(N_EXPERTS, dtype=jnp.int32)[:, None, None]
    dest = jnp.sum(jnp.where(info[None, 0:2] == experts_col, base_tok[:, None, :], 0), axis=0) + info[2:4]
    nb = -(-(2 * N) // MOE_ROWS) + N_EXPERTS
    P = nb * MOE_ROWS
    nblk = (pend[-1] // MOE_ROWS).astype(jnp.int32)
    blk_start = jnp.minimum(jnp.arange(nb + 1, dtype=jnp.int32), nblk - 1) * MOE_ROWS
    blk_e = jnp.sum((blk_start[:, None] >= pend[None, :]).astype(jnp.int32), axis=1)
    blk_e = jnp.minimum(blk_e, N_EXPERTS - 1)

    last_blk = jnp.where(tot > 0, pend - MOE_ROWS, -1)
    tail_blk = nblk + jnp.arange(N_EXPERTS, dtype=jnp.int32)
    tail_blk = jnp.where(tail_blk < nb, tail_blk * MOE_ROWS, -1)
    xs = _dispatch(jnp.concatenate([last_blk, tail_blk]).astype(jnp.int32), dest, xpk, P)
    blk_i = jnp.arange(nb + 1, dtype=jnp.int32)
    experts_row = jnp.arange(N_EXPERTS, dtype=jnp.int32)
    run_first = ((blk_i < nblk) & ((blk_i == 0) | (blk_e != jnp.roll(blk_e, 1)))).astype(jnp.int32)
    run_idx = jnp.sum(jnp.where(blk_i[None, :] <= blk_i[:, None], run_first[None, :], 0), axis=1) - 1
    later = (experts_row[None, :] > experts_row[:, None]) & (tot[None, :] > 0)
    next_nonempty = jnp.min(jnp.where(later, experts_row[None, :], N_EXPERTS), axis=1)
    next_nonempty = jnp.where(next_nonempty < N_EXPERTS, next_nonempty, -1)
    next_e = jnp.sum(jnp.where(blk_e[:, None] == experts_row[None, :], next_nonempty[None, :], 0), axis=1)
    yt = _experts(blk_e, nblk[None], run_first, run_idx.astype(jnp.int32), next_e.astype(jnp.int32),
                  xs, exp_w1, exp_w3, exp_w2, N)
    out = _combine(x1f, wts, yt, row(ln2_g), row(ln2_b))
    return out.reshape(B, S, D_MODEL)


def kernel(x, w_in, b_in, conv_w, conv_b, rg_w_a, rg_b_a, rg_w_x, rg_b_x, rg_lambda, gla_w_a2, gla_b_a, gla_norm_g, w_proj_rnn, w_proj_gla, w_o, b_o, ln1_g, ln1_b, router_w_group, router_b_group, router_w_expert, router_b_expert, exp_w1, exp_w3, exp_w2, ln2_g, ln2_b):
    h = x
    for l in range(w_in.shape[0]):
        h = _layer(h, w_in[l], b_in[l], conv_w[l], conv_b[l], rg_w_a[l], rg_b_a[l], rg_w_x[l], rg_b_x[l],
                   rg_lambda[l], gla_w_a2[l], gla_b_a[l], gla_norm_g[l], w_proj_rnn[l], w_proj_gla[l], w_o[l],
                   b_o[l], ln1_g[l], ln1_b[l], router_w_group[l], router_b_group[l], router_w_expert[l],
                   router_b_expert[l], exp_w1[l], exp_w3[l], exp_w2[l], ln2_g[l], ln2_b[l])
    return h
```

```python
import functools

import jax
import jax.numpy as jnp
from jax import lax
from jax.experimental import pallas as pl
from jax.experimental.pallas import tpu as pltpu

F32 = jnp.float32
BF16 = jnp.bfloat16

D_MODEL = 1024
RNN_WIDTH = 1024
RNN_BLOCKS = 8
RNN_BLOCK_W = RNN_WIDTH // RNN_BLOCKS
CONV_WIDTH = 4
LRU_C = 8.0
GLA_HEADS = 4
GLA_DK = D_MODEL // 2
GLA_DV = D_MODEL
GLA_HEAD_K = GLA_DK // GLA_HEADS
GLA_HEAD_V = GLA_DV // GLA_HEADS
GLA_RANK = 16
GLA_TAU = 16.0
GLA_CHUNK = 64
N_GROUPS = 4
EXPERTS_PER_GROUP = 8
N_EXPERTS = N_GROUPS * EXPERTS_PER_GROUP
EXPERT_FF = 512
DN_ALPHA = 2.0 ** 0.25
LN_EPS = 1e-5
RMS_EPS = 1e-6

LANES = 128
SUBLANES = 8
VMEM_LIMIT = 56 * 1024 * 1024

C_RX, C_RY, C_Q, C_K, C_V, C_G = 0, 1024, 2048, 2560, 3072, 4096
C_GA, C_GB = 0, 1024
C_MAIN_END = 5120
C_GATES_START = C_MAIN_END + GLA_RANK

MIX_TILE = 256
MOE_ROWS = 512
DISPATCH_TILE = 2048
COMBINE_TILE = 1024
N_STAGE = 3
GATE_PAIR_W = 2 * RNN_BLOCK_W
PK_TILES = D_MODEL // 2 // LANES
TOK_ROWS = PK_TILES + 1
META_EXPERT, META_TOKEN, META_VALID = 0, 1, 2


def _sigmoid(v):
    return 0.5 * jnp.tanh(0.5 * v) + 0.5


def _softplus(v):
    return jnp.maximum(v, 0.0) + jnp.log1p(jnp.exp(-jnp.abs(v)))


def _layer_norm(v, g, b):
    mu = jnp.mean(v, axis=-1, keepdims=True)
    c = v - mu
    var = jnp.mean(c * c, axis=-1, keepdims=True)
    return c * lax.rsqrt(var + LN_EPS) * g + b


def _dot(a, b):
    return jnp.dot(a, b, preferred_element_type=F32)


def _dot_nt(a, b):
    return lax.dot_general(a, b, (((1,), (1,)), ((), ())), preferred_element_type=F32)


def _dot_tn(a, b):
    return lax.dot_general(a, b, (((0,), (0,)), ((), ())), preferred_element_type=F32)


def _pack_bf16_pairs(v):
    half = v.shape[1] // 2
    bits = pltpu.bitcast(v.astype(BF16).astype(F32), jnp.uint32)
    return (bits[:, :half] >> 16) | bits[:, half:]


def _unpack_bf16_pairs(u):
    lo = pltpu.bitcast(u << 16, F32)
    hi = pltpu.bitcast(u & jnp.uint32(0xFFFF0000), F32)
    return jnp.concatenate([lo, hi], axis=1)


def _const_spec(shape):
    nd = len(shape)
    return pl.BlockSpec(shape, lambda *_: (0,) * nd, pipeline_mode=pl.Buffered(1))


def _mixer_kernel(x_ref, wmain_ref, bmain_ref, wbg_ref, bbg_ref, walr_ref, balr_ref,
                  convw_ref, convb_ref, wgate_ref, bgate_ref, lam_ref,
                  wa2_ref, ba2_ref, gnorm_ref, wprnn_ref, wpgla_ref, wo_ref, bo_ref, ln1g_ref, ln1b_ref,
                  wr_ref, br_ref,
                  x1_ref, xpk_ref, info_ref, wts_ref, cnt_ref, rxbuf, hcar, st_ref, hbuf, zbuf,
                  *, tile, steps_per_seq):
    T = tile
    step = pl.program_id(0)

    @pl.when(step == 0)
    def _():
        zbuf[...] = jnp.zeros_like(zbuf)

    @pl.when(step % steps_per_seq == 0)
    def _():
        rxbuf[0:SUBLANES, :] = jnp.zeros((SUBLANES, RNN_WIDTH), F32)
        hcar[...] = jnp.zeros_like(hcar)
        st_ref[...] = jnp.zeros_like(st_ref)

    x = x_ref[...]
    xb = x.astype(BF16)

    def proj(c0, c1, w_ref=wmain_ref, b_ref=bmain_ref):
        return _dot(xb, w_ref[:, c0:c1]) + b_ref[:, c0:c1]

    rx = proj(C_RX, C_RX + RNN_WIDTH)
    rxbuf[SUBLANES:SUBLANES + T, :] = rx
    u = convb_ref[...] + convw_ref[CONV_WIDTH - 1:CONV_WIDTH, :] * rx
    for j in range(1, CONV_WIDTH):
        u = u + convw_ref[CONV_WIDTH - 1 - j:CONV_WIDTH - j, :] * rxbuf[SUBLANES - j:SUBLANES - j + T, :]
    rxbuf[0:SUBLANES, :] = rxbuf[T:T + SUBLANES, :]

    qk = proj(C_Q, C_Q + 2 * GLA_DK)
    q = qk[:, :GLA_DK] * (GLA_HEAD_K ** -0.5)
    k = qk[:, GLA_DK:]
    alr = proj(0, LANES, walr_ref, balr_ref)

    x1 = _layer_norm(zbuf[...], ln1g_ref[...], ln1b_ref[...])
    x1_ref[...] = x1
    info, wts, total = _route_tile(x1.astype(BF16), wr_ref[...], br_ref[...])
    info_ref[...] = info
    wts_ref[...] = wts
    cnt_ref[0] = jnp.broadcast_to(total, (N_EXPERTS, LANES))
    pk = _pack_bf16_pairs(x1)
    for c in range(PK_TILES):
        xpk_ref[pl.ds(c, T, stride=TOK_ROWS), :] = pk[:, c * LANES:(c + 1) * LANES]
    first_expert = jnp.broadcast_to(info[0:1, :].astype(F32), (LANES, T)).T.astype(jnp.int32)
    token = lax.broadcasted_iota(jnp.int32, (T, LANES), 0) + jnp.maximum(step - 1, 0) * T
    lane = lax.broadcasted_iota(jnp.int32, (T, LANES), 1)
    meta = jnp.where(lane == META_EXPERT, first_expert,
                     jnp.where(lane == META_TOKEN, token, jnp.where(lane == META_VALID, 1, 0)))
    xpk_ref[pl.ds(PK_TILES, T, stride=TOK_ROWS), :] = pltpu.bitcast(meta, jnp.uint32)

    r_parts, i_parts = [], []
    for p in range(RNN_BLOCKS // 2):
        up = u[:, GATE_PAIR_W * p:GATE_PAIR_W * (p + 1)].astype(BF16)
        gp = _dot(up, wgate_ref[p])
        r_parts.append(gp[:, :GATE_PAIR_W])
        i_parts.append(gp[:, GATE_PAIR_W:])
    r = _sigmoid(jnp.concatenate(r_parts, axis=1) + bgate_ref[:, :RNN_WIDTH])
    ig = _sigmoid(jnp.concatenate(i_parts, axis=1) + bgate_ref[:, RNN_WIDTH:])
    v = proj(C_V, C_V + GLA_DV)
    ry = proj(C_RY, C_RY + RNN_WIDTH)

    z = _dot(alr.astype(BF16), wa2_ref[...]) + ba2_ref[...]
    la = -_softplus(-z) * (1.0 / GLA_TAU)
    ri = lax.broadcasted_iota(jnp.int32, (T, T), 0)
    ci = lax.broadcasted_iota(jnp.int32, (T, T), 1)
    chunk_start = ri - (ri & (GLA_CHUNK - 1))
    tri = jnp.where((ci <= ri) & (ci >= chunk_start), 1.0, 0.0).astype(BF16)
    la_hi = la.astype(BF16)
    la_lo = (la - la_hi.astype(F32)).astype(BF16)
    bcum = _dot(tri, la_hi) + _dot(tri, la_lo)

    g = proj(C_G, C_G + GLA_DV)

    log_a = (-LRU_C) * r * _softplus(-lam_ref[...])
    a = jnp.exp(log_a)
    m2 = -jnp.tanh(log_a) * (1.0 + a * a)
    bv = jnp.where(m2 > 0.0, m2 * lax.rsqrt(m2), 0.0) * (ig * u)

    cr = lax.broadcasted_iota(jnp.int32, (GLA_CHUNK, GLA_CHUNK), 0)
    cc = lax.broadcasted_iota(jnp.int32, (GLA_CHUNK, GLA_CHUNK), 1)
    causal = cr >= cc
    n_chunks = T // GLA_CHUNK
    heads = [(slice(hd * GLA_HEAD_K, (hd + 1) * GLA_HEAD_K), slice(hd * GLA_HEAD_V, (hd + 1) * GLA_HEAD_V))
             for hd in range(GLA_HEADS)]
    qd_c, ki_c, ke_c, vv_c, dec_c = [], [], [], [], []
    for c in range(n_chunks):
        r0 = c * GLA_CHUNK
        bc = bcum[r0:r0 + GLA_CHUNK, :]
        bl = bcum[r0 + GLA_CHUNK - 1:r0 + GLA_CHUNK, :]
        kc = k[r0:r0 + GLA_CHUNK, :]
        qd_c.append((q[r0:r0 + GLA_CHUNK, :] * jnp.exp(bc)).astype(BF16))
        ki_c.append((kc * jnp.exp(-bc)).astype(BF16))
        ke_c.append((kc * jnp.exp(bl - bc)).astype(BF16))
        vv_c.append(v[r0:r0 + GLA_CHUNK, :].astype(BF16))
        dec_c.append(jnp.exp(bl))
    scores = [[jnp.where(causal, _dot_nt(qd_c[c][:, ks], ki_c[c][:, ks]), 0.0).astype(BF16) for ks, _ in heads]
              for c in range(n_chunks)]
    intra = [[_dot(scores[c][hd], vv_c[c][:, vs]) for hd, (_, vs) in enumerate(heads)] for c in range(n_chunks)]
    incr = [[_dot_tn(vv_c[c][:, vs], ke_c[c][:, ks]) for ks, vs in heads] for c in range(n_chunks)]

    n_groups = T // SUBLANES
    sub = lax.broadcasted_iota(jnp.int32, (n_groups, SUBLANES, RNN_WIDTH), 1)
    sa = a.reshape(n_groups, SUBLANES, RNN_WIDTH)
    sb = bv.reshape(n_groups, SUBLANES, RNN_WIDTH)
    for s in (1, 2, 4):
        keep = sub >= s
        sb = sa * jnp.where(keep, pltpu.roll(sb, s, 1), 0.0) + sb
        sa = sa * jnp.where(keep, pltpu.roll(sa, s, 1), 1.0)
    carry = hcar[0:1, :]
    for gi in range(n_groups):
        hg = sb[gi] + sa[gi] * carry
        hbuf[gi * SUBLANES:(gi + 1) * SUBLANES, :] = hg
        carry = hg[SUBLANES - 1:SUBLANES, :]
    hcar[0:1, :] = carry
    h = hbuf[...]
    ga = proj(C_GA, C_GA + D_MODEL, wbg_ref, bbg_ref)

    out_a = _dot((h * jax.nn.gelu(ry)).astype(BF16), wprnn_ref[:, :D_MODEL])
    gb = proj(C_GB, C_GB + D_MODEL, wbg_ref, bbg_ref)

    states = [st_ref[hd] for hd in range(GLA_HEADS)]
    o_chunks = []
    for c in range(n_chunks):
        o_heads = []
        for hd, (ks, _) in enumerate(heads):
            o_heads.append(intra[c][hd] + _dot_nt(qd_c[c][:, ks], states[hd].astype(BF16)))
            states[hd] = states[hd] * dec_c[c][:, ks] + incr[c][hd]
        o_chunks.append(jnp.concatenate(o_heads, axis=1))
    for hd in range(GLA_HEADS):
        st_ref[hd] = states[hd]
    o_all = jnp.concatenate(o_chunks, axis=0)

    o_parts = []
    for hd in range(GLA_HEADS):
        vs = slice(hd * GLA_HEAD_V, (hd + 1) * GLA_HEAD_V)
        oh = o_all[:, vs]
        ms = jnp.mean(oh * oh, axis=-1, keepdims=True)
        o_parts.append(oh * lax.rsqrt(ms + RMS_EPS) * gnorm_ref[:, vs])
    on = jnp.concatenate(o_parts, axis=1) * (g * _sigmoid(g))
    out_b = _dot(on.astype(BF16), wpgla_ref[:, :D_MODEL])

    merged = _sigmoid(ga) * out_a + _sigmoid(gb) * out_b
    y = _dot(merged.astype(BF16), wo_ref[:, :D_MODEL]) + bo_ref[...]
    zbuf[...] = DN_ALPHA * x + y


def _mixer(x, wmain, bmain, wbg, bbg, walr, balr, conv_w, conv_b, wgate, bgate, lam, wa2, ba2, gnorm,
           wprnn, wpgla, wo, bo, ln1g, ln1b, wr, br):
    B, S, _ = x.shape
    T = min(MIX_TILE, S)
    assert S % T == 0 and T % GLA_CHUNK == 0
    N = B * S
    nt = N // T
    mix_tile = lambda i: jnp.minimum(i, nt - 1)
    tail_tile = lambda i: jnp.maximum(i - 1, 0)
    weights = (wmain, bmain, wbg, bbg, walr, balr, conv_w, conv_b, wgate, bgate, lam, wa2, ba2, gnorm,
               wprnn, wpgla, wo, bo, ln1g, ln1b, wr, br)
    return pl.pallas_call(
        functools.partial(_mixer_kernel, tile=T, steps_per_seq=S // T),
        grid=(nt + 1,),
        in_specs=[pl.BlockSpec((T, D_MODEL), lambda i: (mix_tile(i), 0))] + [_const_spec(w.shape) for w in weights],
        out_specs=[pl.BlockSpec((T, D_MODEL), lambda i: (tail_tile(i), 0)),
                   pl.BlockSpec((T * TOK_ROWS, LANES), lambda i: (tail_tile(i), 0)),
                   pl.BlockSpec((SUBLANES, T), lambda i: (0, tail_tile(i))),
                   pl.BlockSpec((SUBLANES, T), lambda i: (0, tail_tile(i))),
                   pl.BlockSpec((1, N_EXPERTS, LANES), lambda i: (tail_tile(i), 0, 0))],
        out_shape=[jax.ShapeDtypeStruct((N, D_MODEL), F32),
                   jax.ShapeDtypeStruct((N * TOK_ROWS, LANES), jnp.uint32),
                   jax.ShapeDtypeStruct((SUBLANES, N), jnp.int32),
                   jax.ShapeDtypeStruct((SUBLANES, N), F32),
                   jax.ShapeDtypeStruct((nt, N_EXPERTS, LANES), jnp.int32)],
        scratch_shapes=[pltpu.VMEM((T + SUBLANES, RNN_WIDTH), F32),
                        pltpu.VMEM((SUBLANES, RNN_WIDTH), F32),
                        pltpu.VMEM((GLA_HEADS, GLA_HEAD_V, GLA_HEAD_K), F32),
                        pltpu.VMEM((T, RNN_WIDTH), F32),
                        pltpu.VMEM((T, D_MODEL), F32)],
        compiler_params=pltpu.CompilerParams(dimension_semantics=("arbitrary",), vmem_limit_bytes=VMEM_LIMIT),
        name="mixer",
    )(x.reshape(N, D_MODEL), *weights)


def _route_tile(x1b, wr, br):
    T = x1b.shape[0]
    logits = _dot_nt(wr, x1b) + br
    row8 = lax.broadcasted_iota(jnp.int32, (SUBLANES, T), 0)
    row8f = row8.astype(F32)
    neg = jnp.float32(-jnp.inf)
    first = lambda hit: jnp.min(jnp.where(hit, row8f, float(SUBLANES)), axis=0, keepdims=True)

    gl = jnp.where(row8 < N_GROUPS, logits[0:SUBLANES, :], neg)
    gmax = jnp.max(gl, axis=0, keepdims=True)
    grp = first(gl == gmax)
    p_grp = 1.0 / jnp.sum(jnp.exp(gl - gmax), axis=0, keepdims=True)

    e_sel = jnp.zeros((EXPERTS_PER_GROUP, T), F32)
    for gi in range(N_GROUPS):
        lo = SUBLANES + gi * EXPERTS_PER_GROUP
        e_sel = jnp.where(grp == float(gi), logits[lo:lo + EXPERTS_PER_GROUP, :], e_sel)
    m1 = jnp.max(e_sel, axis=0, keepdims=True)
    i1 = first(e_sel == m1)
    e_rest = jnp.where(row8f == i1, neg, e_sel)
    m2 = jnp.max(e_rest, axis=0, keepdims=True)
    i2 = first(e_rest == m2)
    e21 = jnp.exp(m2 - m1)
    p1 = 1.0 / (1.0 + e21)
    w0 = p_grp * p1
    w1 = p_grp * (e21 * p1)
    eid0 = (grp * EXPERTS_PER_GROUP + i1).astype(jnp.int32)
    eid1 = (grp * EXPERTS_PER_GROUP + i2).astype(jnp.int32)

    erow = lax.broadcasted_iota(jnp.int32, (N_EXPERTS, T), 0)
    oh0 = jnp.where(erow == eid0, 1.0, 0.0)
    oh1 = jnp.where(erow == eid1, 1.0, 0.0)
    both = oh0 + oh1
    ti = lax.broadcasted_iota(jnp.int32, (T, T), 0)
    tj = lax.broadcasted_iota(jnp.int32, (T, T), 1)
    before = jnp.where(ti < tj, 1.0, 0.0).astype(BF16)
    prior = _dot(both.astype(BF16), before)
    rank0 = jnp.sum(prior * oh0, axis=0, keepdims=True).astype(jnp.int32)
    rank1 = jnp.sum(prior * oh1, axis=0, keepdims=True).astype(jnp.int32)
    total = (prior[:, T - 1:T] + both[:, T - 1:T]).astype(jnp.int32)

    info = jnp.where(row8 == 0, eid0, jnp.where(row8 == 1, eid1, jnp.where(row8 == 2, rank0,
                     jnp.where(row8 == 3, rank1, 0))))
    return info, jnp.where(row8 == 0, w0, jnp.where(row8 == 1, w1, 0.0)), total


def _dispatch_kernel(zero_blk_ref, dest_ref, xpk_ref, xs_ref, zbuf, sem, zsem, *, tile):
    T = tile

    @pl.when(pl.program_id(0) == 0)
    def _():
        zbuf[...] = jnp.zeros_like(zbuf)

        def zero_copy(j):
            row = pl.multiple_of(jnp.maximum(zero_blk_ref[j], 0) * TOK_ROWS, MOE_ROWS * TOK_ROWS)
            return pltpu.make_async_copy(zbuf, xs_ref.at[pl.ds(row, MOE_ROWS * TOK_ROWS)], zsem)

        for j in range(2 * N_EXPERTS):
            @pl.when(zero_blk_ref[j] >= 0)
            def _():
                zero_copy(j).start()
        for j in range(2 * N_EXPERTS):
            @pl.when(zero_blk_ref[j] >= 0)
            def _():
                zero_copy(j).wait()

    for t in range(T):
        for kk in range(2):
            row = dest_ref[kk, t] * TOK_ROWS
            pltpu.make_async_copy(xpk_ref.at[pl.ds(t * TOK_ROWS, TOK_ROWS)], xs_ref.at[pl.ds(row, TOK_ROWS)],
                                  sem).start(priority=kk)
    for kk in range(2):
        pltpu.make_async_copy(xpk_ref, xs_ref.at[pl.ds(0, T * TOK_ROWS)], sem).wait()


def _dispatch(last_blk, dest, xpk, n_rows):
    N = xpk.shape[0] // TOK_ROWS
    T = min(DISPATCH_TILE, N)
    assert N % T == 0
    grid_spec = pltpu.PrefetchScalarGridSpec(
        num_scalar_prefetch=1,
        grid=(N // T,),
        in_specs=[pl.BlockSpec((2, T), lambda i, lb: (0, i), memory_space=pltpu.SMEM),
                  pl.BlockSpec((T * TOK_ROWS, LANES), lambda i, lb: (i, 0))],
        out_specs=pl.BlockSpec(memory_space=pl.ANY),
        scratch_shapes=[pltpu.VMEM((MOE_ROWS * TOK_ROWS, LANES), jnp.uint32),
                        pltpu.SemaphoreType.DMA(()), pltpu.SemaphoreType.DMA(())],
    )
    return pl.pallas_call(
        functools.partial(_dispatch_kernel, tile=T),
        grid_spec=grid_spec,
        out_shape=jax.ShapeDtypeStruct((n_rows * TOK_ROWS, LANES), jnp.uint32),
        compiler_params=pltpu.CompilerParams(dimension_semantics=("arbitrary",), has_side_effects=True),
        name="dispatch",
    )(last_blk, dest, xpk)


def _experts_kernel(blk_e_ref, nblk_ref, run_first_ref, run_idx_ref, next_e_ref,
                    xs_ref, w1_hbm, w3_hbm, w2_hbm, yt_ref,
                    w1b, w3b, w2b, w1f, w3f, w2f, stage, ids_v, ids_s, row_sem, ids_sem, w_sem, *, n_tokens):
    i = pl.program_id(0)
    R = MOE_ROWS
    SR = R * PK_TILES
    n_used = nblk_ref[0]
    spare = 2 * n_tokens
    slot = i % N_STAGE
    prev = (i + N_STAGE - 1) % N_STAGE
    prev2 = (i + N_STAGE - 2) % N_STAGE

    def ids_copy(s):
        return pltpu.make_async_copy(ids_v.at[pl.ds(s, 1)], ids_s.at[pl.ds(s, 1)], ids_sem)

    def row_copy(s, r, dst_row):
        src = pl.multiple_of((s * R + r) * PK_TILES, PK_TILES)
        dst = pl.multiple_of(dst_row * PK_TILES, PK_TILES)
        return pltpu.make_async_copy(stage.at[pl.ds(src, PK_TILES)], yt_ref.at[pl.ds(dst, PK_TILES)],
                                     row_sem.at[s]).start(priority=1)

    def drain_rows(s):
        pltpu.make_async_copy(stage.at[pl.ds(0, SR)], yt_ref.at[pl.ds(0, SR)], row_sem.at[s]).wait()

    def send_rows(s):
        for r in range(R):
            row_copy(s, r, ids_s[s, r])

    def send_rows_rolled(s, dst_of_row):
        def body(r, carry):
            row_copy(s, r, dst_of_row(r))
            return carry
        lax.fori_loop(0, R, body, 0)

    @pl.when(i == 0)
    def _():
        stage[...] = jnp.zeros_like(stage)
        ids_v[...] = (spare + lax.broadcasted_iota(jnp.int32, (SUBLANES, R), 0) * R
                      + lax.broadcasted_iota(jnp.int32, (SUBLANES, R), 1))
        ids_copy(N_STAGE - 1).start()
        for s in range(N_STAGE - 1):
            send_rows_rolled(s, lambda r, s=s: spare + s * R + r)

    def weight_copies(e, buf):
        return [pltpu.make_async_copy(src.at[e], dst.at[buf], w_sem.at[buf])
                for src, dst in ((w1_hbm, w1f), (w3_hbm, w3f), (w2_hbm, w2f))]

    @pl.when(i == 0)
    def _():
        for cp in weight_copies(blk_e_ref[0], 0):
            cp.start()

    @pl.when(run_first_ref[i] != 0)
    def _():
        buf = run_idx_ref[i] % 2
        for cp in weight_copies(blk_e_ref[i], buf):
            cp.wait()
        w1b[...] = w1f[buf].astype(BF16)
        w3b[...] = w3f[buf].astype(BF16)
        w2b[...] = w2f[buf].astype(BF16)

        @pl.when(next_e_ref[i] >= 0)
        def _():
            for cp in weight_copies(next_e_ref[i], 1 - buf):
                cp.start()

    @pl.when(i < n_used)
    def _():
        ids_copy(prev).wait()
        drain_rows(slot)
        send_rows(prev)

        meta = pltpu.bitcast(xs_ref[pl.ds(PK_TILES, R, stride=TOK_ROWS), :], jnp.int32).astype(F32).T
        first_expert = meta[META_EXPERT:META_EXPERT + 1, :]
        token = meta[META_TOKEN:META_TOKEN + 1, :]
        valid = meta[META_VALID:META_VALID + 1, :]
        second = jnp.where(first_expert != blk_e_ref[i].astype(F32), 1.0, 0.0)
        pad_row = (spare + slot * R + lax.broadcasted_iota(jnp.int32, (1, R), 1)).astype(F32)
        ids_v[pl.ds(slot, 1), :] = jnp.where(valid != 0.0, 2.0 * token + second, pad_row).astype(jnp.int32)
        ids_copy(slot).start()

        u = jnp.concatenate([xs_ref[pl.ds(c, R, stride=TOK_ROWS), :] for c in range(PK_TILES)], axis=1)
        xb = _unpack_bf16_pairs(u).astype(BF16)
        h1 = _dot(xb, w1b[...])
        h3 = _dot(xb, w3b[...])
        hact = (h1 * _sigmoid(h1)) * h3
        y = _dot(hact.astype(BF16), w2b[...])
        pk = _pack_bf16_pairs(y)
        for c in range(PK_TILES):
            stage[pl.ds(slot * SR + c, R, stride=PK_TILES), :] = pk[:, c * LANES:(c + 1) * LANES]

    @pl.when(i == n_used)
    def _():
        ids_copy(prev).wait()
        drain_rows(slot)
        send_rows_rolled(prev, lambda r: ids_s[prev, r])
        drain_rows(prev2)
        drain_rows(prev)


def _experts(blk_e, nblk, run_first, run_idx, next_e, xs, w1, w3, w2, n_tokens):
    P = xs.shape[0] // TOK_ROWS
    nb = P // MOE_ROWS
    assert blk_e.shape[0] == nb + 1
    grid_spec = pltpu.PrefetchScalarGridSpec(
        num_scalar_prefetch=5,
        grid=(nb + 1,),
        in_specs=[pl.BlockSpec((MOE_ROWS * TOK_ROWS, LANES), lambda i, be, n, *_: (jnp.minimum(i, n[0] - 1), 0)),
                  pl.BlockSpec(memory_space=pl.ANY), pl.BlockSpec(memory_space=pl.ANY),
                  pl.BlockSpec(memory_space=pl.ANY)],
        out_specs=pl.BlockSpec(memory_space=pl.ANY),
        scratch_shapes=[pltpu.VMEM((D_MODEL, EXPERT_FF), BF16), pltpu.VMEM((D_MODEL, EXPERT_FF), BF16),
                        pltpu.VMEM((EXPERT_FF, D_MODEL), BF16),
                        pltpu.VMEM((2, D_MODEL, EXPERT_FF), F32), pltpu.VMEM((2, D_MODEL, EXPERT_FF), F32),
                        pltpu.VMEM((2, EXPERT_FF, D_MODEL), F32),
                        pltpu.VMEM((N_STAGE * MOE_ROWS * PK_TILES, LANES), jnp.uint32),
                        pltpu.VMEM((SUBLANES, MOE_ROWS), jnp.int32),
                        pltpu.SMEM((N_STAGE, MOE_ROWS), jnp.int32),
                        pltpu.SemaphoreType.DMA((N_STAGE,)), pltpu.SemaphoreType.DMA(()),
                        pltpu.SemaphoreType.DMA((2,))],
    )
    return pl.pallas_call(
        functools.partial(_experts_kernel, n_tokens=n_tokens),
        grid_spec=grid_spec,
        out_shape=jax.ShapeDtypeStruct(((2 * n_tokens + N_STAGE * MOE_ROWS) * PK_TILES, LANES), jnp.uint32),
        compiler_params=pltpu.CompilerParams(dimension_semantics=("arbitrary",), vmem_limit_bytes=VMEM_LIMIT,
                                             has_side_effects=True),
        name="experts",
    )(blk_e, nblk, run_first, run_idx, next_e, xs, w1, w3, w2)


def _combine_kernel(x1_ref, wts_ref, yt_ref, g_ref, b_ref, out_ref, *, tile):
    T = tile
    wpad = jnp.concatenate([wts_ref[...], jnp.zeros((LANES - SUBLANES, T), F32)], axis=0)
    wt = wpad.T
    slot = lambda s: _unpack_bf16_pairs(jnp.concatenate(
        [yt_ref[pl.ds(s * PK_TILES + c, T, stride=2 * PK_TILES), :] for c in range(PK_TILES)], axis=1))
    y = wt[:, 0:1] * slot(0) + wt[:, 1:2] * slot(1)
    out_ref[...] = _layer_norm(DN_ALPHA * x1_ref[...] + y, g_ref[...], b_ref[...])


def _combine(x1f, wts, yt, g, b):
    N = x1f.shape[0]
    T = min(COMBINE_TILE, N)
    assert N % T == 0
    return pl.pallas_call(
        functools.partial(_combine_kernel, tile=T),
        grid=(N // T,),
        in_specs=[pl.BlockSpec((T, D_MODEL), lambda i: (i, 0)),
                  pl.BlockSpec((SUBLANES, T), lambda i: (0, i)),
                  pl.BlockSpec((T * 2 * PK_TILES, LANES), lambda i: (i, 0)),
                  _const_spec(g.shape), _const_spec(b.shape)],
        out_specs=pl.BlockSpec((T, D_MODEL), lambda i: (i, 0)),
        out_shape=jax.ShapeDtypeStruct((N, D_MODEL), F32),
        compiler_params=pltpu.CompilerParams(dimension_semantics=("arbitrary",)),
        name="combine",
    )(x1f, wts, yt, g, b)


def _odd_tiles(w):
    return jnp.pad(w, ((0, 0), (0, LANES))).astype(BF16)


def _pack_mixer_weights(w_in, b_in, rg_w_a, rg_w_x, rg_b_a, rg_b_x, gla_w_a2):
    wmain = _odd_tiles(w_in[:, :C_MAIN_END])
    wbg = _odd_tiles(w_in[:, C_GATES_START:])
    walr = jnp.pad(w_in[:, C_MAIN_END:C_GATES_START], ((0, 0), (0, LANES - GLA_RANK))).astype(BF16)
    bmain = b_in[None, :C_MAIN_END]
    bbg = b_in[None, C_GATES_START:]
    balr = jnp.pad(b_in[None, C_MAIN_END:C_GATES_START], ((0, 0), (0, LANES - GLA_RANK)))
    zero = jnp.zeros((RNN_BLOCK_W, RNN_BLOCK_W), w_in.dtype)
    tiles = []
    for p in range(RNN_BLOCKS // 2):
        top = jnp.concatenate([rg_w_a[2 * p], zero, rg_w_x[2 * p], zero], axis=1)
        bot = jnp.concatenate([zero, rg_w_a[2 * p + 1], zero, rg_w_x[2 * p + 1]], axis=1)
        tiles.append(jnp.concatenate([top, bot], axis=0))
    wgate = jnp.stack(tiles).astype(BF16)
    bgate = jnp.concatenate([rg_b_a, rg_b_x])[None, :]
    wa2 = jnp.concatenate([gla_w_a2, jnp.zeros((LANES - GLA_RANK, GLA_DK), gla_w_a2.dtype)], axis=0).astype(BF16)
    return (wmain, bmain, wbg, bbg, walr, balr), wgate, bgate, wa2


def _layer(x, w_in, b_in, conv_w, conv_b, rg_w_a, rg_b_a, rg_w_x, rg_b_x, rg_lambda, gla_w_a2, gla_b_a,
           gla_norm_g, w_proj_rnn, w_proj_gla, w_o, b_o, ln1_g, ln1_b, router_w_group, router_b_group,
           router_w_expert, router_b_expert, exp_w1, exp_w3, exp_w2, ln2_g, ln2_b):
    B, S, _ = x.shape
    N = B * S
    row = lambda p: p[None, :]

    w_slices, wgate, bgate, wa2 = _pack_mixer_weights(w_in, b_in, rg_w_a, rg_w_x, rg_b_a, rg_b_x, gla_w_a2)
    wr = jnp.concatenate([router_w_group.T, jnp.zeros((SUBLANES - N_GROUPS, D_MODEL), F32), router_w_expert.T],
                         axis=0).astype(BF16)
    br = jnp.concatenate([router_b_group, jnp.zeros((SUBLANES - N_GROUPS,), F32), router_b_expert])[:, None]
    x1f, xpk, info, wts, tcnt = _mixer(
        x, *w_slices, conv_w, row(conv_b), wgate, bgate, row(rg_lambda), wa2, row(gla_b_a), row(gla_norm_g),
        _odd_tiles(w_proj_rnn), _odd_tiles(w_proj_gla), _odd_tiles(w_o), row(b_o), row(ln1_g), row(ln1_b), wr, br)

    tcnt = tcnt[:, :, 0]
    nt = tcnt.shape[0]
    tot = jnp.sum(tcnt, axis=0)
    pcount = (tot + MOE_ROWS - 1) // MOE_ROWS * MOE_ROWS
    pend = jnp.cumsum(pcount)
    base = (pend - pcount)[None, :] + jnp.cumsum(tcnt, axis=0) - tcnt
    base_tok = jnp.repeat(base.T, N // nt, axis=1)
    experts_col = jnp.arange(N_EXPERTS, dtype=jnp.int32)[:, None, None]
    dest = jnp.sum(jnp.where(info[None, 0:2] == experts_col, base_tok[:, None, :], 0), axis=0) + info[2:4]
    nb = -(-(2 * N) // MOE_ROWS) + N_EXPERTS
    P = nb * MOE_ROWS
    nblk = (pend[-1] // MOE_ROWS).astype(jnp.int32)
    blk_start = jnp.minimum(jnp.arange(nb + 1, dtype=jnp.int32), nblk - 1) * MOE_ROWS
    blk_e = jnp.sum((blk_start[:, None] >= pend[None, :]).astype(jnp.int32), axis=1)
    blk_e = jnp.minimum(blk_e, N_EXPERTS - 1)

    last_blk = jnp.where(tot > 0, pend - MOE_ROWS, -1)
    tail_blk = nblk + jnp.arange(N_EXPERTS, dtype=jnp.int32)
    tail_blk = jnp.where(tail_blk < nb, tail_blk * MOE_ROWS, -1)
    xs = _dispatch(jnp.concatenate([last_blk, tail_blk]).astype(jnp.int32), dest, xpk, P)
    blk_i = jnp.arange(nb + 1, dtype=jnp.int32)
    experts_row = jnp.arange(N_EXPERTS, dtype=jnp.int32)
    run_first = ((blk_i < nblk) & ((blk_i == 0) | (blk_e != jnp.roll(blk_e, 1)))).astype(jnp.int32)
    run_idx = jnp.sum(jnp.where(blk_i[None, :] <= blk_i[:, None], run_first[None, :], 0), axis=1) - 1
    later = (experts_row[None, :] > experts_row[:, None]) & (tot[None, :] > 0)
    next_nonempty = jnp.min(jnp.where(later, experts_row[None, :], N_EXPERTS), axis=1)
    next_nonempty = jnp.where(next_nonempty < N_EXPERTS, next_nonempty, -1)
    next_e = jnp.sum(jnp.where(blk_e[:, None] == experts_row[None, :], next_nonempty[None, :], 0), axis=1)
    yt = _experts(blk_e, nblk[None], run_first, run_idx.astype(jnp.int32), next_e.astype(jnp.int32),
                  xs, exp_w1, exp_w3, exp_w2, N)
    out = _combine(x1f, wts, yt, row(ln2_g), row(ln2_b))
    return out.reshape(B, S, D_MODEL)


def kernel(x, w_in, b_in, conv_w, conv_b, rg_w_a, rg_b_a, rg_w_x, rg_b_x, rg_lambda, gla_w_a2, gla_b_a, gla_norm_g, w_proj_rnn, w_proj_gla, w_o, b_o, ln1_g, ln1_b, router_w_group, router_b_group, router_w_expert, router_b_expert, exp_w1, exp_w3, exp_w2, ln2_g, ln2_b):
    h = x
    for l in range(w_in.shape[0]):
        h = _layer(h, w_in[l], b_in[l], conv_w[l], conv_b[l], rg_w_a[l], rg_b_a[l], rg_w_x[l], rg_b_x[l],
                   rg_lambda[l], gla_w_a2[l], gla_b_a[l], gla_norm_g[l], w_proj_rnn[l], w_proj_gla[l], w_o[l],
                   b_o[l], ln1_g[l], ln1_b[l], router_w_group[l], router_b_group[l], router_w_expert[l],
                   router_b_expert[l], exp_w1[l], exp_w3[l], exp_w2[l], ln2_g[l], ln2_b[l])
    return h
```

```python
import functools

import jax
import jax.numpy as jnp
from jax import lax
from jax.experimental import pallas as pl
from jax.experimental.pallas import tpu as pltpu

F32 = jnp.float32
BF16 = jnp.bfloat16

D_MODEL = 1024
RNN_WIDTH = 1024
RNN_BLOCKS = 8
RNN_BLOCK_W = RNN_WIDTH // RNN_BLOCKS
CONV_WIDTH = 4
LRU_C = 8.0
GLA_HEADS = 4
GLA_DK = D_MODEL // 2
GLA_DV = D_MODEL
GLA_HEAD_K = GLA_DK // GLA_HEADS
GLA_HEAD_V = GLA_DV // GLA_HEADS
GLA_RANK = 16
GLA_TAU = 16.0
GLA_CHUNK = 64
N_GROUPS = 4
EXPERTS_PER_GROUP = 8
N_EXPERTS = N_GROUPS * EXPERTS_PER_GROUP
EXPERT_FF = 512
DN_ALPHA = 2.0 ** 0.25
LN_EPS = 1e-5
RMS_EPS = 1e-6

LANES = 128
SUBLANES = 8
VMEM_LIMIT = 56 * 1024 * 1024

C_RX, C_RY, C_Q, C_K, C_V, C_G = 0, 1024, 2048, 2560, 3072, 4096
C_GA, C_GB = 0, 1024
C_MAIN_END = 5120
C_GATES_START = C_MAIN_END + GLA_RANK

MIX_TILE = 256
MOE_ROWS = 512
DISPATCH_TILE = 2048
COMBINE_TILE = 1024
N_STAGE = 3
GATE_PAIR_W = 2 * RNN_BLOCK_W
PK_TILES = D_MODEL // 2 // LANES
TOK_ROWS = PK_TILES + 1
META_EXPERT, META_TOKEN, META_VALID = 0, 1, 2


def _sigmoid(v):
    return 0.5 * jnp.tanh(0.5 * v) + 0.5


def _softplus(v):
    return jnp.maximum(v, 0.0) + jnp.log1p(jnp.exp(-jnp.abs(v)))


def _layer_norm(v, g, b):
    mu = jnp.mean(v, axis=-1, keepdims=True)
    c = v - mu
    var = jnp.mean(c * c, axis=-1, keepdims=True)
    return c * lax.rsqrt(var + LN_EPS) * g + b


def _dot(a, b):
    return jnp.dot(a, b, preferred_element_type=F32)


def _dot_nt(a, b):
    return lax.dot_general(a, b, (((1,), (1,)), ((), ())), preferred_element_type=F32)


def _dot_tn(a, b):
    return lax.dot_general(a, b, (((0,), (0,)), ((), ())), preferred_element_type=F32)


def _pack_bf16_pairs(v):
    half = v.shape[1] // 2
    bits = pltpu.bitcast(v.astype(BF16).astype(F32), jnp.uint32)
    return (bits[:, :half] >> 16) | bits[:, half:]


def _unpack_bf16_pairs(u):
    lo = pltpu.bitcast(u << 16, F32)
    hi = pltpu.bitcast(u & jnp.uint32(0xFFFF0000), F32)
    return jnp.concatenate([lo, hi], axis=1)


def _const_spec(shape):
    nd = len(shape)
    return pl.BlockSpec(shape, lambda *_: (0,) * nd, pipeline_mode=pl.Buffered(1))


def _mixer_kernel(x_ref, wmain_ref, bmain_ref, wbg_ref, bbg_ref, walr_ref, balr_ref,
                  convw_ref, convb_ref, wgate_ref, bgate_ref, lam_ref,
                  wa2_ref, ba2_ref, gnorm_ref, wprnn_ref, wpgla_ref, wo_ref, bo_ref, ln1g_ref, ln1b_ref,
                  wr_ref, br_ref,
                  x1_ref, xpk_ref, info_ref, wts_ref, cnt_ref, rxbuf, hcar, st_ref, hbuf, zbuf,
                  *, tile, steps_per_seq):
    T = tile
    step = pl.program_id(0)

    @pl.when(step == 0)
    def _():
        zbuf[...] = jnp.zeros_like(zbuf)

    @pl.when(step % steps_per_seq == 0)
    def _():
        rxbuf[0:SUBLANES, :] = jnp.zeros((SUBLANES, RNN_WIDTH), F32)
        hcar[...] = jnp.zeros_like(hcar)
        st_ref[...] = jnp.zeros_like(st_ref)

    x = x_ref[...]
    xb = x.astype(BF16)

    def proj(c0, c1, w_ref=wmain_ref, b_ref=bmain_ref):
        return _dot(xb, w_ref[:, c0:c1]) + b_ref[:, c0:c1]

    rx = proj(C_RX, C_RX + RNN_WIDTH)
    rxbuf[SUBLANES:SUBLANES + T, :] = rx
    u = convb_ref[...] + convw_ref[CONV_WIDTH - 1:CONV_WIDTH, :] * rx
    for j in range(1, CONV_WIDTH):
        u = u + convw_ref[CONV_WIDTH - 1 - j:CONV_WIDTH - j, :] * rxbuf[SUBLANES - j:SUBLANES - j + T, :]
    rxbuf[0:SUBLANES, :] = rxbuf[T:T + SUBLANES, :]

    qk = proj(C_Q, C_Q + 2 * GLA_DK)
    q = qk[:, :GLA_DK] * (GLA_HEAD_K ** -0.5)
    k = qk[:, GLA_DK:]
    alr = proj(0, LANES, walr_ref, balr_ref)

    x1 = _layer_norm(zbuf[...], ln1g_ref[...], ln1b_ref[...])
    x1_ref[...] = x1
    info, wts, total = _route_tile(x1.astype(BF16), wr_ref[...], br_ref[...])
    info_ref[...] = info
    wts_ref[...] = wts
    cnt_ref[0] = jnp.broadcast_to(total, (N_EXPERTS, LANES))
    pk = _pack_bf16_pairs(x1)
    for c in range(PK_TILES):
        xpk_ref[pl.ds(c, T, stride=TOK_ROWS), :] = pk[:, c * LANES:(c + 1) * LANES]
    first_expert = jnp.broadcast_to(info[0:1, :].astype(F32), (LANES, T)).T.astype(jnp.int32)
    token = lax.broadcasted_iota(jnp.int32, (T, LANES), 0) + jnp.maximum(step - 1, 0) * T
    lane = lax.broadcasted_iota(jnp.int32, (T, LANES), 1)
    meta = jnp.where(lane == META_EXPERT, first_expert,
                     jnp.where(lane == META_TOKEN, token, jnp.where(lane == META_VALID, 1, 0)))
    xpk_ref[pl.ds(PK_TILES, T, stride=TOK_ROWS), :] = pltpu.bitcast(meta, jnp.uint32)

    r_parts, i_parts = [], []
    for p in range(RNN_BLOCKS // 2):
        up = u[:, GATE_PAIR_W * p:GATE_PAIR_W * (p + 1)].astype(BF16)
        gp = _dot(up, wgate_ref[p])
        r_parts.append(gp[:, :GATE_PAIR_W])
        i_parts.append(gp[:, GATE_PAIR_W:])
    r = _sigmoid(jnp.concatenate(r_parts, axis=1) + bgate_ref[:, :RNN_WIDTH])
    ig = _sigmoid(jnp.concatenate(i_parts, axis=1) + bgate_ref[:, RNN_WIDTH:])
    v = proj(C_V, C_V + GLA_DV)
    ry = proj(C_RY, C_RY + RNN_WIDTH)

    z = _dot(alr.astype(BF16), wa2_ref[...]) + ba2_ref[...]
    la = -_softplus(-z) * (1.0 / GLA_TAU)
    ri = lax.broadcasted_iota(jnp.int32, (T, T), 0)
    ci = lax.broadcasted_iota(jnp.int32, (T, T), 1)
    chunk_start = ri - (ri & (GLA_CHUNK - 1))
    tri = jnp.where((ci <= ri) & (ci >= chunk_start), 1.0, 0.0).astype(BF16)
    la_hi = la.astype(BF16)
    la_lo = (la - la_hi.astype(F32)).astype(BF16)
    bcum = _dot(tri, la_hi) + _dot(tri, la_lo)

    g = proj(C_G, C_G + GLA_DV)

    log_a = (-LRU_C) * r * _softplus(-lam_ref[...])
    a = jnp.exp(log_a)
    m2 = -jnp.tanh(log_a) * (1.0 + a * a)
    bv = jnp.where(m2 > 0.0, m2 * lax.rsqrt(m2), 0.0) * (ig * u)

    cr = lax.broadcasted_iota(jnp.int32, (GLA_CHUNK, GLA_CHUNK), 0)
    cc = lax.broadcasted_iota(jnp.int32, (GLA_CHUNK, GLA_CHUNK), 1)
    causal = cr >= cc
    n_chunks = T // GLA_CHUNK
    heads = [(slice(hd * GLA_HEAD_K, (hd + 1) * GLA_HEAD_K), slice(hd * GLA_HEAD_V, (hd + 1) * GLA_HEAD_V))
             for hd in range(GLA_HEADS)]
    qd_c, ki_c, ke_c, vv_c, dec_c = [], [], [], [], []
    for c in range(n_chunks):
        r0 = c * GLA_CHUNK
        bc = bcum[r0:r0 + GLA_CHUNK, :]
        bl = bcum[r0 + GLA_CHUNK - 1:r0 + GLA_CHUNK, :]
        kc = k[r0:r0 + GLA_CHUNK, :]
        qd_c.append((q[r0:r0 + GLA_CHUNK, :] * jnp.exp(bc)).astype(BF16))
        ki_c.append((kc * jnp.exp(-bc)).astype(BF16))
        ke_c.append((kc * jnp.exp(bl - bc)).astype(BF16))
        vv_c.append(v[r0:r0 + GLA_CHUNK, :].astype(BF16))
        dec_c.append(jnp.exp(bl))
    scores = [[jnp.where(causal, _dot_nt(qd_c[c][:, ks], ki_c[c][:, ks]), 0.0).astype(BF16) for ks, _ in heads]
              for c in range(n_chunks)]
    intra = [[_dot(scores[c][hd], vv_c[c][:, vs]) for hd, (_, vs) in enumerate(heads)] for c in range(n_chunks)]
    incr = [[_dot_tn(vv_c[c][:, vs], ke_c[c][:, ks]) for ks, vs in heads] for c in range(n_chunks)]

    n_groups = T // SUBLANES
    sub = lax.broadcasted_iota(jnp.int32, (n_groups, SUBLANES, RNN_WIDTH), 1)
    sa = a.reshape(n_groups, SUBLANES, RNN_WIDTH)
    sb = bv.reshape(n_groups, SUBLANES, RNN_WIDTH)
    for s in (1, 2, 4):
        keep = sub >= s
        sb = sa * jnp.where(keep, pltpu.roll(sb, s, 1), 0.0) + sb
        sa = sa * jnp.where(keep, pltpu.roll(sa, s, 1), 1.0)
    carry = hcar[0:1, :]
    for gi in range(n_groups):
        hg = sb[gi] + sa[gi] * carry
        hbuf[gi * SUBLANES:(gi + 1) * SUBLANES, :] = hg
        carry = hg[SUBLANES - 1:SUBLANES, :]
    hcar[0:1, :] = carry
    h = hbuf[...]
    ga = proj(C_GA, C_GA + D_MODEL, wbg_ref, bbg_ref)

    out_a = _dot((h * jax.nn.gelu(ry)).astype(BF16), wprnn_ref[:, :D_MODEL])
    gb = proj(C_GB, C_GB + D_MODEL, wbg_ref, bbg_ref)

    states = [st_ref[hd] for hd in range(GLA_HEADS)]
    o_chunks = []
    for c in range(n_chunks):
        o_heads = []
        for hd, (ks, _) in enumerate(heads):
            o_heads.append(intra[c][hd] + _dot_nt(qd_c[c][:, ks], states[hd].astype(BF16)))
            states[hd] = states[hd] * dec_c[c][:, ks] + incr[c][hd]
        o_chunks.append(jnp.concatenate(o_heads, axis=1))
    for hd in range(GLA_HEADS):
        st_ref[hd] = states[hd]
    o_all = jnp.concatenate(o_chunks, axis=0)

    o_parts = []
    for hd in range(GLA_HEADS):
        vs = slice(hd * GLA_HEAD_V, (hd + 1) * GLA_HEAD_V)
        oh = o_all[:, vs]
        ms = jnp.mean(oh * oh, axis=-1, keepdims=True)
        o_parts.append(oh * lax.rsqrt(ms + RMS_EPS) * gnorm_ref[:, vs])
    on = jnp.concatenate(o_parts, axis=1) * (g * _sigmoid(g))
    out_b = _dot(on.astype(BF16), wpgla_ref[:, :D_MODEL])

    merged = _sigmoid(ga) * out_a + _sigmoid(gb) * out_b
    y = _dot(merged.astype(BF16), wo_ref[:, :D_MODEL]) + bo_ref[...]
    zbuf[...] = DN_ALPHA * x + y


def _mixer(x, wmain, bmain, wbg, bbg, walr, balr, conv_w, conv_b, wgate, bgate, lam, wa2, ba2, gnorm,
           wprnn, wpgla, wo, bo, ln1g, ln1b, wr, br):
    B, S, _ = x.shape
    T = min(MIX_TILE, S)
    assert S % T == 0 and T % GLA_CHUNK == 0
    N = B * S
    nt = N // T
    mix_tile = lambda i: jnp.minimum(i, nt - 1)
    tail_tile = lambda i: jnp.maximum(i - 1, 0)
    weights = (wmain, bmain, wbg, bbg, walr, balr, conv_w, conv_b, wgate, bgate, lam, wa2, ba2, gnorm,
               wprnn, wpgla, wo, bo, ln1g, ln1b, wr, br)
    return pl.pallas_call(
        functools.partial(_mixer_kernel, tile=T, steps_per_seq=S // T),
        grid=(nt + 1,),
        in_specs=[pl.BlockSpec((T, D_MODEL), lambda i: (mix_tile(i), 0))] + [_const_spec(w.shape) for w in weights],
        out_specs=[pl.BlockSpec((T, D_MODEL), lambda i: (tail_tile(i), 0)),
                   pl.BlockSpec((T * TOK_ROWS, LANES), lambda i: (tail_tile(i), 0)),
                   pl.BlockSpec((SUBLANES, T), lambda i: (0, tail_tile(i))),
                   pl.BlockSpec((SUBLANES, T), lambda i: (0, tail_tile(i))),
                   pl.BlockSpec((1, N_EXPERTS, LANES), lambda i: (tail_tile(i), 0, 0))],
        out_shape=[jax.ShapeDtypeStruct((N, D_MODEL), F32),
                   jax.ShapeDtypeStruct((N * TOK_ROWS, LANES), jnp.uint32),
                   jax.ShapeDtypeStruct((SUBLANES, N), jnp.int32),
                   jax.ShapeDtypeStruct((SUBLANES, N), F32),
                   jax.ShapeDtypeStruct((nt, N_EXPERTS, LANES), jnp.int32)],
        scratch_shapes=[pltpu.VMEM((T + SUBLANES, RNN_WIDTH), F32),
                        pltpu.VMEM((SUBLANES, RNN_WIDTH), F32),
                        pltpu.VMEM((GLA_HEADS, GLA_HEAD_V, GLA_HEAD_K), F32),
                        pltpu.VMEM((T, RNN_WIDTH), F32),
                        pltpu.VMEM((T, D_MODEL), F32)],
        compiler_params=pltpu.CompilerParams(dimension_semantics=("arbitrary",), vmem_limit_bytes=VMEM_LIMIT),
        name="mixer",
    )(x.reshape(N, D_MODEL), *weights)


def _route_tile(x1b, wr, br):
    T = x1b.shape[0]
    logits = _dot_nt(wr, x1b) + br
    row8 = lax.broadcasted_iota(jnp.int32, (SUBLANES, T), 0)
    row8f = row8.astype(F32)
    neg = jnp.float32(-jnp.inf)
    first = lambda hit: jnp.min(jnp.where(hit, row8f, float(SUBLANES)), axis=0, keepdims=True)

    gl = jnp.where(row8 < N_GROUPS, logits[0:SUBLANES, :], neg)
    gmax = jnp.max(gl, axis=0, keepdims=True)
    grp = first(gl == gmax)
    p_grp = 1.0 / jnp.sum(jnp.exp(gl - gmax), axis=0, keepdims=True)

    e_sel = jnp.zeros((EXPERTS_PER_GROUP, T), F32)
    for gi in range(N_GROUPS):
        lo = SUBLANES + gi * EXPERTS_PER_GROUP
        e_sel = jnp.where(grp == float(gi), logits[lo:lo + EXPERTS_PER_GROUP, :], e_sel)
    m1 = jnp.max(e_sel, axis=0, keepdims=True)
    i1 = first(e_sel == m1)
    e_rest = jnp.where(row8f == i1, neg, e_sel)
    m2 = jnp.max(e_rest, axis=0, keepdims=True)
    i2 = first(e_rest == m2)
    e21 = jnp.exp(m2 - m1)
    p1 = 1.0 / (1.0 + e21)
    w0 = p_grp * p1
    w1 = p_grp * (e21 * p1)
    eid0 = (grp * EXPERTS_PER_GROUP + i1).astype(jnp.int32)
    eid1 = (grp * EXPERTS_PER_GROUP + i2).astype(jnp.int32)

    erow = lax.broadcasted_iota(jnp.int32, (N_EXPERTS, T), 0)
    oh0 = jnp.where(erow == eid0, 1.0, 0.0)
    oh1 = jnp.where(erow == eid1, 1.0, 0.0)
    both = oh0 + oh1
    ti = lax.broadcasted_iota(jnp.int32, (T, T), 0)
    tj = lax.broadcasted_iota(jnp.int32, (T, T), 1)
    before = jnp.where(ti < tj, 1.0, 0.0).astype(BF16)
    prior = _dot(both.astype(BF16), before)
    rank0 = jnp.sum(prior * oh0, axis=0, keepdims=True).astype(jnp.int32)
    rank1 = jnp.sum(prior * oh1, axis=0, keepdims=True).astype(jnp.int32)
    total = (prior[:, T - 1:T] + both[:, T - 1:T]).astype(jnp.int32)

    info = jnp.where(row8 == 0, eid0, jnp.where(row8 == 1, eid1, jnp.where(row8 == 2, rank0,
                     jnp.where(row8 == 3, rank1, 0))))
    return info, jnp.where(row8 == 0, w0, jnp.where(row8 == 1, w1, 0.0)), total


def _dispatch_kernel(zero_blk_ref, dest_ref, xpk_ref, xs_ref, zbuf, sem, zsem, *, tile):
    T = tile

    @pl.when(pl.program_id(0) == 0)
    def _():
        zbuf[...] = jnp.zeros_like(zbuf)

        def zero_copy(j):
            row = pl.multiple_of(jnp.maximum(zero_blk_ref[j], 0) * TOK_ROWS, MOE_ROWS * TOK_ROWS)
            return pltpu.make_async_copy(zbuf, xs_ref.at[pl.ds(row, MOE_ROWS * TOK_ROWS)], zsem)

        for j in range(2 * N_EXPERTS):
            @pl.when(zero_blk_ref[j] >= 0)
            def _():
                zero_copy(j).start()
        for j in range(2 * N_EXPERTS):
            @pl.when(zero_blk_ref[j] >= 0)
            def _():
                zero_copy(j).wait()

    for t in range(T):
        for kk in range(2):
            row = dest_ref[kk, t] * TOK_ROWS
            pltpu.make_async_copy(xpk_ref.at[pl.ds(t * TOK_ROWS, TOK_ROWS)], xs_ref.at[pl.ds(row, TOK_ROWS)],
                                  sem).start(priority=kk)
    for kk in range(2):
        pltpu.make_async_copy(xpk_ref, xs_ref.at[pl.ds(0, T * TOK_ROWS)], sem).wait()


def _dispatch(last_blk, dest, xpk, n_rows):
    N = xpk.shape[0] // TOK_ROWS
    T = min(DISPATCH_TILE, N)
    assert N % T == 0
    grid_spec = pltpu.PrefetchScalarGridSpec(
        num_scalar_prefetch=1,
        grid=(N // T,),
        in_specs=[pl.BlockSpec((2, T), lambda i, lb: (0, i), memory_space=pltpu.SMEM),
                  pl.BlockSpec((T * TOK_ROWS, LANES), lambda i, lb: (i, 0))],
        out_specs=pl.BlockSpec(memory_space=pl.ANY),
        scratch_shapes=[pltpu.VMEM((MOE_ROWS * TOK_ROWS, LANES), jnp.uint32),
                        pltpu.SemaphoreType.DMA(()), pltpu.SemaphoreType.DMA(())],
    )
    return pl.pallas_call(
        functools.partial(_dispatch_kernel, tile=T),
        grid_spec=grid_spec,
        out_shape=jax.ShapeDtypeStruct((n_rows * TOK_ROWS, LANES), jnp.uint32),
        compiler_params=pltpu.CompilerParams(dimension_semantics=("arbitrary",), has_side_effects=True),
        name="dispatch",
    )(last_blk, dest, xpk)


def _experts_kernel(blk_e_ref, nblk_ref, run_first_ref, run_idx_ref, next_e_ref,
                    xs_ref, w1_hbm, w3_hbm, w2_hbm, yt_ref,
                    w1b, w3b, w2b, w1f, w3f, w2f, stage, ids_v, ids_s, row_sem, ids_sem, w_sem, *, n_tokens):
    i = pl.program_id(0)
    R = MOE_ROWS
    SR = R * PK_TILES
    n_used = nblk_ref[0]
    spare = 2 * n_tokens
    slot = i % N_STAGE
    prev = (i + N_STAGE - 1) % N_STAGE
    prev2 = (i + N_STAGE - 2) % N_STAGE

    def ids_copy(s):
        return pltpu.make_async_copy(ids_v.at[pl.ds(s, 1)], ids_s.at[pl.ds(s, 1)], ids_sem)

    def row_copy(s, r, dst_row):
        src = pl.multiple_of((s * R + r) * PK_TILES, PK_TILES)
        dst = pl.multiple_of(dst_row * PK_TILES, PK_TILES)
        return pltpu.make_async_copy(stage.at[pl.ds(src, PK_TILES)], yt_ref.at[pl.ds(dst, PK_TILES)],
                                     row_sem.at[s]).start(priority=1)

    def drain_rows(s):
        pltpu.make_async_copy(stage.at[pl.ds(0, SR)], yt_ref.at[pl.ds(0, SR)], row_sem.at[s]).wait()

    def send_rows(s):
        for r in range(R):
            row_copy(s, r, ids_s[s, r])

    @pl.when(i == 0)
    def _():
        stage[...] = jnp.zeros_like(stage)
        ids_v[...] = (spare + lax.broadcasted_iota(jnp.int32, (SUBLANES, R), 0) * R
                      + lax.broadcasted_iota(jnp.int32, (SUBLANES, R), 1))
        ids_copy(N_STAGE - 1).start()
        for s in range(N_STAGE - 1):
            for r in range(R):
                row_copy(s, r, spare + s * R + r)

    def weight_copies(e, buf):
        return [pltpu.make_async_copy(src.at[e], dst.at[buf], w_sem.at[buf])
                for src, dst in ((w1_hbm, w1f), (w3_hbm, w3f), (w2_hbm, w2f))]

    @pl.when(i == 0)
    def _():
        for cp in weight_copies(blk_e_ref[0], 0):
            cp.start()

    @pl.when(run_first_ref[i] != 0)
    def _():
        buf = run_idx_ref[i] % 2
        for cp in weight_copies(blk_e_ref[i], buf):
            cp.wait()
        w1b[...] = w1f[buf].astype(BF16)
        w3b[...] = w3f[buf].astype(BF16)
        w2b[...] = w2f[buf].astype(BF16)

        @pl.when(next_e_ref[i] >= 0)
        def _():
            for cp in weight_copies(next_e_ref[i], 1 - buf):
                cp.start()

    @pl.when(i < n_used)
    def _():
        ids_copy(prev).wait()
        drain_rows(slot)
        send_rows(prev)

        meta = pltpu.bitcast(xs_ref[pl.ds(PK_TILES, R, stride=TOK_ROWS), :], jnp.int32).astype(F32).T
        first_expert = meta[META_EXPERT:META_EXPERT + 1, :]
        token = meta[META_TOKEN:META_TOKEN + 1, :]
        valid = meta[META_VALID:META_VALID + 1, :]
        second = jnp.where(first_expert != blk_e_ref[i].astype(F32), 1.0, 0.0)
        pad_row = (spare + slot * R + lax.broadcasted_iota(jnp.int32, (1, R), 1)).astype(F32)
        ids_v[pl.ds(slot, 1), :] = jnp.where(valid != 0.0, 2.0 * token + second, pad_row).astype(jnp.int32)
        ids_copy(slot).start()

        u = jnp.concatenate([xs_ref[pl.ds(c, R, stride=TOK_ROWS), :] for c in range(PK_TILES)], axis=1)
        xb = _unpack_bf16_pairs(u).astype(BF16)
        h1 = _dot(xb, w1b[...])
        h3 = _dot(xb, w3b[...])
        hact = (h1 * _sigmoid(h1)) * h3
        y = _dot(hact.astype(BF16), w2b[...])
        pk = _pack_bf16_pairs(y)
        for c in range(PK_TILES):
            stage[pl.ds(slot * SR + c, R, stride=PK_TILES), :] = pk[:, c * LANES:(c + 1) * LANES]

    @pl.when(i == n_used)
    def _():
        ids_copy(prev).wait()
        drain_rows(slot)
        send_rows(prev)
        drain_rows(prev2)
        drain_rows(prev)


def _experts_loop_kernel(blk_e_ref, nblk_ref, run_first_ref, run_idx_ref, next_e_ref,
                         xs_hbm, w1_hbm, w3_hbm, w2_hbm, yt_ref,
                         w1b, w3b, w2b, w1f, w3f, w2f, stage, ids_v, ids_s, row_sem, ids_sem, w_sem,
                         xbuf, x_sem, *, n_tokens):
    R = MOE_ROWS
    SR = R * PK_TILES
    XR = R * TOK_ROWS
    n_used = nblk_ref[0]
    spare = 2 * n_tokens

    def ids_copy(s):
        return pltpu.make_async_copy(ids_v.at[pl.ds(s, 1)], ids_s.at[pl.ds(s, 1)], ids_sem)

    def row_copy(s, r, dst_row):
        src = pl.multiple_of((s * R + r) * PK_TILES, PK_TILES)
        dst = pl.multiple_of(dst_row * PK_TILES, PK_TILES)
        return pltpu.make_async_copy(stage.at[pl.ds(src, PK_TILES)], yt_ref.at[pl.ds(dst, PK_TILES)],
                                     row_sem.at[s]).start(priority=1)

    def drain_rows(s):
        pltpu.make_async_copy(stage.at[pl.ds(0, SR)], yt_ref.at[pl.ds(0, SR)], row_sem.at[s]).wait()

    def send_rows(s):
        for r in range(R):
            row_copy(s, r, ids_s[s, r])

    def weight_copies(e, buf):
        return [pltpu.make_async_copy(src.at[e], dst.at[buf], w_sem.at[buf])
                for src, dst in ((w1_hbm, w1f), (w3_hbm, w3f), (w2_hbm, w2f))]

    def block_copy(blk, b):
        return pltpu.make_async_copy(xs_hbm.at[pl.ds(pl.multiple_of(blk * XR, XR), XR)],
                                     xbuf.at[pl.ds(pl.multiple_of(b * XR, XR), XR)], x_sem.at[b])

    stage[...] = jnp.zeros_like(stage)
    ids_v[...] = (spare + lax.broadcasted_iota(jnp.int32, (SUBLANES, R), 0) * R
                  + lax.broadcasted_iota(jnp.int32, (SUBLANES, R), 1))
    ids_copy(N_STAGE - 1).start()
    for s in range(N_STAGE - 1):
        for r in range(R):
            row_copy(s, r, spare + s * R + r)
    for cp in weight_copies(blk_e_ref[0], 0):
        cp.start()
    block_copy(0, 0).start()

    def block_step(i, carry):
        slot = i % N_STAGE
        prev = (i + N_STAGE - 1) % N_STAGE
        xslot = i % 2

        @pl.when(run_first_ref[i] != 0)
        def _():
            buf = run_idx_ref[i] % 2
            for cp in weight_copies(blk_e_ref[i], buf):
                cp.wait()
            w1b[...] = w1f[buf].astype(BF16)
            w3b[...] = w3f[buf].astype(BF16)
            w2b[...] = w2f[buf].astype(BF16)

            @pl.when(next_e_ref[i] >= 0)
            def _():
                for cp in weight_copies(next_e_ref[i], 1 - buf):
                    cp.start()

        block_copy(i, xslot).wait()

        @pl.when(i + 1 < n_used)
        def _():
            block_copy(i + 1, 1 - xslot).start()

        ids_copy(prev).wait()
        drain_rows(slot)
        send_rows(prev)

        xrow = xslot * XR
        meta = pltpu.bitcast(xbuf[pl.ds(xrow + PK_TILES, R, stride=TOK_ROWS), :], jnp.int32).astype(F32).T
        first_expert = meta[META_EXPERT:META_EXPERT + 1, :]
        token = meta[META_TOKEN:META_TOKEN + 1, :]
        valid = meta[META_VALID:META_VALID + 1, :]
        second = jnp.where(first_expert != blk_e_ref[i].astype(F32), 1.0, 0.0)
        pad_row = (spare + slot * R + lax.broadcasted_iota(jnp.int32, (1, R), 1)).astype(F32)
        ids_v[pl.ds(slot, 1), :] = jnp.where(valid != 0.0, 2.0 * token + second, pad_row).astype(jnp.int32)
        ids_copy(slot).start()

        u = jnp.concatenate([xbuf[pl.ds(xrow + c, R, stride=TOK_ROWS), :] for c in range(PK_TILES)], axis=1)
        xb = _unpack_bf16_pairs(u).astype(BF16)
        h1 = _dot(xb, w1b[...])
        h3 = _dot(xb, w3b[...])
        hact = (h1 * _sigmoid(h1)) * h3
        y = _dot(hact.astype(BF16), w2b[...])
        pk = _pack_bf16_pairs(y)
        for c in range(PK_TILES):
            stage[pl.ds(slot * SR + c, R, stride=PK_TILES), :] = pk[:, c * LANES:(c + 1) * LANES]
        return carry

    lax.fori_loop(0, n_used, block_step, 0)

    slot = n_used % N_STAGE
    prev = (n_used + N_STAGE - 1) % N_STAGE
    prev2 = (n_used + N_STAGE - 2) % N_STAGE
    ids_copy(prev).wait()
    drain_rows(slot)
    send_rows(prev)
    drain_rows(prev2)
    drain_rows(prev)


def _experts(blk_e, nblk, run_first, run_idx, next_e, xs, w1, w3, w2, n_tokens):
    P = xs.shape[0] // TOK_ROWS
    nb = P // MOE_ROWS
    assert blk_e.shape[0] == nb + 1
    grid_spec = pltpu.PrefetchScalarGridSpec(
        num_scalar_prefetch=5,
        grid=(1,),
        in_specs=[pl.BlockSpec(memory_space=pl.ANY),
                  pl.BlockSpec(memory_space=pl.ANY), pl.BlockSpec(memory_space=pl.ANY),
                  pl.BlockSpec(memory_space=pl.ANY)],
        out_specs=pl.BlockSpec(memory_space=pl.ANY),
        scratch_shapes=[pltpu.VMEM((D_MODEL, EXPERT_FF), BF16), pltpu.VMEM((D_MODEL, EXPERT_FF), BF16),
                        pltpu.VMEM((EXPERT_FF, D_MODEL), BF16),
                        pltpu.VMEM((2, D_MODEL, EXPERT_FF), F32), pltpu.VMEM((2, D_MODEL, EXPERT_FF), F32),
                        pltpu.VMEM((2, EXPERT_FF, D_MODEL), F32),
                        pltpu.VMEM((N_STAGE * MOE_ROWS * PK_TILES, LANES), jnp.uint32),
                        pltpu.VMEM((SUBLANES, MOE_ROWS), jnp.int32),
                        pltpu.SMEM((N_STAGE, MOE_ROWS), jnp.int32),
                        pltpu.SemaphoreType.DMA((N_STAGE,)), pltpu.SemaphoreType.DMA(()),
                        pltpu.SemaphoreType.DMA((2,)),
                        pltpu.VMEM((2 * MOE_ROWS * TOK_ROWS, LANES), jnp.uint32),
                        pltpu.SemaphoreType.DMA((2,))],
    )
    return pl.pallas_call(
        functools.partial(_experts_loop_kernel, n_tokens=n_tokens),
        grid_spec=grid_spec,
        out_shape=jax.ShapeDtypeStruct(((2 * n_tokens + N_STAGE * MOE_ROWS) * PK_TILES, LANES), jnp.uint32),
        compiler_params=pltpu.CompilerParams(dimension_semantics=("arbitrary",), vmem_limit_bytes=VMEM_LIMIT,
                                             has_side_effects=True),
        name="experts",
    )(blk_e, nblk, run_first, run_idx, next_e, xs, w1, w3, w2)


def _combine_kernel(x1_ref, wts_ref, yt_ref, g_ref, b_ref, out_ref, *, tile):
    T = tile
    wpad = jnp.concatenate([wts_ref[...], jnp.zeros((LANES - SUBLANES, T), F32)], axis=0)
    wt = wpad.T
    slot = lambda s: _unpack_bf16_pairs(jnp.concatenate(
        [yt_ref[pl.ds(s * PK_TILES + c, T, stride=2 * PK_TILES), :] for c in range(PK_TILES)], axis=1))
    y = wt[:, 0:1] * slot(0) + wt[:, 1:2] * slot(1)
    out_ref[...] = _layer_norm(DN_ALPHA * x1_ref[...] + y, g_ref[...], b_ref[...])


def _combine(x1f, wts, yt, g, b):
    N = x1f.shape[0]
    T = min(COMBINE_TILE, N)
    assert N % T == 0
    return pl.pallas_call(
        functools.partial(_combine_kernel, tile=T),
        grid=(N // T,),
        in_specs=[pl.BlockSpec((T, D_MODEL), lambda i: (i, 0)),
                  pl.BlockSpec((SUBLANES, T), lambda i: (0, i)),
                  pl.BlockSpec((T * 2 * PK_TILES, LANES), lambda i: (i, 0)),
                  _const_spec(g.shape), _const_spec(b.shape)],
        out_specs=pl.BlockSpec((T, D_MODEL), lambda i: (i, 0)),
        out_shape=jax.ShapeDtypeStruct((N, D_MODEL), F32),
        compiler_params=pltpu.CompilerParams(dimension_semantics=("arbitrary",)),
        name="combine",
    )(x1f, wts, yt, g, b)


def _odd_tiles(w):
    return jnp.pad(w, ((0, 0), (0, LANES))).astype(BF16)


def _pack_mixer_weights(w_in, b_in, rg_w_a, rg_w_x, rg_b_a, rg_b_x, gla_w_a2):
    wmain = _odd_tiles(w_in[:, :C_MAIN_END])
    wbg = _odd_tiles(w_in[:, C_GATES_START:])
    walr = jnp.pad(w_in[:, C_MAIN_END:C_GATES_START], ((0, 0), (0, LANES - GLA_RANK))).astype(BF16)
    bmain = b_in[None, :C_MAIN_END]
    bbg = b_in[None, C_GATES_START:]
    balr = jnp.pad(b_in[None, C_MAIN_END:C_GATES_START], ((0, 0), (0, LANES - GLA_RANK)))
    zero = jnp.zeros((RNN_BLOCK_W, RNN_BLOCK_W), w_in.dtype)
    tiles = []
    for p in range(RNN_BLOCKS // 2):
        top = jnp.concatenate([rg_w_a[2 * p], zero, rg_w_x[2 * p], zero], axis=1)
        bot = jnp.concatenate([zero, rg_w_a[2 * p + 1], zero, rg_w_x[2 * p + 1]], axis=1)
        tiles.append(jnp.concatenate([top, bot], axis=0))
    wgate = jnp.stack(tiles).astype(BF16)
    bgate = jnp.concatenate([rg_b_a, rg_b_x])[None, :]
    wa2 = jnp.concatenate([gla_w_a2, jnp.zeros((LANES - GLA_RANK, GLA_DK), gla_w_a2.dtype)], axis=0).astype(BF16)
    return (wmain, bmain, wbg, bbg, walr, balr), wgate, bgate, wa2


def _layer(x, w_in, b_in, conv_w, conv_b, rg_w_a, rg_b_a, rg_w_x, rg_b_x, rg_lambda, gla_w_a2, gla_b_a,
           gla_norm_g, w_proj_rnn, w_proj_gla, w_o, b_o, ln1_g, ln1_b, router_w_group, router_b_group,
           router_w_expert, router_b_expert, exp_w1, exp_w3, exp_w2, ln2_g, ln2_b):
    B, S, _ = x.shape
    N = B * S
    row = lambda p: p[None, :]

    w_slices, wgate, bgate, wa2 = _pack_mixer_weights(w_in, b_in, rg_w_a, rg_w_x, rg_b_a, rg_b_x, gla_w_a2)
    wr = jnp.concatenate([router_w_group.T, jnp.zeros((SUBLANES - N_GROUPS, D_MODEL), F32), router_w_expert.T],
                         axis=0).astype(BF16)
    br = jnp.concatenate([router_b_group, jnp.zeros((SUBLANES - N_GROUPS,), F32), router_b_expert])[:, None]
    x1f, xpk, info, wts, tcnt = _mixer(
        x, *w_slices, conv_w, row(conv_b), wgate, bgate, row(rg_lambda), wa2, row(gla_b_a), row(gla_norm_g),
        _odd_tiles(w_proj_rnn), _odd_tiles(w_proj_gla), _odd_tiles(w_o), row(b_o), row(ln1_g), row(ln1_b), wr, br)

    tcnt = tcnt[:, :, 0]
    nt = tcnt.shape[0]
    tot = jnp.sum(tcnt, axis=0)
    pcount = (tot + MOE_ROWS - 1) // MOE_ROWS * MOE_ROWS
    pend = jnp.cumsum(pcount)
    base = (pend - pcount)[None, :] + jnp.cumsum(tcnt, axis=0) - tcnt
    base_tok = jnp.repeat(base.T, N // nt, axis=1)
    experts_col = jnp.arange(N_EXPERTS, dtype=jnp.int32)[:, None, None]
    dest = jnp.sum(jnp.where(info[None, 0:2] == experts_col, base_tok[:, None, :], 0), axis=0) + info[2:4]
    nb = -(-(2 * N) // MOE_ROWS) + N_EXPERTS
    P = nb * MOE_ROWS
    nblk = (pend[-1] // MOE_ROWS).astype(jnp.int32)
    blk_start = jnp.minimum(jnp.arange(nb + 1, dtype=jnp.int32), nblk - 1) * MOE_ROWS
    blk_e = jnp.sum((blk_start[:, None] >= pend[None, :]).astype(jnp.int32), axis=1)
    blk_e = jnp.minimum(blk_e, N_EXPERTS - 1)

    last_blk = jnp.where(tot > 0, pend - MOE_ROWS, -1)
    tail_blk = nblk + jnp.arange(N_EXPERTS, dtype=jnp.int32)
    tail_blk = jnp.where(tail_blk < nb, tail_blk * MOE_ROWS, -1)
    xs = _dispatch(jnp.concatenate([last_blk, tail_blk]).astype(jnp.int32), dest, xpk, P)
    blk_i = jnp.arange(nb + 1, dtype=jnp.int32)
    experts_row = jnp.arange(N_EXPERTS, dtype=jnp.int32)
    run_first = ((blk_i < nblk) & ((blk_i == 0) | (blk_e != jnp.roll(blk_e, 1)))).astype(jnp.int32)
    run_idx = jnp.sum(jnp.where(blk_i[None, :] <= blk_i[:, None], run_first[None, :], 0), axis=1) - 1
    later = (experts_row[None, :] > experts_row[:, None]) & (tot[None, :] > 0)
    next_nonempty = jnp.min(jnp.where(later, experts_row[None, :], N_EXPERTS), axis=1)
    next_nonempty = jnp.where(next_nonempty < N_EXPERTS, next_nonempty, -1)
    next_e = jnp.sum(jnp.where(blk_e[:, None] == experts_row[None, :], next_nonempty[None, :], 0), axis=1)
    yt = _experts(blk_e, nblk[None], run_first, run_idx.astype(jnp.int32), next_e.astype(jnp.int32),
                  xs, exp_w1, exp_w3, exp_w2, N)
    out = _combine(x1f, wts, yt, row(ln2_g), row(ln2_b))
    return out.reshape(B, S, D_MODEL)


def kernel(x, w_in, b_in, conv_w, conv_b, rg_w_a, rg_b_a, rg_w_x, rg_b_x, rg_lambda, gla_w_a2, gla_b_a, gla_norm_g, w_proj_rnn, w_proj_gla, w_o, b_o, ln1_g, ln1_b, router_w_group, router_b_group, router_w_expert, router_b_expert, exp_w1, exp_w3, exp_w2, ln2_g, ln2_b):
    h = x
    for l in range(w_in.shape[0]):
        h = _layer(h, w_in[l], b_in[l], conv_w[l], conv_b[l], rg_w_a[l], rg_b_a[l], rg_w_x[l], rg_b_x[l],
                   rg_lambda[l], gla_w_a2[l], gla_b_a[l], gla_norm_g[l], w_proj_rnn[l], w_proj_gla[l], w_o[l],
                   b_o[l], ln1_g[l], ln1_b[l], router_w_group[l], router_b_group[l], router_w_expert[l],
                   router_b_expert[l], exp_w1[l], exp_w3[l], exp_w2[l], ln2_g[l], ln2_b[l])
    return h
```

```python
import functools

import jax
import jax.numpy as jnp
from jax import lax
from jax.experimental import pallas as pl
from jax.experimental.pallas import tpu as pltpu

F32 = jnp.float32
BF16 = jnp.bfloat16

D_MODEL = 1024
RNN_WIDTH = 1024
RNN_BLOCKS = 8
RNN_BLOCK_W = RNN_WIDTH // RNN_BLOCKS
CONV_WIDTH = 4
LRU_C = 8.0
GLA_HEADS = 4
GLA_DK = D_MODEL // 2
GLA_DV = D_MODEL
GLA_HEAD_K = GLA_DK // GLA_HEADS
GLA_HEAD_V = GLA_DV // GLA_HEADS
GLA_RANK = 16
GLA_TAU = 16.0
GLA_CHUNK = 64
N_GROUPS = 4
EXPERTS_PER_GROUP = 8
N_EXPERTS = N_GROUPS * EXPERTS_PER_GROUP
EXPERT_FF = 512
DN_ALPHA = 2.0 ** 0.25
LN_EPS = 1e-5
RMS_EPS = 1e-6

LANES = 128
SUBLANES = 8
VMEM_LIMIT = 56 * 1024 * 1024

C_RX, C_RY, C_Q, C_K, C_V, C_G = 0, 1024, 2048, 2560, 3072, 4096
C_GA, C_GB = 0, 1024
C_MAIN_END = 5120
C_GATES_START = C_MAIN_END + GLA_RANK

MIX_TILE = 256
MOE_ROWS = 512
DISPATCH_TILE = 2048
COMBINE_TILE = 1024
N_STAGE = 3
GATE_PAIR_W = 2 * RNN_BLOCK_W
PK_TILES = D_MODEL // 2 // LANES
TOK_ROWS = PK_TILES + 1
META_EXPERT, META_TOKEN, META_VALID = 0, 1, 2


def _sigmoid(v):
    return 0.5 * jnp.tanh(0.5 * v) + 0.5


def _softplus(v):
    return jnp.maximum(v, 0.0) + jnp.log1p(jnp.exp(-jnp.abs(v)))


def _layer_norm(v, g, b):
    mu = jnp.mean(v, axis=-1, keepdims=True)
    c = v - mu
    var = jnp.mean(c * c, axis=-1, keepdims=True)
    return c * lax.rsqrt(var + LN_EPS) * g + b


def _dot(a, b):
    return jnp.dot(a, b, preferred_element_type=F32)


def _dot_nt(a, b):
    return lax.dot_general(a, b, (((1,), (1,)), ((), ())), preferred_element_type=F32)


def _dot_tn(a, b):
    return lax.dot_general(a, b, (((0,), (0,)), ((), ())), preferred_element_type=F32)


def _pack_bf16_pairs(v):
    half = v.shape[1] // 2
    bits = pltpu.bitcast(v.astype(BF16).astype(F32), jnp.uint32)
    return (bits[:, :half] >> 16) | bits[:, half:]


def _unpack_bf16_pairs(u):
    lo = pltpu.bitcast(u << 16, F32)
    hi = pltpu.bitcast(u & jnp.uint32(0xFFFF0000), F32)
    return jnp.concatenate([lo, hi], axis=1)


def _const_spec(shape):
    nd = len(shape)
    return pl.BlockSpec(shape, lambda *_: (0,) * nd, pipeline_mode=pl.Buffered(1))


def _mixer_kernel(x_ref, wmain_ref, bmain_ref, wbg_ref, bbg_ref, walr_ref, balr_ref,
                  convw_ref, convb_ref, wgate_ref, bgate_ref, lam_ref,
                  wa2_ref, ba2_ref, gnorm_ref, wprnn_ref, wpgla_ref, wo_ref, bo_ref, ln1g_ref, ln1b_ref,
                  wr_ref, br_ref,
                  x1_ref, xpk_ref, info_ref, wts_ref, cnt_ref, rxbuf, hcar, st_ref, hbuf, zbuf,
                  *, tile, steps_per_seq):
    T = tile
    step = pl.program_id(0)

    @pl.when(step == 0)
    def _():
        zbuf[...] = jnp.zeros_like(zbuf)

    @pl.when(step % steps_per_seq == 0)
    def _():
        rxbuf[0:SUBLANES, :] = jnp.zeros((SUBLANES, RNN_WIDTH), F32)
        hcar[...] = jnp.zeros_like(hcar)
        st_ref[...] = jnp.zeros_like(st_ref)

    x = x_ref[...]
    xb = x.astype(BF16)

    def proj(c0, c1, w_ref=wmain_ref, b_ref=bmain_ref):
        return _dot(xb, w_ref[:, c0:c1]) + b_ref[:, c0:c1]

    rx = proj(C_RX, C_RX + RNN_WIDTH)
    rxbuf[SUBLANES:SUBLANES + T, :] = rx
    u = convb_ref[...] + convw_ref[CONV_WIDTH - 1:CONV_WIDTH, :] * rx
    for j in range(1, CONV_WIDTH):
        u = u + convw_ref[CONV_WIDTH - 1 - j:CONV_WIDTH - j, :] * rxbuf[SUBLANES - j:SUBLANES - j + T, :]
    rxbuf[0:SUBLANES, :] = rxbuf[T:T + SUBLANES, :]

    qk = proj(C_Q, C_Q + 2 * GLA_DK)
    q = qk[:, :GLA_DK] * (GLA_HEAD_K ** -0.5)
    k = qk[:, GLA_DK:]
    alr = proj(0, LANES, walr_ref, balr_ref)

    x1 = _layer_norm(zbuf[...], ln1g_ref[...], ln1b_ref[...])
    x1_ref[...] = x1
    info, wts, total = _route_tile(x1.astype(BF16), wr_ref[...], br_ref[...])
    info_ref[...] = info
    wts_ref[...] = wts
    cnt_ref[0] = jnp.broadcast_to(total, (N_EXPERTS, LANES))
    pk = _pack_bf16_pairs(x1)
    for c in range(PK_TILES):
        xpk_ref[pl.ds(c, T, stride=TOK_ROWS), :] = pk[:, c * LANES:(c + 1) * LANES]
    first_expert = jnp.broadcast_to(info[0:1, :].astype(F32), (LANES, T)).T.astype(jnp.int32)
    token = lax.broadcasted_iota(jnp.int32, (T, LANES), 0) + jnp.maximum(step - 1, 0) * T
    lane = lax.broadcasted_iota(jnp.int32, (T, LANES), 1)
    meta = jnp.where(lane == META_EXPERT, first_expert,
                     jnp.where(lane == META_TOKEN, token, jnp.where(lane == META_VALID, 1, 0)))
    xpk_ref[pl.ds(PK_TILES, T, stride=TOK_ROWS), :] = pltpu.bitcast(meta, jnp.uint32)

    r_parts, i_parts = [], []
    for p in range(RNN_BLOCKS // 2):
        up = u[:, GATE_PAIR_W * p:GATE_PAIR_W * (p + 1)].astype(BF16)
        gp = _dot(up, wgate_ref[p])
        r_parts.append(gp[:, :GATE_PAIR_W])
        i_parts.append(gp[:, GATE_PAIR_W:])
    r = _sigmoid(jnp.concatenate(r_parts, axis=1) + bgate_ref[:, :RNN_WIDTH])
    ig = _sigmoid(jnp.concatenate(i_parts, axis=1) + bgate_ref[:, RNN_WIDTH:])
    v = proj(C_V, C_V + GLA_DV)
    ry = proj(C_RY, C_RY + RNN_WIDTH)

    z = _dot(alr.astype(BF16), wa2_ref[...]) + ba2_ref[...]
    la = -_softplus(-z) * (1.0 / GLA_TAU)
    ri = lax.broadcasted_iota(jnp.int32, (T, T), 0)
    ci = lax.broadcasted_iota(jnp.int32, (T, T), 1)
    chunk_start = ri - (ri & (GLA_CHUNK - 1))
    tri = jnp.where((ci <= ri) & (ci >= chunk_start), 1.0, 0.0).astype(BF16)
    la_hi = la.astype(BF16)
    la_lo = (la - la_hi.astype(F32)).astype(BF16)
    bcum = _dot(tri, la_hi) + _dot(tri, la_lo)

    g = proj(C_G, C_G + GLA_DV)

    log_a = (-LRU_C) * r * _softplus(-lam_ref[...])
    a = jnp.exp(log_a)
    m2 = -jnp.tanh(log_a) * (1.0 + a * a)
    bv = jnp.where(m2 > 0.0, m2 * lax.rsqrt(m2), 0.0) * (ig * u)

    cr = lax.broadcasted_iota(jnp.int32, (GLA_CHUNK, GLA_CHUNK), 0)
    cc = lax.broadcasted_iota(jnp.int32, (GLA_CHUNK, GLA_CHUNK), 1)
    causal = cr >= cc
    n_chunks = T // GLA_CHUNK
    heads = [(slice(hd * GLA_HEAD_K, (hd + 1) * GLA_HEAD_K), slice(hd * GLA_HEAD_V, (hd + 1) * GLA_HEAD_V))
             for hd in range(GLA_HEADS)]
    qd_c, ki_c, ke_c, vv_c, dec_c = [], [], [], [], []
    for c in range(n_chunks):
        r0 = c * GLA_CHUNK
        bc = bcum[r0:r0 + GLA_CHUNK, :]
        bl = bcum[r0 + GLA_CHUNK - 1:r0 + GLA_CHUNK, :]
        kc = k[r0:r0 + GLA_CHUNK, :]
        qd_c.append((q[r0:r0 + GLA_CHUNK, :] * jnp.exp(bc)).astype(BF16))
        ki_c.append((kc * jnp.exp(-bc)).astype(BF16))
        ke_c.append((kc * jnp.exp(bl - bc)).astype(BF16))
        vv_c.append(v[r0:r0 + GLA_CHUNK, :].astype(BF16))
        dec_c.append(jnp.exp(bl))
    scores = [[jnp.where(causal, _dot_nt(qd_c[c][:, ks], ki_c[c][:, ks]), 0.0).astype(BF16) for ks, _ in heads]
              for c in range(n_chunks)]
    intra = [[_dot(scores[c][hd], vv_c[c][:, vs]) for hd, (_, vs) in enumerate(heads)] for c in range(n_chunks)]
    incr = [[_dot_tn(vv_c[c][:, vs], ke_c[c][:, ks]) for ks, vs in heads] for c in range(n_chunks)]

    n_groups = T // SUBLANES
    sub = lax.broadcasted_iota(jnp.int32, (n_groups, SUBLANES, RNN_WIDTH), 1)
    sa = a.reshape(n_groups, SUBLANES, RNN_WIDTH)
    sb = bv.reshape(n_groups, SUBLANES, RNN_WIDTH)
    for s in (1, 2, 4):
        keep = sub >= s
        sb = sa * jnp.where(keep, pltpu.roll(sb, s, 1), 0.0) + sb
        sa = sa * jnp.where(keep, pltpu.roll(sa, s, 1), 1.0)
    carry = hcar[0:1, :]
    for gi in range(n_groups):
        hg = sb[gi] + sa[gi] * carry
        hbuf[gi * SUBLANES:(gi + 1) * SUBLANES, :] = hg
        carry = hg[SUBLANES - 1:SUBLANES, :]
    hcar[0:1, :] = carry
    h = hbuf[...]
    ga = proj(C_GA, C_GA + D_MODEL, wbg_ref, bbg_ref)

    out_a = _dot((h * jax.nn.gelu(ry)).astype(BF16), wprnn_ref[:, :D_MODEL])
    gb = proj(C_GB, C_GB + D_MODEL, wbg_ref, bbg_ref)

    states = [st_ref[hd] for hd in range(GLA_HEADS)]
    o_chunks = []
    for c in range(n_chunks):
        o_heads = []
        for hd, (ks, _) in enumerate(heads):
            o_heads.append(intra[c][hd] + _dot_nt(qd_c[c][:, ks], states[hd].astype(BF16)))
            states[hd] = states[hd] * dec_c[c][:, ks] + incr[c][hd]
        o_chunks.append(jnp.concatenate(o_heads, axis=1))
    for hd in range(GLA_HEADS):
        st_ref[hd] = states[hd]
    o_all = jnp.concatenate(o_chunks, axis=0)

    o_parts = []
    for hd in range(GLA_HEADS):
        vs = slice(hd * GLA_HEAD_V, (hd + 1) * GLA_HEAD_V)
        oh = o_all[:, vs]
        ms = jnp.mean(oh * oh, axis=-1, keepdims=True)
        o_parts.append(oh * lax.rsqrt(ms + RMS_EPS) * gnorm_ref[:, vs])
    on = jnp.concatenate(o_parts, axis=1) * (g * _sigmoid(g))
    out_b = _dot(on.astype(BF16), wpgla_ref[:, :D_MODEL])

    merged = _sigmoid(ga) * out_a + _sigmoid(gb) * out_b
    y = _dot(merged.astype(BF16), wo_ref[:, :D_MODEL]) + bo_ref[...]
    zbuf[...] = DN_ALPHA * x + y


def _mixer(x, wmain, bmain, wbg, bbg, walr, balr, conv_w, conv_b, wgate, bgate, lam, wa2, ba2, gnorm,
           wprnn, wpgla, wo, bo, ln1g, ln1b, wr, br):
    B, S, _ = x.shape
    T = min(MIX_TILE, S)
    assert S % T == 0 and T % GLA_CHUNK == 0
    N = B * S
    nt = N // T
    mix_tile = lambda i: jnp.minimum(i, nt - 1)
    tail_tile = lambda i: jnp.maximum(i - 1, 0)
    weights = (wmain, bmain, wbg, bbg, walr, balr, conv_w, conv_b, wgate, bgate, lam, wa2, ba2, gnorm,
               wprnn, wpgla, wo, bo, ln1g, ln1b, wr, br)
    return pl.pallas_call(
        functools.partial(_mixer_kernel, tile=T, steps_per_seq=S // T),
        grid=(nt + 1,),
        in_specs=[pl.BlockSpec((T, D_MODEL), lambda i: (mix_tile(i), 0))] + [_const_spec(w.shape) for w in weights],
        out_specs=[pl.BlockSpec((T, D_MODEL), lambda i: (tail_tile(i), 0)),
                   pl.BlockSpec((T * TOK_ROWS, LANES), lambda i: (tail_tile(i), 0)),
                   pl.BlockSpec((SUBLANES, T), lambda i: (0, tail_tile(i))),
                   pl.BlockSpec((SUBLANES, T), lambda i: (0, tail_tile(i))),
                   pl.BlockSpec((1, N_EXPERTS, LANES), lambda i: (tail_tile(i), 0, 0))],
        out_shape=[jax.ShapeDtypeStruct((N, D_MODEL), F32),
                   jax.ShapeDtypeStruct((N * TOK_ROWS, LANES), jnp.uint32),
                   jax.ShapeDtypeStruct((SUBLANES, N), jnp.int32),
                   jax.ShapeDtypeStruct((SUBLANES, N), F32),
                   jax.ShapeDtypeStruct((nt, N_EXPERTS, LANES), jnp.int32)],
        scratch_shapes=[pltpu.VMEM((T + SUBLANES, RNN_WIDTH), F32),
                        pltpu.VMEM((SUBLANES, RNN_WIDTH), F32),
                        pltpu.VMEM((GLA_HEADS, GLA_HEAD_V, GLA_HEAD_K), F32),
                        pltpu.VMEM((T, RNN_WIDTH), F32),
                        pltpu.VMEM((T, D_MODEL), F32)],
        compiler_params=pltpu.CompilerParams(dimension_semantics=("arbitrary",), vmem_limit_bytes=VMEM_LIMIT),
        name="mixer",
    )(x.reshape(N, D_MODEL), *weights)


def _route_tile(x1b, wr, br):
    T = x1b.shape[0]
    logits = _dot_nt(wr, x1b) + br
    row8 = lax.broadcasted_iota(jnp.int32, (SUBLANES, T), 0)
    row8f = row8.astype(F32)
    neg = jnp.float32(-jnp.inf)
    first = lambda hit: jnp.min(jnp.where(hit, row8f, float(SUBLANES)), axis=0, keepdims=True)

    gl = jnp.where(row8 < N_GROUPS, logits[0:SUBLANES, :], neg)
    gmax = jnp.max(gl, axis=0, keepdims=True)
    grp = first(gl == gmax)
    p_grp = 1.0 / jnp.sum(jnp.exp(gl - gmax), axis=0, keepdims=True)

    e_sel = jnp.zeros((EXPERTS_PER_GROUP, T), F32)
    for gi in range(N_GROUPS):
        lo = SUBLANES + gi * EXPERTS_PER_GROUP
        e_sel = jnp.where(grp == float(gi), logits[lo:lo + EXPERTS_PER_GROUP, :], e_sel)
    m1 = jnp.max(e_sel, axis=0, keepdims=True)
    i1 = first(e_sel == m1)
    e_rest = jnp.where(row8f == i1, neg, e_sel)
    m2 = jnp.max(e_rest, axis=0, keepdims=True)
    i2 = first(e_rest == m2)
    e21 = jnp.exp(m2 - m1)
    p1 = 1.0 / (1.0 + e21)
    w0 = p_grp * p1
    w1 = p_grp * (e21 * p1)
    eid0 = (grp * EXPERTS_PER_GROUP + i1).astype(jnp.int32)
    eid1 = (grp * EXPERTS_PER_GROUP + i2).astype(jnp.int32)

    erow = lax.broadcasted_iota(jnp.int32, (N_EXPERTS, T), 0)
    oh0 = jnp.where(erow == eid0, 1.0, 0.0)
    oh1 = jnp.where(erow == eid1, 1.0, 0.0)
    both = oh0 + oh1
    ti = lax.broadcasted_iota(jnp.int32, (T, T), 0)
    tj = lax.broadcasted_iota(jnp.int32, (T, T), 1)
    before = jnp.where(ti < tj, 1.0, 0.0).astype(BF16)
    prior = _dot(both.astype(BF16), before)
    rank0 = jnp.sum(prior * oh0, axis=0, keepdims=True).astype(jnp.int32)
    rank1 = jnp.sum(prior * oh1, axis=0, keepdims=True).astype(jnp.int32)
    total = (prior[:, T - 1:T] + both[:, T - 1:T]).astype(jnp.int32)

    info = jnp.where(row8 == 0, eid0, jnp.where(row8 == 1, eid1, jnp.where(row8 == 2, rank0,
                     jnp.where(row8 == 3, rank1, 0))))
    return info, jnp.where(row8 == 0, w0, jnp.where(row8 == 1, w1, 0.0)), total


def _dispatch_kernel(zero_blk_ref, dest_ref, xpk_ref, xs_ref, zbuf, sem, zsem, *, tile):
    T = tile

    @pl.when(pl.program_id(0) == 0)
    def _():
        zbuf[...] = jnp.zeros_like(zbuf)

        def zero_copy(j):
            row = pl.multiple_of(jnp.maximum(zero_blk_ref[j], 0) * TOK_ROWS, MOE_ROWS * TOK_ROWS)
            return pltpu.make_async_copy(zbuf, xs_ref.at[pl.ds(row, MOE_ROWS * TOK_ROWS)], zsem)

        for j in range(2 * N_EXPERTS):
            @pl.when(zero_blk_ref[j] >= 0)
            def _():
                zero_copy(j).start()
        for j in range(2 * N_EXPERTS):
            @pl.when(zero_blk_ref[j] >= 0)
            def _():
                zero_copy(j).wait()

    for t in range(T):
        for kk in range(2):
            row = dest_ref[kk, t] * TOK_ROWS
            pltpu.make_async_copy(xpk_ref.at[pl.ds(t * TOK_ROWS, TOK_ROWS)], xs_ref.at[pl.ds(row, TOK_ROWS)],
                                  sem).start(priority=kk)
    for kk in range(2):
        pltpu.make_async_copy(xpk_ref, xs_ref.at[pl.ds(0, T * TOK_ROWS)], sem).wait()


def _dispatch(last_blk, dest, xpk, n_rows):
    N = xpk.shape[0] // TOK_ROWS
    T = min(DISPATCH_TILE, N)
    assert N % T == 0
    grid_spec = pltpu.PrefetchScalarGridSpec(
        num_scalar_prefetch=1,
        grid=(N // T,),
        in_specs=[pl.BlockSpec((2, T), lambda i, lb: (0, i), memory_space=pltpu.SMEM),
                  pl.BlockSpec((T * TOK_ROWS, LANES), lambda i, lb: (i, 0))],
        out_specs=pl.BlockSpec(memory_space=pl.ANY),
        scratch_shapes=[pltpu.VMEM((MOE_ROWS * TOK_ROWS, LANES), jnp.uint32),
                        pltpu.SemaphoreType.DMA(()), pltpu.SemaphoreType.DMA(())],
    )
    return pl.pallas_call(
        functools.partial(_dispatch_kernel, tile=T),
        grid_spec=grid_spec,
        out_shape=jax.ShapeDtypeStruct((n_rows * TOK_ROWS, LANES), jnp.uint32),
        compiler_params=pltpu.CompilerParams(dimension_semantics=("arbitrary",), has_side_effects=True),
        name="dispatch",
    )(last_blk, dest, xpk)


def _experts_kernel(blk_e_ref, nblk_ref, run_first_ref, run_idx_ref, next_e_ref,
                    xs_ref, w1_hbm, w3_hbm, w2_hbm, yt_ref,
                    w1b, w3b, w2b, w1f, w3f, w2f, stage, ids_v, ids_s, row_sem, ids_sem, w_sem, *, n_tokens):
    i = pl.program_id(0)
    R = MOE_ROWS
    SR = R * PK_TILES
    n_used = nblk_ref[0]
    spare = 2 * n_tokens
    slot = i % N_STAGE
    prev = (i + N_STAGE - 1) % N_STAGE
    prev2 = (i + N_STAGE - 2) % N_STAGE

    def ids_copy(s):
        return pltpu.make_async_copy(ids_v.at[pl.ds(s, 1)], ids_s.at[pl.ds(s, 1)], ids_sem)

    def row_copy(s, r, dst_row):
        src = pl.multiple_of((s * R + r) * PK_TILES, PK_TILES)
        dst = pl.multiple_of(dst_row * PK_TILES, PK_TILES)
        return pltpu.make_async_copy(stage.at[pl.ds(src, PK_TILES)], yt_ref.at[pl.ds(dst, PK_TILES)],
                                     row_sem.at[s]).start(priority=1)

    def drain_rows(s):
        pltpu.make_async_copy(stage.at[pl.ds(0, SR)], yt_ref.at[pl.ds(0, SR)], row_sem.at[s]).wait()

    def send_rows(s):
        for r in range(R):
            row_copy(s, r, ids_s[s, r])

    @pl.when(i == 0)
    def _():
        stage[...] = jnp.zeros_like(stage)
        ids_v[...] = (spare + lax.broadcasted_iota(jnp.int32, (SUBLANES, R), 0) * R
                      + lax.broadcasted_iota(jnp.int32, (SUBLANES, R), 1))
        ids_copy(N_STAGE - 1).start()
        for s in range(N_STAGE - 1):
            for r in range(R):
                row_copy(s, r, spare + s * R + r)

    def weight_copies(e, buf):
        return [pltpu.make_async_copy(src.at[e], dst.at[buf], w_sem.at[buf])
                for src, dst in ((w1_hbm, w1f), (w3_hbm, w3f), (w2_hbm, w2f))]

    @pl.when(i == 0)
    def _():
        for cp in weight_copies(blk_e_ref[0], 0):
            cp.start()

    @pl.when(run_first_ref[i] != 0)
    def _():
        buf = run_idx_ref[i] % 2
        for cp in weight_copies(blk_e_ref[i], buf):
            cp.wait()
        w1b[...] = w1f[buf].astype(BF16)
        w3b[...] = w3f[buf].astype(BF16)
        w2b[...] = w2f[buf].astype(BF16)

        @pl.when(next_e_ref[i] >= 0)
        def _():
            for cp in weight_copies(next_e_ref[i], 1 - buf):
                cp.start()

    @pl.when(i < n_used)
    def _():
        ids_copy(prev).wait()
        drain_rows(slot)
        send_rows(prev)

        meta = pltpu.bitcast(xs_ref[pl.ds(PK_TILES, R, stride=TOK_ROWS), :], jnp.int32).astype(F32).T
        first_expert = meta[META_EXPERT:META_EXPERT + 1, :]
        token = meta[META_TOKEN:META_TOKEN + 1, :]
        valid = meta[META_VALID:META_VALID + 1, :]
        second = jnp.where(first_expert != blk_e_ref[i].astype(F32), 1.0, 0.0)
        pad_row = (spare + slot * R + lax.broadcasted_iota(jnp.int32, (1, R), 1)).astype(F32)
        ids_v[pl.ds(slot, 1), :] = jnp.where(valid != 0.0, 2.0 * token + second, pad_row).astype(jnp.int32)
        ids_copy(slot).start()

        H = R // 2
        for half in range(2):
            u = jnp.concatenate([xs_ref[pl.ds(half * H * TOK_ROWS + c, H, stride=TOK_ROWS), :]
                                 for c in range(PK_TILES)], axis=1)
            xb = _unpack_bf16_pairs(u).astype(BF16)
            h1 = _dot(xb, w1b[...])
            h3 = _dot(xb, w3b[...])
            hact = (h1 * _sigmoid(h1)) * h3
            y = _dot(hact.astype(BF16), w2b[...])
            pk = _pack_bf16_pairs(y)
            for c in range(PK_TILES):
                stage[pl.ds(slot * SR + half * H * PK_TILES + c, H, stride=PK_TILES), :] = (
                    pk[:, c * LANES:(c + 1) * LANES])

    @pl.when(i == n_used)
    def _():
        ids_copy(prev).wait()
        drain_rows(slot)
        send_rows(prev)
        drain_rows(prev2)
        drain_rows(prev)


def _experts(blk_e, nblk, run_first, run_idx, next_e, xs, w1, w3, w2, n_tokens):
    P = xs.shape[0] // TOK_ROWS
    nb = P // MOE_ROWS
    assert blk_e.shape[0] == nb + 1
    grid_spec = pltpu.PrefetchScalarGridSpec(
        num_scalar_prefetch=5,
        grid=(nb + 1,),
        in_specs=[pl.BlockSpec((MOE_ROWS * TOK_ROWS, LANES), lambda i, be, n, *_: (jnp.minimum(i, n[0] - 1), 0)),
                  pl.BlockSpec(memory_space=pl.ANY), pl.BlockSpec(memory_space=pl.ANY),
                  pl.BlockSpec(memory_space=pl.ANY)],
        out_specs=pl.BlockSpec(memory_space=pl.ANY),
        scratch_shapes=[pltpu.VMEM((D_MODEL, EXPERT_FF), BF16), pltpu.VMEM((D_MODEL, EXPERT_FF), BF16),
                        pltpu.VMEM((EXPERT_FF, D_MODEL), BF16),
                        pltpu.VMEM((2, D_MODEL, EXPERT_FF), F32), pltpu.VMEM((2, D_MODEL, EXPERT_FF), F32),
                        pltpu.VMEM((2, EXPERT_FF, D_MODEL), F32),
                        pltpu.VMEM((N_STAGE * MOE_ROWS * PK_TILES, LANES), jnp.uint32),
                        pltpu.VMEM((SUBLANES, MOE_ROWS), jnp.int32),
                        pltpu.SMEM((N_STAGE, MOE_ROWS), jnp.int32),
                        pltpu.SemaphoreType.DMA((N_STAGE,)), pltpu.SemaphoreType.DMA(()),
                        pltpu.SemaphoreType.DMA((2,))],
    )
    return pl.pallas_call(
        functools.partial(_experts_kernel, n_tokens=n_tokens),
        grid_spec=grid_spec,
        out_shape=jax.ShapeDtypeStruct(((2 * n_tokens + N_STAGE * MOE_ROWS) * PK_TILES, LANES), jnp.uint32),
        compiler_params=pltpu.CompilerParams(dimension_semantics=("arbitrary",), vmem_limit_bytes=VMEM_LIMIT,
                                             has_side_effects=True),
        name="experts",
    )(blk_e, nblk, run_first, run_idx, next_e, xs, w1, w3, w2)


def _combine_kernel(x1_ref, wts_ref, yt_ref, g_ref, b_ref, out_ref, *, tile):
    T = tile
    wpad = jnp.concatenate([wts_ref[...], jnp.zeros((LANES - SUBLANES, T), F32)], axis=0)
    wt = wpad.T
    slot = lambda s: _unpack_bf16_pairs(jnp.concatenate(
        [yt_ref[pl.ds(s * PK_TILES + c, T, stride=2 * PK_TILES), :] for c in range(PK_TILES)], axis=1))
    y = wt[:, 0:1] * slot(0) + wt[:, 1:2] * slot(1)
    out_ref[...] = _layer_norm(DN_ALPHA * x1_ref[...] + y, g_ref[...], b_ref[...])


def _combine(x1f, wts, yt, g, b):
    N = x1f.shape[0]
    T = min(COMBINE_TILE, N)
    assert N % T == 0
    return pl.pallas_call(
        functools.partial(_combine_kernel, tile=T),
        grid=(N // T,),
        in_specs=[pl.BlockSpec((T, D_MODEL), lambda i: (i, 0)),
                  pl.BlockSpec((SUBLANES, T), lambda i: (0, i)),
                  pl.BlockSpec((T * 2 * PK_TILES, LANES), lambda i: (i, 0)),
                  _const_spec(g.shape), _const_spec(b.shape)],
        out_specs=pl.BlockSpec((T, D_MODEL), lambda i: (i, 0)),
        out_shape=jax.ShapeDtypeStruct((N, D_MODEL), F32),
        compiler_params=pltpu.CompilerParams(dimension_semantics=("arbitrary",)),
        name="combine",
    )(x1f, wts, yt, g, b)


def _odd_tiles(w):
    return jnp.pad(w, ((0, 0), (0, LANES))).astype(BF16)


def _pack_mixer_weights(w_in, b_in, rg_w_a, rg_w_x, rg_b_a, rg_b_x, gla_w_a2):
    wmain = _odd_tiles(w_in[:, :C_MAIN_END])
    wbg = _odd_tiles(w_in[:, C_GATES_START:])
    walr = jnp.pad(w_in[:, C_MAIN_END:C_GATES_START], ((0, 0), (0, LANES - GLA_RANK))).astype(BF16)
    bmain = b_in[None, :C_MAIN_END]
    bbg = b_in[None, C_GATES_START:]
    balr = jnp.pad(b_in[None, C_MAIN_END:C_GATES_START], ((0, 0), (0, LANES - GLA_RANK)))
    zero = jnp.zeros((RNN_BLOCK_W, RNN_BLOCK_W), w_in.dtype)
    tiles = []
    for p in range(RNN_BLOCKS // 2):
        top = jnp.concatenate([rg_w_a[2 * p], zero, rg_w_x[2 * p], zero], axis=1)
        bot = jnp.concatenate([zero, rg_w_a[2 * p + 1], zero, rg_w_x[2 * p + 1]], axis=1)
        tiles.append(jnp.concatenate([top, bot], axis=0))
    wgate = jnp.stack(tiles).astype(BF16)
    bgate = jnp.concatenate([rg_b_a, rg_b_x])[None, :]
    wa2 = jnp.concatenate([gla_w_a2, jnp.zeros((LANES - GLA_RANK, GLA_DK), gla_w_a2.dtype)], axis=0).astype(BF16)
    return (wmain, bmain, wbg, bbg, walr, balr), wgate, bgate, wa2


def _layer(x, w_in, b_in, conv_w, conv_b, rg_w_a, rg_b_a, rg_w_x, rg_b_x, rg_lambda, gla_w_a2, gla_b_a,
           gla_norm_g, w_proj_rnn, w_proj_gla, w_o, b_o, ln1_g, ln1_b, router_w_group, router_b_group,
           router_w_expert, router_b_expert, exp_w1, exp_w3, exp_w2, ln2_g, ln2_b):
    B, S, _ = x.shape
    N = B * S
    row = lambda p: p[None, :]

    w_slices, wgate, bgate, wa2 = _pack_mixer_weights(w_in, b_in, rg_w_a, rg_w_x, rg_b_a, rg_b_x, gla_w_a2)
    wr = jnp.concatenate([router_w_group.T, jnp.zeros((SUBLANES - N_GROUPS, D_MODEL), F32), router_w_expert.T],
                         axis=0).astype(BF16)
    br = jnp.concatenate([router_b_group, jnp.zeros((SUBLANES - N_GROUPS,), F32), router_b_expert])[:, None]
    x1f, xpk, info, wts, tcnt = _mixer(
        x, *w_slices, conv_w, row(conv_b), wgate, bgate, row(rg_lambda), wa2, row(gla_b_a), row(gla_norm_g),
        _odd_tiles(w_proj_rnn), _odd_tiles(w_proj_gla), _odd_tiles(w_o), row(b_o), row(ln1_g), row(ln1_b), wr, br)

    tcnt = tcnt[:, :, 0]
    nt = tcnt.shape[0]
    tot = jnp.sum(tcnt, axis=0)
    pcount = (tot + MOE_ROWS - 1) // MOE_ROWS * MOE_ROWS
    pend = jnp.cumsum(pcount)
    base = (pend - pcount)[None, :] + jnp.cumsum(tcnt, axis=0) - tcnt
    base_tok = jnp.repeat(base.T, N // nt, axis=1)
    experts_col = jnp.arange(N_EXPERTS, dtype=jnp.int32)[:, None, None]
    dest = jnp.sum(jnp.where(info[None, 0:2] == experts_col, base_tok[:, None, :], 0), axis=0) + info[2:4]
    nb = -(-(2 * N) // MOE_ROWS) + N_EXPERTS
    P = nb * MOE_ROWS
    nblk = (pend[-1] // MOE_ROWS).astype(jnp.int32)
    blk_start = jnp.minimum(jnp.arange(nb + 1, dtype=jnp.int32), nblk - 1) * MOE_ROWS
    blk_e = jnp.sum((blk_start[:, None] >= pend[None, :]).astype(jnp.int32), axis=1)
    blk_e = jnp.minimum(blk_e, N_EXPERTS - 1)

    last_blk = jnp.where(tot > 0, pend - MOE_ROWS, -1)
    tail_blk = nblk + jnp.arange(N_EXPERTS, dtype=jnp.int32)
    tail_blk = jnp.where(tail_blk < nb, tail_blk * MOE_ROWS, -1)
    xs = _dispatch(jnp.concatenate([last_blk, tail_blk]).astype(jnp.int32), dest, xpk, P)
    blk_i = jnp.arange(nb + 1, dtype=jnp.int32)
    experts_row = jnp.arange(N_EXPERTS, dtype=jnp.int32)
    run_first = ((blk_i < nblk) & ((blk_i == 0) | (blk_e != jnp.roll(blk_e, 1)))).astype(jnp.int32)
    run_idx = jnp.sum(jnp.where(blk_i[None, :] <= blk_i[:, None], run_first[None, :], 0), axis=1) - 1
    later = (experts_row[None, :] > experts_row[:, None]) & (tot[None, :] > 0)
    next_nonempty = jnp.min(jnp.where(later, experts_row[None, :], N_EXPERTS), axis=1)
    next_nonempty = jnp.where(next_nonempty < N_EXPERTS, next_nonempty, -1)
    next_e = jnp.sum(jnp.where(blk_e[:, None] == experts_row[None, :], next_nonempty[None, :], 0), axis=1)
    yt = _experts(blk_e, nblk[None], run_first, run_idx.astype(jnp.int32), next_e.astype(jnp.int32),
                  xs, exp_w1, exp_w3, exp_w2, N)
    out = _combine(x1f, wts, yt, row(ln2_g), row(ln2_b))
    return out.reshape(B, S, D_MODEL)


def kernel(x, w_in, b_in, conv_w, conv_b, rg_w_a, rg_b_a, rg_w_x, rg_b_x, rg_lambda, gla_w_a2, gla_b_a, gla_norm_g, w_proj_rnn, w_proj_gla, w_o, b_o, ln1_g, ln1_b, router_w_group, router_b_group, router_w_expert, router_b_expert, exp_w1, exp_w3, exp_w2, ln2_g, ln2_b):
    h = x
    for l in range(w_in.shape[0]):
        h = _layer(h, w_in[l], b_in[l], conv_w[l], conv_b[l], rg_w_a[l], rg_b_a[l], rg_w_x[l], rg_b_x[l],
                   rg_lambda[l], gla_w_a2[l], gla_b_a[l], gla_norm_g[l], w_proj_rnn[l], w_proj_gla[l], w_o[l],
                   b_o[l], ln1_g[l], ln1_b[l], router_w_group[l], router_b_group[l], router_w_expert[l],
                   router_b_expert[l], exp_w1[l], exp_w3[l], exp_w2[l], ln2_g[l], ln2_b[l])
    return h
```
